```python
import math
import jax, jax.numpy as jnp
from jax import lax
import numpy as np

D_MODEL = 1024
BATCH = 1
SEQ = 16384
DEPTH = 1
DEC_BATCH = 16
DEC_SEQ = 16
PAST_LEN = 2048

CHUNK = 64
D_MIX = D_MODEL
N_HEADS = 8
HEAD_DIM = 64
ATT_WIDTH = N_HEADS * HEAD_DIM
N_IDX_HEADS = 4
IDX_DIM = 64
TOPK_MAX = 256
CONV_CH = D_MIX - ATT_WIDTH
CONV_WIDTH = 31
PEER_HEADS = 8
PEER_KEYS = 128
PEER_EXPERTS = PEER_KEYS * PEER_KEYS
PEER_QDIM = 128
PEER_TOPK = 16
Q_BLOCK = 128
PEER_BLOCK = 256
EPS = 1e-6
SPLIT_SIZES = (ATT_WIDTH, ATT_WIDTH, ATT_WIDTH, N_IDX_HEADS * IDX_DIM, IDX_DIM, N_IDX_HEADS, 2 * CONV_CH)
D_IN = 3 * ATT_WIDTH + N_IDX_HEADS * IDX_DIM + IDX_DIM + N_IDX_HEADS + 2 * CONV_CH

kernel_name = 'hymba_dsa_conformer_peer_stream_step'


def rmsnorm(x, g):
    xf = x.astype(jnp.float32)
    r = xf * lax.rsqrt(jnp.mean(xf * xf, axis=-1, keepdims=True) + EPS)
    return (r * g.astype(jnp.float32)).astype(x.dtype)


def layernorm(x, g, b):
    xf = x.astype(jnp.float32)
    mu = jnp.mean(xf, axis=-1, keepdims=True)
    var = jnp.mean(jnp.square(xf - mu), axis=-1, keepdims=True)
    r = (xf - mu) * lax.rsqrt(var + EPS)
    return (r * g.astype(jnp.float32) + b.astype(jnp.float32)).astype(x.dtype)


def split_cols(z):
    B, T = z.shape[:2]
    parts = []
    off = 0
    for n in SPLIT_SIZES:
        parts.append(z[..., off:off + n])
        off += n
    q, k, v, qi, ki, wi, glu = parts
    return (q.reshape(B, T, N_HEADS, HEAD_DIM), k.reshape(B, T, N_HEADS, HEAD_DIM),
            v.reshape(B, T, N_HEADS, HEAD_DIM), qi.reshape(B, T, N_IDX_HEADS, IDX_DIM), ki, wi, glu)


def indexer_scores(qi, wi, ki):
    s = jax.nn.relu(jnp.einsum('bthd,bld->bthl', qi, ki).astype(jnp.float32))
    w = wi.astype(jnp.float32) * (N_IDX_HEADS ** -0.5 * IDX_DIM ** -0.5)
    return jnp.einsum('bth,bthl->btl', w, s)


def sparse_attend(q, k_all, v_all, scores, topk):
    B = q.shape[0]
    top_vals, idx = lax.top_k(scores, topk)
    valid = jnp.isfinite(top_vals)
    bidx = jnp.arange(B)[:, None, None]
    ks = k_all[bidx, idx]
    vs = v_all[bidx, idx]
    logits = jnp.einsum('bthd,btkhd->bthk', q, ks).astype(jnp.float32) * (HEAD_DIM ** -0.5)
    logits = jnp.where(valid[:, :, None, :], logits, -jnp.inf)
    p = jax.nn.softmax(logits, axis=-1)
    o = jnp.einsum('bthk,btkhd->bthd', p.astype(vs.dtype), vs)
    return o.reshape(q.shape[0], q.shape[1], ATT_WIDTH)


def prompt_attention(q, k, v, qi, ki, wi):
    B, S = q.shape[:2]
    nb = S // Q_BLOCK
    topk = min(TOPK_MAX, S // 4)
    key_pos = jnp.arange(S)

    def to_blocks(a):
        return jnp.moveaxis(a.reshape((B, nb, Q_BLOCK) + a.shape[2:]), 1, 0)

    def block(args):
        qb, qib, wib, blk = args
        t = blk * Q_BLOCK + jnp.arange(Q_BLOCK)
        limit = (t // CHUNK + 1) * CHUNK
        adm = key_pos[None, :] < limit[:, None]
        sc = jnp.where(adm[None], indexer_scores(qib, wib, ki), -jnp.inf)
        return sparse_attend(qb, k, v, sc, topk)

    out = lax.map(block, (to_blocks(q), to_blocks(qi), to_blocks(wi), jnp.arange(nb)))
    return jnp.moveaxis(out, 0, 1).reshape(B, S, ATT_WIDTH)


def sample_attention(q, k_all, v_all, qi, ki_all, wi):
    L = k_all.shape[1]
    topk = min(TOPK_MAX, L // 4)
    return sparse_attend(q, k_all, v_all, indexer_scores(qi, wi, ki_all), topk)


def conformer_conv(glu, left, w, b, g, beta):
    u = glu[..., :CONV_CH] * jax.nn.sigmoid(glu[..., CONV_CH:])
    cat = jnp.concatenate([left, u], axis=1)
    y = lax.conv_general_dilated(cat, w[:, None, :].astype(cat.dtype), window_strides=(1,), padding='VALID',
                                 dimension_numbers=('NWC', 'WIO', 'NWC'), feature_group_count=CONV_CH)
    y = jax.nn.silu(layernorm(y + b, g, beta))
    return y, cat[:, -(CONV_WIDTH - 1):]


def peer(x, w_q, sub_keys, u_tab, v_tab):
    B, T, D = x.shape
    n = B * T
    pad = (-n) % PEER_BLOCK
    xf = jnp.pad(x.reshape(n, D), ((0, pad), (0, 0)))
    xb = xf.reshape(-1, PEER_BLOCK, D)
    half = PEER_QDIM // 2

    def block(xt):
        qh = (xt @ w_q).reshape(PEER_BLOCK, PEER_HEADS, PEER_QDIM)
        s1 = jnp.einsum('thd,kd->thk', qh[..., :half], sub_keys[0]).astype(jnp.float32)
        s2 = jnp.einsum('thd,kd->thk', qh[..., half:], sub_keys[1]).astype(jnp.float32)
        v1, i1 = lax.top_k(s1, PEER_TOPK)
        v2, i2 = lax.top_k(s2, PEER_TOPK)
        cand = (v1[..., :, None] + v2[..., None, :]).reshape(PEER_BLOCK, PEER_HEADS, PEER_TOPK * PEER_TOPK)
        cidx = (i1[..., :, None] * PEER_KEYS + i2[..., None, :]).reshape(PEER_BLOCK, PEER_HEADS, PEER_TOPK * PEER_TOPK)
        sv, pos = lax.top_k(cand, PEER_TOPK)
        eidx = jnp.take_along_axis(cidx, pos, axis=-1)
        gate = jax.nn.softmax(sv, axis=-1)
        ue = u_tab[eidx]
        ve = v_tab[eidx]
        a = jnp.einsum('thkd,td->thk', ue, xt).astype(jnp.float32)
        h = (jax.nn.gelu(a, approximate=False) * gate).astype(ve.dtype)
        return jnp.einsum('thk,thkd->td', h, ve)

    out = lax.map(block, xb).reshape(-1, D)[:n]
    return out.reshape(B, T, D)


def setup_inputs(seed: int = 0) -> dict:
    key = jax.random.key(seed)
    ks = jax.random.split(key, 20)
    f32 = jnp.float32
    nrm = lambda k, s, sc: jax.random.normal(k, s, f32) * sc
    return {
        'x_prompt': nrm(ks[0], (BATCH, SEQ, D_MODEL), 1.0),
        'x_sample': nrm(ks[1], (DEC_BATCH, DEC_SEQ, D_MODEL), 1.0),
        'cache_k': nrm(ks[2], (DEPTH, DEC_BATCH, PAST_LEN, N_HEADS, HEAD_DIM), 1.0),
        'cache_v': nrm(ks[3], (DEPTH, DEC_BATCH, PAST_LEN, N_HEADS, HEAD_DIM), 1.0),
        'cache_kidx': nrm(ks[4], (DEPTH, DEC_BATCH, PAST_LEN, IDX_DIM), 1.0),
        'state_conv': nrm(ks[5], (DEPTH, DEC_BATCH, CONV_WIDTH - 1, CONV_CH), 0.5),
        'attn_norm_g': 1.0 + nrm(ks[6], (DEPTH, D_MODEL), 0.01),
        'w_in': nrm(ks[7], (DEPTH, D_MODEL, D_IN), D_MODEL ** -0.5),
        'conv_w': nrm(ks[8], (DEPTH, CONV_WIDTH, CONV_CH), CONV_WIDTH ** -0.5),
        'conv_b': nrm(ks[9], (DEPTH, CONV_CH), 0.01),
        'conv_ln_g': 1.0 + nrm(ks[10], (DEPTH, CONV_CH), 0.01),
        'conv_ln_b': nrm(ks[11], (DEPTH, CONV_CH), 0.01),
        'w_out': nrm(ks[12], (DEPTH, D_MIX, D_MODEL), D_MIX ** -0.5),
        'ffn_norm_g': 1.0 + nrm(ks[13], (DEPTH, D_MODEL), 0.01),
        'peer_wq': nrm(ks[14], (DEPTH, D_MODEL, PEER_HEADS * PEER_QDIM), D_MODEL ** -0.5),
        'peer_subkeys': nrm(ks[15], (DEPTH, 2, PEER_KEYS, PEER_QDIM // 2), (PEER_QDIM // 2) ** -0.5),
        'peer_u': nrm(ks[16], (DEPTH, PEER_EXPERTS, D_MODEL), D_MODEL ** -0.5),
        'peer_v': nrm(ks[17], (DEPTH, PEER_EXPERTS, D_MODEL), PEER_HEADS ** -0.5),
        'final_norm_g': 1.0 + nrm(ks[18], (D_MODEL,), 0.01),
    }


def reference(x_prompt, x_sample, cache_k, cache_v, cache_kidx, state_conv, attn_norm_g, w_in, conv_w, conv_b,
              conv_ln_g, conv_ln_b, w_out, ffn_norm_g, peer_wq, peer_subkeys, peer_u, peer_v, final_norm_g):
    hp, hs = x_prompt, x_sample
    kp_l, vp_l, kip_l, cp_l = [], [], [], []
    ks_l, vs_l, kis_l, cs_l = [], [], [], []
    for l in range(DEPTH):
        qp, kp, vp, qip, kip, wip, glup = split_cols(rmsnorm(hp, attn_norm_g[l]) @ w_in[l])
        att_p = prompt_attention(qp, kp, vp, qip, kip, wip)
        left_p = jnp.zeros((hp.shape[0], CONV_WIDTH - 1, CONV_CH), glup.dtype)
        conv_p, cst_p = conformer_conv(glup, left_p, conv_w[l], conv_b[l], conv_ln_g[l], conv_ln_b[l])
        hp = hp + jnp.concatenate([att_p, conv_p], axis=-1) @ w_out[l]
        hp = hp + peer(rmsnorm(hp, ffn_norm_g[l]), peer_wq[l], peer_subkeys[l], peer_u[l], peer_v[l])
        qs, kS, vS, qis, kis, wis, glus = split_cols(rmsnorm(hs, attn_norm_g[l]) @ w_in[l])
        k_all = jnp.concatenate([cache_k[l], kS], axis=1)
        v_all = jnp.concatenate([cache_v[l], vS], axis=1)
        ki_all = jnp.concatenate([cache_kidx[l], kis], axis=1)
        att_s = sample_attention(qs, k_all, v_all, qis, ki_all, wis)
        conv_s, cst_s = conformer_conv(glus, state_conv[l], conv_w[l], conv_b[l], conv_ln_g[l], conv_ln_b[l])
        hs = hs + jnp.concatenate([att_s, conv_s], axis=-1) @ w_out[l]
        hs = hs + peer(rmsnorm(hs, ffn_norm_g[l]), peer_wq[l], peer_subkeys[l], peer_u[l], peer_v[l])
        kp_l.append(kp); vp_l.append(vp); kip_l.append(kip); cp_l.append(cst_p)
        ks_l.append(kS); vs_l.append(vS); kis_l.append(kis); cs_l.append(cst_s)
    y_prompt = rmsnorm(hp, final_norm_g)
    y_sample = rmsnorm(hs, final_norm_g)
    new_k_prompt = jnp.stack(kp_l, 0)
    new_v_prompt = jnp.stack(vp_l, 0)
    new_kidx_prompt = jnp.stack(kip_l, 0)
    new_conv_prompt = jnp.stack(cp_l, 0)
    new_k_sample = jnp.stack(ks_l, 0)
    new_v_sample = jnp.stack(vs_l, 0)
    new_kidx_sample = jnp.stack(kis_l, 0)
    new_conv_sample = jnp.stack(cs_l, 0)
    return (y_prompt, y_sample, new_k_prompt, new_v_prompt, new_kidx_prompt, new_conv_prompt,
            new_k_sample, new_v_sample, new_kidx_sample, new_conv_sample)
```

```python
import functools

import jax
import jax.numpy as jnp
from jax import lax
from jax.experimental import pallas as pl
from jax.experimental.pallas import tpu as pltpu

F32 = jnp.float32
BF16 = jnp.bfloat16
I32 = jnp.int32

D_MODEL = 1024
N_HEADS = 8
HEAD_DIM = 64
ATT_WIDTH = N_HEADS * HEAD_DIM
N_IDX_HEADS = 4
IDX_DIM = 64
TOPK = 256
CHUNK_SHIFT = 6
CONV_CH = 512
CONV_WIDTH = 31
CONV_TAIL = 32
PEER_HEADS = 8
PEER_KEYS = 128
PEER_HALF = 64
PEER_TOPK = 16
EPS = 1e-6
INT_MIN = -2147483648
ORDER_MASK = 0x7FFFFFFF
INV_SQRT2 = 0.7071067811865476
NEG_INF = float("-inf")
M_INIT = -1e30

VMEM_LIMIT = 56 * 1024 * 1024


def _params(sem, vmem=VMEM_LIMIT):
    return pltpu.CompilerParams(dimension_semantics=sem, vmem_limit_bytes=vmem)


def _nt(a, b):
    return lax.dot_general(a, b, (((1,), (1,)), ((), ())), preferred_element_type=F32)


def _dot(a, b):
    return jnp.dot(a, b, preferred_element_type=F32)


def _rms(x, g):
    return x * lax.rsqrt(jnp.mean(x * x, axis=-1, keepdims=True) + EPS) * g


def _order_key(x):
    b = pltpu.bitcast(x, I32)
    return b ^ ((b >> 31) & ORDER_MASK)


def _order_unkey(k):
    return pltpu.bitcast(k ^ ((k >> 31) & ORDER_MASK), F32)


def _in_proj_body(x_ref, g_ref, wqkv_ref, widx_ref, wglu_ref,
                  q_ref, k_ref, v_ref, qi_ref, kw_ref, glu_ref):
    xn = _rms(x_ref[...], g_ref[...]).astype(BF16)
    qkv = _dot(xn, wqkv_ref[...])
    q_ref[...] = (qkv[:, :ATT_WIDTH] * (HEAD_DIM ** -0.5)).astype(BF16)
    k_ref[...] = qkv[:, ATT_WIDTH:2 * ATT_WIDTH]
    v_ref[...] = qkv[:, 2 * ATT_WIDTH:]
    ix = _dot(xn, widx_ref[...])
    qi_ref[...] = ix[:, :N_IDX_HEADS * IDX_DIM].astype(BF16)
    kw_ref[...] = ix[:, N_IDX_HEADS * IDX_DIM:]
    glu_ref[...] = _dot(xn, wglu_ref[...])


def _in_proj(x, g, wqkv, widx, wglu, tb):
    t = x.shape[0]
    row = lambda w: pl.BlockSpec((tb, w), lambda i: (i, 0))
    full = lambda a: pl.BlockSpec(a.shape, lambda i: (0, 0))
    return pl.pallas_call(
        _in_proj_body,
        grid=(t // tb,),
        in_specs=[row(D_MODEL), full(g), full(wqkv), full(widx), full(wglu)],
        out_specs=[row(ATT_WIDTH), row(ATT_WIDTH), row(ATT_WIDTH), row(256), row(128), row(2 * CONV_CH)],
        out_shape=[jax.ShapeDtypeStruct((t, ATT_WIDTH), BF16),
                   jax.ShapeDtypeStruct((t, ATT_WIDTH), F32),
                   jax.ShapeDtypeStruct((t, ATT_WIDTH), F32),
                   jax.ShapeDtypeStruct((t, 256), BF16),
                   jax.ShapeDtypeStruct((t, 128), F32),
                   jax.ShapeDtypeStruct((t, 2 * CONV_CH), F32)],
        compiler_params=_params(("parallel",)),
        name="in_proj",
    )(x, g, wqkv, widx, wglu)


def _conv_body(glu_ref, tail0_ref, w_ref, b_ref, g_ref, beta_ref, out_ref, tail_ref, ubuf, *, tb):
    j = pl.program_id(1)

    @pl.when(j == 0)
    def _():
        ubuf[0:CONV_TAIL, :] = tail0_ref[0]

    @pl.when(j > 0)
    def _():
        ubuf[0:CONV_TAIL, :] = ubuf[tb:tb + CONV_TAIL, :]

    glu = glu_ref[...]
    ubuf[CONV_TAIL:CONV_TAIL + tb, :] = glu[:, :CONV_CH] * jax.nn.sigmoid(glu[:, CONV_CH:])
    off = CONV_TAIL - (CONV_WIDTH - 1)
    acc = ubuf[off:off + tb, :] * w_ref[0:1, :]
    for t in range(1, CONV_WIDTH):
        acc = acc + ubuf[off + t:off + t + tb, :] * w_ref[t:t + 1, :]
    y = acc + b_ref[...]
    mu = jnp.mean(y, axis=-1, keepdims=True)
    d = y - mu
    var = jnp.mean(d * d, axis=-1, keepdims=True)
    z = d * lax.rsqrt(var + EPS) * g_ref[...] + beta_ref[...]
    out_ref[...] = (z * jax.nn.sigmoid(z)).astype(BF16)
    tail_ref[0] = ubuf[tb:tb + CONV_TAIL, :]


def _conv(glu, tail0, w, b, g, beta, nseq, tb):
    t = glu.shape[0]
    nblk = t // (nseq * tb)
    full = lambda a: pl.BlockSpec(a.shape, lambda s, j: (0, 0))
    return pl.pallas_call(
        functools.partial(_conv_body, tb=tb),
        grid=(nseq, nblk),
        in_specs=[pl.BlockSpec((tb, 2 * CONV_CH), lambda s, j: (s * nblk + j, 0)),
                  pl.BlockSpec((1, CONV_TAIL, CONV_CH), lambda s, j: (s, 0, 0)),
                  full(w), full(b), full(g), full(beta)],
        out_specs=[pl.BlockSpec((tb, CONV_CH), lambda s, j: (s * nblk + j, 0)),
                   pl.BlockSpec((1, CONV_TAIL, CONV_CH), lambda s, j: (s, 0, 0))],
        out_shape=[jax.ShapeDtypeStruct((t, CONV_CH), BF16),
                   jax.ShapeDtypeStruct((nseq, CONV_TAIL, CONV_CH), F32)],
        scratch_shapes=[pltpu.VMEM((tb + CONV_TAIL, CONV_CH), F32)],
        compiler_params=_params(("parallel", "arbitrary")),
        name="conv",
    )(glu, tail0, w, b, g, beta)


def _idx_scores_t(ki_tile, qi_ref, w):
    acc = None
    for h in range(N_IDX_HEADS):
        term = w[h:h + 1, :] * jnp.maximum(_nt(ki_tile, qi_ref[h]), 0.0)
        acc = term if acc is None else acc + term
    return acc


def _chunk_limit(q0, tq):
    qpos = q0 + lax.broadcasted_iota(I32, (1, tq), 1)
    return ((qpos >> CHUNK_SHIFT) + 1) << CHUNK_SHIFT


def _thr_body(qi_ref, w_ref, ki_ref, thr_ref, cut_ref, keys_ref, *, tq):
    i = pl.program_id(0)
    nt = i + 1
    w = w_ref[...]
    limit = _chunk_limit(i * tq, tq)
    rows = lax.broadcasted_iota(I32, (tq, tq), 0)

    def fill(t, c):
        r0 = pl.multiple_of(t * tq, tq)
        s = _idx_scores_t(ki_ref[pl.ds(r0, tq), :], qi_ref, w)
        keys_ref[pl.ds(r0, tq), :] = jnp.where(rows + r0 < limit, _order_key(s), INT_MIN)
        return c

    lax.fori_loop(0, nt, fill, 0)

    def count(pred):
        def body(t, acc):
            blk = keys_ref[pl.ds(pl.multiple_of(t * tq, tq), tq), :]
            hit = jnp.where(pred(blk), 1, 0).reshape(tq // 8, 8, tq)
            return acc + jnp.sum(hit, axis=0)
        acc = lax.fori_loop(0, nt, body, jnp.zeros((8, tq), I32))
        return jnp.sum(acc, axis=0, keepdims=True)

    def bit_step(p, u):
        cand_u = u | (jnp.int32(1) << (31 - p))
        cand = cand_u ^ INT_MIN
        return jnp.where(count(lambda blk: blk >= cand) >= TOPK, cand_u, u)

    kth = lax.fori_loop(0, 32, bit_step, jnp.zeros((1, tq), I32)) ^ INT_MIN
    need = (TOPK - count(lambda blk: blk > kth)).astype(F32)

    tri = jnp.where(lax.broadcasted_iota(I32, (tq, tq), 1) <= rows, 1.0, 0.0).astype(BF16)

    def tie(t, carry):
        seen, cut = carry
        r0 = pl.multiple_of(t * tq, tq)
        eq = keys_ref[pl.ds(r0, tq), :] == kth
        pre = _dot(tri, jnp.where(eq, 1.0, 0.0).astype(BF16))
        last = jnp.where(eq & (pre + seen == need), rows + r0, -1)
        return seen + pre[tq - 1:tq, :], jnp.maximum(cut, jnp.max(last, axis=0, keepdims=True))

    _, cut = lax.fori_loop(0, nt, tie, (jnp.zeros((1, tq), F32), jnp.full((1, tq), -1, I32)))
    thr_ref[...] = jnp.where(kth == INT_MIN, NEG_INF, _order_unkey(kth))
    cut_ref[...] = cut


def _idx_thr(qi_hm, w_t, ki, tq):
    s = ki.shape[0]
    return pl.pallas_call(
        functools.partial(_thr_body, tq=tq),
        grid=(s // tq,),
        in_specs=[pl.BlockSpec((N_IDX_HEADS, tq, IDX_DIM), lambda i: (0, i, 0)),
                  pl.BlockSpec((N_IDX_HEADS, tq), lambda i: (0, i)),
                  pl.BlockSpec((s, IDX_DIM), lambda i: (0, 0))],
        out_specs=[pl.BlockSpec((1, tq), lambda i: (0, i)),
                   pl.BlockSpec((1, tq), lambda i: (0, i))],
        out_shape=[jax.ShapeDtypeStruct((1, s), F32), jax.ShapeDtypeStruct((1, s), I32)],
        scratch_shapes=[pltpu.VMEM((s, tq), I32)],
        compiler_params=_params(("arbitrary",)),
        name="idx_thr",
    )(qi_hm, w_t, ki)


def _attn_body(qb_ref, kb_ref, q_ref, k_ref, vt_ref, qi_ref, w_ref, ki_ref, thr_ref, cut_ref,
               out_ref, m_ref, l_ref, acc_ref, *, tq, tk):
    step = pl.program_id(0)
    i = qb_ref[step]
    j = kb_ref[step]

    @pl.when(j == 0)
    def _():
        m_ref[...] = jnp.full(m_ref.shape, M_INIT, F32)
        l_ref[...] = jnp.zeros(l_ref.shape, F32)
        acc_ref[...] = jnp.zeros(acc_ref.shape, F32)

    sc = _idx_scores_t(ki_ref[...], qi_ref, w_ref[...])
    kpos = j * tk + lax.broadcasted_iota(I32, (tk, tq), 0)
    thr = thr_ref[...]
    sel = (kpos < _chunk_limit(i * tq, tq)) & ((sc > thr) | ((sc == thr) & (kpos <= cut_ref[...])))
    bias = jnp.where(sel, 0.0, NEG_INF)
    for h in range(N_HEADS):
        s = _nt(k_ref[h], q_ref[h]) + bias
        m_old = m_ref[h:h + 1, :]
        m_new = jnp.maximum(m_old, jnp.max(s, axis=0, keepdims=True))
        alpha = jnp.exp(m_old - m_new)
        p = jnp.exp(s - m_new)
        l_ref[h:h + 1, :] = alpha * l_ref[h:h + 1, :] + jnp.sum(p, axis=0, keepdims=True)
        acc_ref[h] = alpha * acc_ref[h] + _dot(vt_ref[h], p.astype(BF16))
        m_ref[h:h + 1, :] = m_new

    @pl.when(j == ((i + 1) * tq - 1) // tk)
    def _():
        for h in range(N_HEADS):
            out_ref[h * HEAD_DIM:(h + 1) * HEAD_DIM, :] = (acc_ref[h] / l_ref[h:h + 1, :]).astype(BF16)


def _attn(q_hm, k_hm, vt_hm, qi_hm, w_t, ki, thr, cut, tq, tk):
    s = ki.shape[0]
    qb, kb = [], []
    for i in range(s // tq):
        for j in range(((i + 1) * tq - 1) // tk + 1):
            qb.append(i)
            kb.append(j)
    qb = jnp.asarray(qb, I32)
    kb = jnp.asarray(kb, I32)
    grid_spec = pltpu.PrefetchScalarGridSpec(
        num_scalar_prefetch=2,
        grid=(int(qb.shape[0]),),
        in_specs=[pl.BlockSpec((N_HEADS, tq, HEAD_DIM), lambda t, qb, kb: (0, qb[t], 0)),
                  pl.BlockSpec((N_HEADS, tk, HEAD_DIM), lambda t, qb, kb: (0, kb[t], 0)),
                  pl.BlockSpec((N_HEADS, HEAD_DIM, tk), lambda t, qb, kb: (0, 0, kb[t])),
                  pl.BlockSpec((N_IDX_HEADS, tq, IDX_DIM), lambda t, qb, kb: (0, qb[t], 0)),
                  pl.BlockSpec((N_IDX_HEADS, tq), lambda t, qb, kb: (0, qb[t])),
                  pl.BlockSpec((tk, IDX_DIM), lambda t, qb, kb: (kb[t], 0)),
                  pl.BlockSpec((1, tq), lambda t, qb, kb: (0, qb[t])),
                  pl.BlockSpec((1, tq), lambda t, qb, kb: (0, qb[t]))],
        out_specs=pl.BlockSpec((ATT_WIDTH, tq), lambda t, qb, kb: (0, qb[t])),
        scratch_shapes=[pltpu.VMEM((N_HEADS, tq), F32),
                        pltpu.VMEM((N_HEADS, tq), F32),
                        pltpu.VMEM((N_HEADS, HEAD_DIM, tq), F32)],
    )
    return pl.pallas_call(
        functools.partial(_attn_body, tq=tq, tk=tk),
        grid_spec=grid_spec,
        out_shape=jax.ShapeDtypeStruct((ATT_WIDTH, s), BF16),
        compiler_params=_params(("arbitrary",)),
        name="attn",
    )(qb, kb, q_hm, k_hm, vt_hm, qi_hm, w_t, ki, thr, cut)


NEW_PAD = 128


def _sample_attn_body(q_ref, qi_ref, w_ref, kic_ref, kin_ref, kc_ref, vc_ref, kn_ref, vn_ref,
                      out_ref, kinp, knp, vnp, *, past, tn):
    kinp[...] = jnp.zeros(kinp.shape, BF16)
    kinp[0:tn, :] = kin_ref[0].astype(BF16)
    knp[...] = jnp.zeros(knp.shape, BF16)
    vnp[...] = jnp.zeros(vnp.shape, BF16)
    for h in range(N_HEADS):
        knp[h, 0:tn, :] = kn_ref[0, h]
        vnp[h, 0:tn, :] = vn_ref[0, h]

    w = w_ref[0]
    kic = kic_ref[0].astype(BF16)
    kin = kinp[...]
    sc = None
    sn = None
    for h in range(N_IDX_HEADS):
        qih = qi_ref[0, h]
        wc = w[:, h:h + 1]
        tc = wc * jnp.maximum(_nt(qih, kic), 0.0)
        tnw = wc * jnp.maximum(_nt(qih, kin), 0.0)
        sc = tc if sc is None else sc + tc
        sn = tnw if sn is None else sn + tnw
    new_ok = lax.broadcasted_iota(I32, (tn, NEW_PAD), 1) < tn
    keyc = _order_key(sc)
    keyn = jnp.where(new_ok, _order_key(sn), INT_MIN)

    def count(pc, pn):
        return (jnp.sum(jnp.where(pc, 1, 0), axis=1, keepdims=True)
                + jnp.sum(jnp.where(pn, 1, 0), axis=1, keepdims=True))

    def bit_step(p, u):
        cand_u = u | (jnp.int32(1) << (31 - p))
        cand = cand_u ^ INT_MIN
        return jnp.where(count(keyc >= cand, keyn >= cand) >= TOPK, cand_u, u)

    kth = lax.fori_loop(0, 32, bit_step, jnp.zeros((tn, 1), I32)) ^ INT_MIN
    need = (TOPK - count(keyc > kth, keyn > kth)).astype(F32)

    cw = 256
    tri = jnp.where(lax.broadcasted_iota(I32, (cw, cw), 0) <= lax.broadcasted_iota(I32, (cw, cw), 1),
                    1.0, 0.0).astype(BF16)
    seen = jnp.zeros((tn, 1), F32)
    bias_c = []
    for c in range(past // cw):
        kc_ = keyc[:, c * cw:(c + 1) * cw]
        eq = kc_ == kth
        pre = _dot(jnp.where(eq, 1.0, 0.0).astype(BF16), tri)
        sel = (kc_ > kth) | (eq & (pre + seen <= need))
        bias_c.append(jnp.where(sel, 0.0, NEG_INF))
        seen = seen + pre[:, cw - 1:cw]
    bias_c = jnp.concatenate(bias_c, axis=1)
    eqn = keyn == kth
    pren = _dot(jnp.where(eqn, 1.0, 0.0).astype(BF16), tri[:NEW_PAD, :NEW_PAD])
    bias_n = jnp.where(new_ok & ((keyn > kth) | (eqn & (pren + seen <= need))), 0.0, NEG_INF)

    for h in range(N_HEADS):
        qh = q_ref[0, h]
        s_c = _nt(qh, kc_ref[0, h]) + bias_c
        s_n = _nt(qh, knp[h]) + bias_n
        m = jnp.maximum(jnp.max(s_c, axis=1, keepdims=True), jnp.max(s_n, axis=1, keepdims=True))
        p_c = jnp.exp(s_c - m)
        p_n = jnp.exp(s_n - m)
        l = jnp.sum(p_c, axis=1, keepdims=True) + jnp.sum(p_n, axis=1, keepdims=True)
        o = _dot(p_c.astype(BF16), vc_ref[0, h]) + _dot(p_n.astype(BF16), vnp[h])
        out_ref[0, h] = (o / l).astype(BF16)


def _sample_attn(q_hm, qi_hm, w, kic, kin, kc_hm, vc_hm, kn_hm, vn_hm):
    nb, _, tn, _ = q_hm.shape
    past = kic.shape[1]
    b4 = lambda a: pl.BlockSpec((1,) + a.shape[1:], lambda b: (b, 0, 0, 0))
    b3 = lambda a: pl.BlockSpec((1,) + a.shape[1:], lambda b: (b, 0, 0))
    return pl.pallas_call(
        functools.partial(_sample_attn_body, past=past, tn=tn),
        grid=(nb,),
        in_specs=[b4(q_hm), b4(qi_hm), b3(w), b3(kic), b3(kin), b4(kc_hm), b4(vc_hm), b4(kn_hm), b4(vn_hm)],
        out_specs=pl.BlockSpec((1, N_HEADS, tn, HEAD_DIM), lambda b: (b, 0, 0, 0)),
        out_shape=jax.ShapeDtypeStruct((nb, N_HEADS, tn, HEAD_DIM), BF16),
        scratch_shapes=[pltpu.VMEM((NEW_PAD, IDX_DIM), BF16),
                        pltpu.VMEM((N_HEADS, NEW_PAD, HEAD_DIM), BF16),
                        pltpu.VMEM((N_HEADS, NEW_PAD, HEAD_DIM), BF16)],
        compiler_params=_params(("parallel",)),
        name="sample_attn",
    )(q_hm, qi_hm, w, kic, kin, kc_hm, vc_hm, kn_hm, vn_hm)


def _cmpx(a, b):
    return jnp.maximum(a, b), jnp.minimum(a, b)


def _bitonic_merge_desc(v):
    n = len(v)
    v = list(v)
    j = n // 2
    while j >= 1:
        for i in range(n):
            l = i ^ j
            if l > i:
                v[i], v[l] = _cmpx(v[i], v[l])
        j //= 2
    return v


def _bitonic_sort_desc(v):
    n = len(v)
    v = list(v)
    k = 2
    while k <= n:
        j = k // 2
        while j >= 1:
            for i in range(n):
                l = i ^ j
                if l > i:
                    hi, lo = _cmpx(v[i], v[l])
                    v[i], v[l] = (hi, lo) if (i & k) == 0 else (lo, hi)
            j //= 2
        k *= 2
    return v


def _merge_top(a, b):
    n = len(a)
    return _bitonic_merge_desc([jnp.maximum(a[k], b[n - 1 - k]) for k in range(n)])


def _top16_desc(vals):
    groups = [_bitonic_sort_desc(vals[g:g + PEER_TOPK]) for g in range(0, len(vals), PEER_TOPK)]
    while len(groups) > 1:
        groups = [_merge_top(groups[g], groups[g + 1]) for g in range(0, len(groups), 2)]
    return groups[0]


def _mid_body(x_ref, att_ref, conv_ref, woa_ref, woc_ref, g_ref, wqt_ref, a1_ref, a2_ref,
              h_ref, xn_ref, qht_ref, st_ref):
    h = x_ref[...] + _dot(att_ref[...], woa_ref[...]) + _dot(conv_ref[...], woc_ref[...])
    h_ref[...] = h
    xn = _rms(h, g_ref[...]).astype(BF16)
    xn_ref[...] = xn
    qht = _nt(wqt_ref[...], xn).astype(BF16)
    qht_ref[...] = qht
    half = PEER_HEADS * PEER_HALF
    s1 = _dot(a1_ref[...], qht[:half])
    s2 = _dot(a2_ref[...], qht[half:])
    rows = lambda s: [s[r * PEER_HEADS:(r + 1) * PEER_HEADS, :] for r in range(PEER_KEYS)]
    v1 = _top16_desc(rows(s1))
    v2 = _top16_desc(rows(s2))
    ninf = jnp.full(v1[0].shape, NEG_INF, F32)
    top = [v1[0] + v2[b] for b in range(PEER_TOPK)]
    for a in range(1, PEER_TOPK):
        n_a = PEER_TOPK // (a + 1)
        top = _merge_top(top, [v1[a] + v2[b] if b < n_a else ninf for b in range(PEER_TOPK)])
    z = jnp.exp(top[0] - top[0])
    for k in range(1, PEER_TOPK):
        z = z + jnp.exp(top[k] - top[0])
    st_ref[0:8, :] = v1[0]
    st_ref[8:16, :] = v2[0]
    st_ref[16:24, :] = top[PEER_TOPK - 1]
    st_ref[24:32, :] = 1.0 / z


def _mid(x, att, conv, woa, woc, g, wqt, a1, a2, tb):
    t = x.shape[0]
    row = lambda w: pl.BlockSpec((tb, w), lambda i: (i, 0))
    col = lambda r: pl.BlockSpec((r, tb), lambda i: (0, i))
    full = lambda a: pl.BlockSpec(a.shape, lambda i: (0, 0))
    return pl.pallas_call(
        _mid_body,
        grid=(t // tb,),
        in_specs=[row(D_MODEL), row(ATT_WIDTH), row(CONV_CH), full(woa), full(woc), full(g),
                  full(wqt), full(a1), full(a2)],
        out_specs=[row(D_MODEL), row(D_MODEL), col(D_MODEL), col(32)],
        out_shape=[jax.ShapeDtypeStruct((t, D_MODEL), F32),
                   jax.ShapeDtypeStruct((t, D_MODEL), BF16),
                   jax.ShapeDtypeStruct((D_MODEL, t), BF16),
                   jax.ShapeDtypeStruct((32, t), F32)],
        compiler_params=_params(("parallel",)),
        name="mid",
    )(x, att, conv, woa, woc, g, wqt, a1, a2)


def _peer_body(xn_ref, qht_ref, st_ref, a1_ref, a2_ref, u_ref, vt_ref, out_ref,
               s1_ref, s2_ref, e1_ref, e2_ref, hid_ref, *, eb):
    e = pl.program_id(1)
    nsub = eb // PEER_KEYS
    half = PEER_HEADS * PEER_HALF

    @pl.when(e == 0)
    def _():
        qht = qht_ref[...]
        s1 = _dot(a1_ref[...], qht[:half])
        s2 = _dot(a2_ref[...], qht[half:])
        s1_ref[...] = s1
        s2_ref[...] = s2
        for h in range(PEER_HEADS):
            r = slice(h * PEER_KEYS, (h + 1) * PEER_KEYS)
            e1_ref[r, :] = jnp.exp(s1[r] - st_ref[h:h + 1, :])
            e2_ref[r, :] = jnp.exp(s2[r] - st_ref[8 + h:9 + h, :]) * st_ref[24 + h:25 + h, :]
        out_ref[...] = jnp.zeros(out_ref.shape, F32)

    xn = xn_ref[...]
    for ii in range(nsub):
        i = e * nsub + ii
        a = _nt(u_ref[ii * PEER_KEYS:(ii + 1) * PEER_KEYS, :], xn)
        gate = None
        for h in range(PEER_HEADS):
            r = slice(h * PEER_KEYS, (h + 1) * PEER_KEYS)
            pair = s1_ref[pl.ds(h * PEER_KEYS + i, 1), :] + s2_ref[r, :]
            g = e1_ref[pl.ds(h * PEER_KEYS + i, 1), :] * e2_ref[r, :]
            g = jnp.where(pair >= st_ref[16 + h:17 + h, :], g, 0.0)
            gate = g if gate is None else gate + g
        hid = 0.5 * a * (1.0 + lax.erf(a * INV_SQRT2)) * gate
        hid_ref[ii * PEER_KEYS:(ii + 1) * PEER_KEYS, :] = hid.astype(BF16)
    out_ref[...] += _dot(vt_ref[...], hid_ref[...])


def _peer(xn, qht, st, a1, a2, u, vt, tb, eb):
    t = xn.shape[0]
    ne = u.shape[0]
    full = lambda a: pl.BlockSpec(a.shape, lambda i, e: (0, 0))
    return pl.pallas_call(
        functools.partial(_peer_body, eb=eb),
        grid=(t // tb, ne // eb),
        in_specs=[pl.BlockSpec((tb, D_MODEL), lambda i, e: (i, 0)),
                  pl.BlockSpec((D_MODEL, tb), lambda i, e: (0, i)),
                  pl.BlockSpec((32, tb), lambda i, e: (0, i)),
                  full(a1), full(a2),
                  pl.BlockSpec((eb, D_MODEL), lambda i, e: (e, 0)),
                  pl.BlockSpec((D_MODEL, eb), lambda i, e: (0, e))],
        out_specs=pl.BlockSpec((D_MODEL, tb), lambda i, e: (0, i)),
        out_shape=jax.ShapeDtypeStruct((D_MODEL, t), F32),
        scratch_shapes=[pltpu.VMEM((PEER_HEADS * PEER_KEYS, tb), F32)] * 4
                       + [pltpu.VMEM((eb, tb), BF16)],
        compiler_params=_params(("parallel", "arbitrary")),
        name="peer",
    )(xn, qht, st, a1, a2, u, vt)


def _final_body(h_ref, pt_ref, g_ref, y_ref):
    y_ref[...] = _rms(h_ref[...] + pt_ref[...].T, g_ref[...])


def _final(h, pt, g, tb):
    t = h.shape[0]
    return pl.pallas_call(
        _final_body,
        grid=(t // tb,),
        in_specs=[pl.BlockSpec((tb, D_MODEL), lambda i: (i, 0)),
                  pl.BlockSpec((D_MODEL, tb), lambda i: (0, i)),
                  pl.BlockSpec((1, D_MODEL), lambda i: (0, 0))],
        out_specs=pl.BlockSpec((tb, D_MODEL), lambda i: (i, 0)),
        out_shape=jax.ShapeDtypeStruct((t, D_MODEL), F32),
        compiler_params=_params(("parallel",)),
        name="final",
    )(h, pt, g)


def _pick(n, pref):
    b = min(n, pref)
    while n % b:
        b //= 2
    return b


def _head_major(a, nh, dt=BF16):
    t = a.shape[0]
    return a.reshape(t, nh, a.shape[1] // nh).transpose(1, 0, 2).astype(dt)


def _token_stage(x, att, conv, p, tb_mid, tb_peer):
    h, xn, qht, st = _mid(x, att, conv, p["woa"], p["woc"], p["ffn_g"], p["wqt"], p["a1s"], p["a2s"], tb_mid)
    pt = _peer(xn, qht, st, p["a1d"], p["a2d"], p["u"], p["vt"], tb_peer, 1024)
    return _final(h, pt, p["final_g"], tb_mid)


def kernel(x_prompt, x_sample, cache_k, cache_v, cache_kidx, state_conv, attn_norm_g, w_in, conv_w, conv_b,
           conv_ln_g, conv_ln_b, w_out, ffn_norm_g, peer_wq, peer_subkeys, peer_u, peer_v, final_norm_g):
    assert w_in.shape[0] == 1, "single layer"
    nbp, s, _ = x_prompt.shape
    nb, tn, _ = x_sample.shape
    assert nbp == 1
    past = cache_k.shape[2]

    wi = w_in[0]
    o_qi = 3 * ATT_WIDTH
    o_ki = o_qi + N_IDX_HEADS * IDX_DIM
    o_wi = o_ki + IDX_DIM
    o_glu = o_wi + N_IDX_HEADS
    wqkv = wi[:, :o_qi].astype(BF16)
    widx = jnp.concatenate([wi[:, o_qi:o_glu], jnp.zeros((D_MODEL, 128 - IDX_DIM - N_IDX_HEADS), F32)],
                           axis=1).astype(BF16)
    wglu = wi[:, o_glu:].astype(BF16)
    g_attn = attn_norm_g[0][None, :]
    cw = jnp.concatenate([conv_w[0], jnp.zeros((CONV_TAIL - CONV_WIDTH, CONV_CH), F32)], axis=0)
    cb, cg, cbeta = conv_b[0][None, :], conv_ln_g[0][None, :], conv_ln_b[0][None, :]
    eye = jnp.eye(PEER_HEADS, dtype=F32)
    sk = peer_subkeys[0]
    a_s = [jnp.einsum("id,hg->ihgd", sk[c], eye).reshape(PEER_KEYS * PEER_HEADS, PEER_HEADS * PEER_HALF)
           .astype(BF16) for c in range(2)]
    a_d = [jnp.einsum("id,hg->higd", sk[c], eye).reshape(PEER_KEYS * PEER_HEADS, PEER_HEADS * PEER_HALF)
           .astype(BF16) for c in range(2)]
    wq = peer_wq[0].reshape(D_MODEL, PEER_HEADS, 2, PEER_HALF).transpose(2, 1, 3, 0)
    p = {
        "woa": w_out[0][:ATT_WIDTH].astype(BF16), "woc": w_out[0][ATT_WIDTH:].astype(BF16),
        "ffn_g": ffn_norm_g[0][None, :], "final_g": final_norm_g[None, :],
        "wqt": wq.reshape(D_MODEL, D_MODEL).astype(BF16),
        "a1s": a_s[0], "a2s": a_s[1], "a1d": a_d[0], "a2d": a_d[1],
        "u": peer_u[0].astype(BF16), "vt": peer_v[0].T.astype(BF16),
    }
    w_scale = (N_IDX_HEADS ** -0.5) * (IDX_DIM ** -0.5)

    xp = x_prompt[0]
    tb = _pick(s, 256)
    q, k, v, qi, kw, glu = _in_proj(xp, g_attn, wqkv, widx, wglu, tb)
    conv_p, tail_p = _conv(glu, jnp.zeros((1, CONV_TAIL, CONV_CH), F32), cw, cb, cg, cbeta, 1, tb)
    ki = kw[:, :IDX_DIM]
    ki_b = ki.astype(BF16)
    qi_hm = _head_major(qi, N_IDX_HEADS)
    w_t = (kw[:, IDX_DIM:IDX_DIM + N_IDX_HEADS] * w_scale).T
    tq = _pick(s, 256)
    thr, cut = _idx_thr(qi_hm, w_t, ki_b, tq)
    q_hm = _head_major(q, N_HEADS)
    k_hm = _head_major(k, N_HEADS)
    vt_hm = v.reshape(s, N_HEADS, HEAD_DIM).transpose(1, 2, 0).astype(BF16)
    att_t = _attn(q_hm, k_hm, vt_hm, qi_hm, w_t, ki_b, thr, cut, tq, _pick(s, 512))
    y_p = _token_stage(xp, att_t.T, conv_p, p, tb, _pick(s, 512))

    xs = x_sample.reshape(nb * tn, D_MODEL)
    ts = nb * tn
    tbs = _pick(ts, 256)
    qs, ks, vs, qis, kws, glus = _in_proj(xs, g_attn, wqkv, widx, wglu, tbs)
    pad = jnp.zeros((nb, CONV_TAIL - (CONV_WIDTH - 1), CONV_CH), F32)
    conv_s, tail_s = _conv(glus, jnp.concatenate([pad, state_conv[0]], axis=1), cw, cb, cg, cbeta, nb, tn)
    hm4 = lambda a, nh, dt=BF16: a.reshape(nb, tn, nh, a.shape[1] // nh).transpose(0, 2, 1, 3).astype(dt)
    kis = kws[:, :IDX_DIM]
    att_s = _sample_attn(
        hm4(qs, N_HEADS), hm4(qis, N_IDX_HEADS),
        (kws[:, IDX_DIM:IDX_DIM + N_IDX_HEADS] * w_scale).reshape(nb, tn, N_IDX_HEADS),
        cache_kidx[0], kis.reshape(nb, tn, IDX_DIM),
        cache_k[0].transpose(0, 2, 1, 3).astype(BF16), cache_v[0].transpose(0, 2, 1, 3).astype(BF16),
        hm4(ks, N_HEADS), hm4(vs, N_HEADS))
    att_s = att_s.transpose(0, 2, 1, 3).reshape(ts, ATT_WIDTH)
    y_s = _token_stage(xs, att_s, conv_s, p, tbs, tbs)

    hd = (N_HEADS, HEAD_DIM)
    keep = CONV_TAIL - (CONV_WIDTH - 1)
    return (y_p[None], y_s.reshape(nb, tn, D_MODEL),
            k.reshape(1, 1, s, *hd), v.reshape(1, 1, s, *hd), ki[None, None], tail_p[None, :, keep:],
            ks.reshape(1, nb, tn, *hd), vs.reshape(1, nb, tn, *hd), kis.reshape(1, nb, tn, IDX_DIM),
            tail_s[None, :, keep:])
```

```python
import functools

import jax
import jax.numpy as jnp
from jax import lax
from jax.experimental import pallas as pl
from jax.experimental.pallas import tpu as pltpu

F32 = jnp.float32
BF16 = jnp.bfloat16
I32 = jnp.int32

D_MODEL = 1024
N_HEADS = 8
HEAD_DIM = 64
ATT_WIDTH = N_HEADS * HEAD_DIM
N_IDX_HEADS = 4
IDX_DIM = 64
TOPK = 256
CHUNK_SHIFT = 6
CONV_CH = 512
CONV_WIDTH = 31
CONV_TAIL = 32
PEER_HEADS = 8
PEER_KEYS = 128
PEER_HALF = 64
PEER_TOPK = 16
EPS = 1e-6
INT_MIN = -2147483648
ORDER_MASK = 0x7FFFFFFF
INV_SQRT2 = 0.7071067811865476
LOG2E = 1.4426950408889634
NEG_INF = float("-inf")
M_INIT = -1e30

VMEM_LIMIT = 56 * 1024 * 1024


def _params(sem, vmem=VMEM_LIMIT):
    return pltpu.CompilerParams(dimension_semantics=sem, vmem_limit_bytes=vmem)


def _nt(a, b):
    return lax.dot_general(a, b, (((1,), (1,)), ((), ())), preferred_element_type=F32)


def _dot(a, b):
    return jnp.dot(a, b, preferred_element_type=F32)


def _rms(x, g):
    return x * lax.rsqrt(jnp.mean(x * x, axis=-1, keepdims=True) + EPS) * g


def _order_key(x):
    b = pltpu.bitcast(x, I32)
    return b ^ ((b >> 31) & ORDER_MASK)


def _order_unkey(k):
    return pltpu.bitcast(k ^ ((k >> 31) & ORDER_MASK), F32)


def _in_proj_body(x_ref, g_ref, wqkv_ref, widx_ref, wglu_ref,
                  q_ref, k_ref, v_ref, qi_ref, kw_ref, glu_ref):
    xn = _rms(x_ref[...], g_ref[...]).astype(BF16)
    qkv = _dot(xn, wqkv_ref[...])
    q_ref[...] = (qkv[:, :ATT_WIDTH] * (HEAD_DIM ** -0.5 * LOG2E)).astype(BF16)
    k_ref[...] = qkv[:, ATT_WIDTH:2 * ATT_WIDTH]
    v_ref[...] = qkv[:, 2 * ATT_WIDTH:]
    ix = _dot(xn, widx_ref[...])
    qi_ref[...] = ix[:, :N_IDX_HEADS * IDX_DIM].astype(BF16)
    kw_ref[...] = ix[:, N_IDX_HEADS * IDX_DIM:]
    glu_ref[...] = _dot(xn, wglu_ref[...])


def _in_proj(x, g, wqkv, widx, wglu, tb):
    t = x.shape[0]
    row = lambda w: pl.BlockSpec((tb, w), lambda i: (i, 0))
    full = lambda a: pl.BlockSpec(a.shape, lambda i: (0, 0))
    return pl.pallas_call(
        _in_proj_body,
        grid=(t // tb,),
        in_specs=[row(D_MODEL), full(g), full(wqkv), full(widx), full(wglu)],
        out_specs=[row(ATT_WIDTH), row(ATT_WIDTH), row(ATT_WIDTH), row(256), row(128), row(2 * CONV_CH)],
        out_shape=[jax.ShapeDtypeStruct((t, ATT_WIDTH), BF16),
                   jax.ShapeDtypeStruct((t, ATT_WIDTH), F32),
                   jax.ShapeDtypeStruct((t, ATT_WIDTH), F32),
                   jax.ShapeDtypeStruct((t, 256), BF16),
                   jax.ShapeDtypeStruct((t, 128), F32),
                   jax.ShapeDtypeStruct((t, 2 * CONV_CH), F32)],
        compiler_params=_params(("parallel",)),
        name="in_proj",
    )(x, g, wqkv, widx, wglu)


def _conv_body(glu_ref, tail0_ref, w_ref, b_ref, g_ref, beta_ref, out_ref, tail_ref, ubuf, *, tb):
    j = pl.program_id(1)

    @pl.when(j == 0)
    def _():
        ubuf[0:CONV_TAIL, :] = tail0_ref[0]

    @pl.when(j > 0)
    def _():
        ubuf[0:CONV_TAIL, :] = ubuf[tb:tb + CONV_TAIL, :]

    glu = glu_ref[...]
    ubuf[CONV_TAIL:CONV_TAIL + tb, :] = glu[:, :CONV_CH] * jax.nn.sigmoid(glu[:, CONV_CH:])
    off = CONV_TAIL - (CONV_WIDTH - 1)
    acc = ubuf[off:off + tb, :] * w_ref[0:1, :]
    for t in range(1, CONV_WIDTH):
        acc = acc + ubuf[off + t:off + t + tb, :] * w_ref[t:t + 1, :]
    y = acc + b_ref[...]
    mu = jnp.mean(y, axis=-1, keepdims=True)
    d = y - mu
    var = jnp.mean(d * d, axis=-1, keepdims=True)
    z = d * lax.rsqrt(var + EPS) * g_ref[...] + beta_ref[...]
    out_ref[...] = (z * jax.nn.sigmoid(z)).astype(BF16)
    tail_ref[0] = ubuf[tb:tb + CONV_TAIL, :]


def _conv(glu, tail0, w, b, g, beta, nseq, tb):
    t = glu.shape[0]
    nblk = t // (nseq * tb)
    full = lambda a: pl.BlockSpec(a.shape, lambda s, j: (0, 0))
    return pl.pallas_call(
        functools.partial(_conv_body, tb=tb),
        grid=(nseq, nblk),
        in_specs=[pl.BlockSpec((tb, 2 * CONV_CH), lambda s, j: (s * nblk + j, 0)),
                  pl.BlockSpec((1, CONV_TAIL, CONV_CH), lambda s, j: (s, 0, 0)),
                  full(w), full(b), full(g), full(beta)],
        out_specs=[pl.BlockSpec((tb, CONV_CH), lambda s, j: (s * nblk + j, 0)),
                   pl.BlockSpec((1, CONV_TAIL, CONV_CH), lambda s, j: (s, 0, 0))],
        out_shape=[jax.ShapeDtypeStruct((t, CONV_CH), BF16),
                   jax.ShapeDtypeStruct((nseq, CONV_TAIL, CONV_CH), F32)],
        scratch_shapes=[pltpu.VMEM((tb + CONV_TAIL, CONV_CH), F32)],
        compiler_params=_params(("parallel", "arbitrary")),
        name="conv",
    )(glu, tail0, w, b, g, beta)


def _idx_scores_t(ki_tile, qi_ref, w):
    acc = None
    for h in range(N_IDX_HEADS):
        term = w[h:h + 1, :] * jnp.maximum(_nt(ki_tile, qi_ref[h]), 0.0)
        acc = term if acc is None else acc + term
    return acc


def _chunk_limit(q0, tq):
    qpos = q0 + lax.broadcasted_iota(I32, (1, tq), 1)
    return ((qpos >> CHUNK_SHIFT) + 1) << CHUNK_SHIFT


def _bit_planes(words):
    x = list(words)
    mask, j = 0x0000FFFF, 16
    while j:
        k = 0
        while k < 32:
            t = (x[k] ^ lax.shift_right_logical(x[k + j], jnp.int32(j))) & mask
            x[k] = x[k] ^ t
            x[k + j] = x[k + j] ^ (t << j)
            k = (k + j + 1) & ~j
        j >>= 1
        mask ^= (mask << j) & 0xFFFFFFFF
    return x


def _thr_body(qi_ref, w_ref, ki_ref, thr_ref, cut_ref, keys_ref, planes_ref, cand_ref, *, tq, grp):
    i = pl.program_id(0)
    nt = i + 1
    ng = (nt + grp - 1) // grp
    w = w_ref[...]
    limit = _chunk_limit(i * tq, tq)
    rows = lax.broadcasted_iota(I32, (tq, tq), 0)
    slabs = tq // 8
    assert slabs == 32

    def fill(t, diagonal):
        r0 = pl.multiple_of(t * tq, tq)
        key = _order_key(_idx_scores_t(ki_ref[pl.ds(r0, tq), :], qi_ref, w))
        if diagonal:
            key = jnp.where(rows + r0 < limit, key, INT_MIN)
        keys_ref[pl.ds(r0, tq), :] = key
        u = key ^ INT_MIN
        planes = _bit_planes([u[8 * k:8 * k + 8, :] for k in range(slabs)])
        at = pl.ds(pl.multiple_of(t * 8, 8), 8)
        for p in range(32):
            planes_ref[p, at, :] = planes[p]
        cand_ref[at, :] = jnp.full((8, tq), -1, I32)

    def fill_full(t, c):
        fill(t, False)
        return c

    lax.fori_loop(0, nt - 1, fill_full, 0)
    fill(nt - 1, True)

    def pad(t, c):
        at = pl.ds(pl.multiple_of(t * 8, 8), 8)
        planes_ref[:, at, :] = jnp.zeros((32, 8, tq), I32)
        cand_ref[at, :] = jnp.zeros((8, tq), I32)
        return c

    lax.fori_loop(nt, ng * grp, pad, 0)

    def sweep(p_prev, flip, p_next):
        def group(g, acc):
            at = pl.ds(pl.multiple_of(g * (grp * 8), grp * 8), grp * 8)
            c = cand_ref[at, :]
            if p_prev is not None:
                c = c & (planes_ref[p_prev, at, :] ^ flip)
                cand_ref[at, :] = c
            if p_next is not None:
                hit = lax.population_count(c & planes_ref[p_next, at, :])
                acc = acc + jnp.sum(hit.reshape(grp, 8, tq), axis=0)
            return acc
        acc = lax.fori_loop(0, ng, group, jnp.zeros((8, tq), I32))
        return jnp.sum(acc, axis=0, keepdims=True)

    def decide(p, cnt, kth_u, need):
        one = cnt >= need
        kth_u = jnp.where(one, kth_u | lax.shift_right_logical(jnp.int32(INT_MIN), p), kth_u)
        return kth_u, jnp.where(one, need, need - cnt), jnp.where(one, 0, -1)

    zero = jnp.zeros((1, tq), I32)
    kth_u, need, flip = decide(jnp.int32(0), sweep(None, None, 0), zero, jnp.full((1, tq), TOPK, I32))

    def bit_step(p, carry):
        kth_u, need, flip = carry
        return decide(p, sweep(p - 1, flip, p), kth_u, need)

    kth_u, need, flip = lax.fori_loop(1, 32, bit_step, (kth_u, need, flip))
    kth = kth_u ^ INT_MIN
    need = need.astype(F32)

    tri = jnp.where(lax.broadcasted_iota(I32, (tq, tq), 1) <= rows, 1.0, 0.0).astype(BF16)

    def tie(t, carry):
        seen, cut = carry
        r0 = pl.multiple_of(t * tq, tq)
        eq = keys_ref[pl.ds(r0, tq), :] == kth
        pre = _dot(tri, jnp.where(eq, 1.0, 0.0).astype(BF16))
        last = jnp.where(eq & (pre + seen == need), rows + r0, -1)
        return seen + pre[tq - 1:tq, :], jnp.maximum(cut, jnp.max(last, axis=0, keepdims=True))

    _, cut = lax.fori_loop(0, nt, tie, (jnp.zeros((1, tq), F32), jnp.full((1, tq), -1, I32)))
    thr_ref[...] = jnp.where(kth == INT_MIN, NEG_INF, _order_unkey(kth))
    cut_ref[...] = cut


def _idx_thr(qi_hm, w_t, ki, tq):
    s = ki.shape[0]
    return pl.pallas_call(
        functools.partial(_thr_body, tq=tq, grp=_pick(s // tq, 8)),
        grid=(s // tq,),
        in_specs=[pl.BlockSpec((N_IDX_HEADS, tq, IDX_DIM), lambda i: (0, i, 0)),
                  pl.BlockSpec((N_IDX_HEADS, tq), lambda i: (0, i)),
                  pl.BlockSpec((s, IDX_DIM), lambda i: (0, 0))],
        out_specs=[pl.BlockSpec((1, tq), lambda i: (0, i)),
                   pl.BlockSpec((1, tq), lambda i: (0, i))],
        out_shape=[jax.ShapeDtypeStruct((1, s), F32), jax.ShapeDtypeStruct((1, s), I32)],
        scratch_shapes=[pltpu.VMEM((s, tq), I32), pltpu.VMEM((32, s // tq * 8, tq), I32),
                        pltpu.VMEM((s // tq * 8, tq), I32)],
        compiler_params=_params(("arbitrary",)),
        name="idx_thr",
    )(qi_hm, w_t, ki)


def _attn_body(qb_ref, kb_ref, q_ref, k_ref, vt_ref, qi_ref, w_ref, ki_ref, thr_ref, cut_ref,
               out_ref, m_ref, l_ref, acc_ref, s_ref, p_ref, *, tq, tk):
    step = pl.program_id(0)
    i = qb_ref[step]
    j = kb_ref[step]

    @pl.when(j == 0)
    def _():
        m_ref[...] = jnp.full(m_ref.shape, M_INIT, F32)
        l_ref[...] = jnp.zeros(l_ref.shape, F32)
        acc_ref[...] = jnp.zeros(acc_ref.shape, F32)

    sc = _idx_scores_t(ki_ref[...], qi_ref, w_ref[...])
    kpos = j * tk + lax.broadcasted_iota(I32, (tk, tq), 0)
    thr = thr_ref[...]
    sel = (kpos < _chunk_limit(i * tq, tq)) & ((sc > thr) | ((sc == thr) & (kpos <= cut_ref[...])))
    bias = jnp.where(sel, 0.0, NEG_INF)
    for h in range(N_HEADS):
        s_ref[h] = _nt(k_ref[h], q_ref[h]) + bias
    m_old = m_ref[...]
    m_new = jnp.maximum(m_old, jnp.concatenate(
        [jnp.max(s_ref[h], axis=0, keepdims=True) for h in range(N_HEADS)], axis=0))
    alpha = jnp.exp2(m_old - m_new)
    sums = []
    for h in range(N_HEADS):
        p = jnp.exp2(s_ref[h] - m_new[h:h + 1, :])
        sums.append(jnp.sum(p, axis=0, keepdims=True))
        p_ref[h] = p.astype(BF16)
    l_ref[...] = alpha * l_ref[...] + jnp.concatenate(sums, axis=0)
    m_ref[...] = m_new
    for h in range(N_HEADS):
        acc_ref[h] = alpha[h:h + 1, :] * acc_ref[h] + _dot(vt_ref[h], p_ref[h])

    @pl.when(j == ((i + 1) * tq - 1) // tk)
    def _():
        for h in range(N_HEADS):
            out_ref[h * HEAD_DIM:(h + 1) * HEAD_DIM, :] = (acc_ref[h] / l_ref[h:h + 1, :]).astype(BF16)


def _attn(q_hm, k_hm, vt_hm, qi_hm, w_t, ki, thr, cut, tq, tk):
    s = ki.shape[0]
    qb, kb = [], []
    for i in range(s // tq):
        for j in range(((i + 1) * tq - 1) // tk + 1):
            qb.append(i)
            kb.append(j)
    qb = jnp.asarray(qb, I32)
    kb = jnp.asarray(kb, I32)
    grid_spec = pltpu.PrefetchScalarGridSpec(
        num_scalar_prefetch=2,
        grid=(int(qb.shape[0]),),
        in_specs=[pl.BlockSpec((N_HEADS, tq, HEAD_DIM), lambda t, qb, kb: (0, qb[t], 0)),
                  pl.BlockSpec((N_HEADS, tk, HEAD_DIM), lambda t, qb, kb: (0, kb[t], 0)),
                  pl.BlockSpec((N_HEADS, HEAD_DIM, tk), lambda t, qb, kb: (0, 0, kb[t])),
                  pl.BlockSpec((N_IDX_HEADS, tq, IDX_DIM), lambda t, qb, kb: (0, qb[t], 0)),
                  pl.BlockSpec((N_IDX_HEADS, tq), lambda t, qb, kb: (0, qb[t])),
                  pl.BlockSpec((tk, IDX_DIM), lambda t, qb, kb: (kb[t], 0)),
                  pl.BlockSpec((1, tq), lambda t, qb, kb: (0, qb[t])),
                  pl.BlockSpec((1, tq), lambda t, qb, kb: (0, qb[t]))],
        out_specs=pl.BlockSpec((ATT_WIDTH, tq), lambda t, qb, kb: (0, qb[t])),
        scratch_shapes=[pltpu.VMEM((N_HEADS, tq), F32),
                        pltpu.VMEM((N_HEADS, tq), F32),
                        pltpu.VMEM((N_HEADS, HEAD_DIM, tq), F32),
                        pltpu.VMEM((N_HEADS, tk, tq), F32),
                        pltpu.VMEM((N_HEADS, tk, tq), BF16)],
    )
    return pl.pallas_call(
        functools.partial(_attn_body, tq=tq, tk=tk),
        grid_spec=grid_spec,
        out_shape=jax.ShapeDtypeStruct((ATT_WIDTH, s), BF16),
        compiler_params=_params(("arbitrary",)),
        name="attn",
    )(qb, kb, q_hm, k_hm, vt_hm, qi_hm, w_t, ki, thr, cut)


NEW_PAD = 128


def _sample_attn_body(q_ref, qi_ref, w_ref, kic_ref, kin_ref, kc_ref, vc_ref, kn_ref, vn_ref,
                      out_ref, kinp, knp, vnp, *, past, tn):
    kinp[...] = jnp.zeros(kinp.shape, BF16)
    kinp[0:tn, :] = kin_ref[0].astype(BF16)
    knp[...] = jnp.zeros(knp.shape, BF16)
    vnp[...] = jnp.zeros(vnp.shape, BF16)
    for h in range(N_HEADS):
        knp[h, 0:tn, :] = kn_ref[0, h]
        vnp[h, 0:tn, :] = vn_ref[0, h]

    w = w_ref[0]
    kic = kic_ref[0].astype(BF16)
    kin = kinp[...]
    sc = None
    sn = None
    for h in range(N_IDX_HEADS):
        qih = qi_ref[0, h]
        wc = w[:, h:h + 1]
        tc = wc * jnp.maximum(_nt(qih, kic), 0.0)
        tnw = wc * jnp.maximum(_nt(qih, kin), 0.0)
        sc = tc if sc is None else sc + tc
        sn = tnw if sn is None else sn + tnw
    new_ok = lax.broadcasted_iota(I32, (tn, NEW_PAD), 1) < tn
    keyc = _order_key(sc)
    keyn = jnp.where(new_ok, _order_key(sn), INT_MIN)

    def count(pc, pn):
        return (jnp.sum(jnp.where(pc, 1, 0), axis=1, keepdims=True)
                + jnp.sum(jnp.where(pn, 1, 0), axis=1, keepdims=True))

    def bit_step(p, u):
        cand_u = u | (jnp.int32(1) << (31 - p))
        cand = cand_u ^ INT_MIN
        return jnp.where(count(keyc >= cand, keyn >= cand) >= TOPK, cand_u, u)

    kth = lax.fori_loop(0, 32, bit_step, jnp.zeros((tn, 1), I32)) ^ INT_MIN
    need = (TOPK - count(keyc > kth, keyn > kth)).astype(F32)

    cw = 256
    tri = jnp.where(lax.broadcasted_iota(I32, (cw, cw), 0) <= lax.broadcasted_iota(I32, (cw, cw), 1),
                    1.0, 0.0).astype(BF16)
    seen = jnp.zeros((tn, 1), F32)
    bias_c = []
    for c in range(past // cw):
        kc_ = keyc[:, c * cw:(c + 1) * cw]
        eq = kc_ == kth
        pre = _dot(jnp.where(eq, 1.0, 0.0).astype(BF16), tri)
        sel = (kc_ > kth) | (eq & (pre + seen <= need))
        bias_c.append(jnp.where(sel, 0.0, NEG_INF))
        seen = seen + pre[:, cw - 1:cw]
    bias_c = jnp.concatenate(bias_c, axis=1)
    eqn = keyn == kth
    pren = _dot(jnp.where(eqn, 1.0, 0.0).astype(BF16), tri[:NEW_PAD, :NEW_PAD])
    bias_n = jnp.where(new_ok & ((keyn > kth) | (eqn & (pren + seen <= need))), 0.0, NEG_INF)

    for h in range(N_HEADS):
        qh = q_ref[0, h]
        s_c = _nt(qh, kc_ref[0, h]) + bias_c
        s_n = _nt(qh, knp[h]) + bias_n
        m = jnp.maximum(jnp.max(s_c, axis=1, keepdims=True), jnp.max(s_n, axis=1, keepdims=True))
        p_c = jnp.exp2(s_c - m)
        p_n = jnp.exp2(s_n - m)
        l = jnp.sum(p_c, axis=1, keepdims=True) + jnp.sum(p_n, axis=1, keepdims=True)
        o = _dot(p_c.astype(BF16), vc_ref[0, h]) + _dot(p_n.astype(BF16), vnp[h])
        out_ref[0, h] = (o / l).astype(BF16)


def _sample_attn(q_hm, qi_hm, w, kic, kin, kc_hm, vc_hm, kn_hm, vn_hm):
    nb, _, tn, _ = q_hm.shape
    past = kic.shape[1]
    b4 = lambda a: pl.BlockSpec((1,) + a.shape[1:], lambda b: (b, 0, 0, 0))
    b3 = lambda a: pl.BlockSpec((1,) + a.shape[1:], lambda b: (b, 0, 0))
    return pl.pallas_call(
        functools.partial(_sample_attn_body, past=past, tn=tn),
        grid=(nb,),
        in_specs=[b4(q_hm), b4(qi_hm), b3(w), b3(kic), b3(kin), b4(kc_hm), b4(vc_hm), b4(kn_hm), b4(vn_hm)],
        out_specs=pl.BlockSpec((1, N_HEADS, tn, HEAD_DIM), lambda b: (b, 0, 0, 0)),
        out_shape=jax.ShapeDtypeStruct((nb, N_HEADS, tn, HEAD_DIM), BF16),
        scratch_shapes=[pltpu.VMEM((NEW_PAD, IDX_DIM), BF16),
                        pltpu.VMEM((N_HEADS, NEW_PAD, HEAD_DIM), BF16),
                        pltpu.VMEM((N_HEADS, NEW_PAD, HEAD_DIM), BF16)],
        compiler_params=_params(("parallel",)),
        name="sample_attn",
    )(q_hm, qi_hm, w, kic, kin, kc_hm, vc_hm, kn_hm, vn_hm)


def _cmpx(a, b):
    return jnp.maximum(a, b), jnp.minimum(a, b)


def _bitonic_merge_desc(v):
    n = len(v)
    v = list(v)
    j = n // 2
    while j >= 1:
        for i in range(n):
            l = i ^ j
            if l > i:
                v[i], v[l] = _cmpx(v[i], v[l])
        j //= 2
    return v


def _bitonic_sort_desc(v):
    n = len(v)
    v = list(v)
    k = 2
    while k <= n:
        j = k // 2
        while j >= 1:
            for i in range(n):
                l = i ^ j
                if l > i:
                    hi, lo = _cmpx(v[i], v[l])
                    v[i], v[l] = (hi, lo) if (i & k) == 0 else (lo, hi)
            j //= 2
        k *= 2
    return v


def _merge_top(a, b):
    n = len(a)
    return _bitonic_merge_desc([jnp.maximum(a[k], b[n - 1 - k]) for k in range(n)])


def _top16_desc(vals):
    groups = [_bitonic_sort_desc(vals[g:g + PEER_TOPK]) for g in range(0, len(vals), PEER_TOPK)]
    while len(groups) > 1:
        groups = [_merge_top(groups[g], groups[g + 1]) for g in range(0, len(groups), 2)]
    return groups[0]


def _mid_body(x_ref, att_ref, conv_ref, woa_ref, woc_ref, g_ref, wqt_ref, a1_ref, a2_ref,
              h_ref, xn_ref, qht_ref, st_ref):
    h = x_ref[...] + _dot(att_ref[...], woa_ref[...]) + _dot(conv_ref[...], woc_ref[...])
    h_ref[...] = h
    xn = _rms(h, g_ref[...]).astype(BF16)
    xn_ref[...] = xn
    qht = _nt(wqt_ref[...], xn).astype(BF16)
    qht_ref[...] = qht
    half = PEER_HEADS * PEER_HALF
    s1 = _dot(a1_ref[...], qht[:half])
    s2 = _dot(a2_ref[...], qht[half:])
    rows = lambda s: [s[r * PEER_HEADS:(r + 1) * PEER_HEADS, :] for r in range(PEER_KEYS)]
    v1 = _top16_desc(rows(s1))
    v2 = _top16_desc(rows(s2))
    ninf = jnp.full(v1[0].shape, NEG_INF, F32)
    top = [v1[0] + v2[b] for b in range(PEER_TOPK)]
    for a in range(1, PEER_TOPK):
        n_a = PEER_TOPK // (a + 1)
        top = _merge_top(top, [v1[a] + v2[b] if b < n_a else ninf for b in range(PEER_TOPK)])
    z = jnp.exp(top[0] - top[0])
    for k in range(1, PEER_TOPK):
        z = z + jnp.exp(top[k] - top[0])
    st_ref[0:8, :] = v1[0]
    st_ref[8:16, :] = v2[0]
    st_ref[16:24, :] = top[PEER_TOPK - 1]
    st_ref[24:32, :] = 1.0 / z


def _mid(x, att, conv, woa, woc, g, wqt, a1, a2, tb):
    t = x.shape[0]
    row = lambda w: pl.BlockSpec((tb, w), lambda i: (i, 0))
    col = lambda r: pl.BlockSpec((r, tb), lambda i: (0, i))
    full = lambda a: pl.BlockSpec(a.shape, lambda i: (0, 0))
    return pl.pallas_call(
        _mid_body,
        grid=(t // tb,),
        in_specs=[row(D_MODEL), row(ATT_WIDTH), row(CONV_CH), full(woa), full(woc), full(g),
                  full(wqt), full(a1), full(a2)],
        out_specs=[row(D_MODEL), row(D_MODEL), col(D_MODEL), col(32)],
        out_shape=[jax.ShapeDtypeStruct((t, D_MODEL), F32),
                   jax.ShapeDtypeStruct((t, D_MODEL), BF16),
                   jax.ShapeDtypeStruct((D_MODEL, t), BF16),
                   jax.ShapeDtypeStruct((32, t), F32)],
        compiler_params=_params(("parallel",)),
        name="mid",
    )(x, att, conv, woa, woc, g, wqt, a1, a2)


def _peer_body(xn_ref, qht_ref, st_ref, a1_ref, a2_ref, u_ref, vt_ref, out_ref,
               s1_ref, s2_ref, e1_ref, e2_ref, hid_ref, *, eb):
    e = pl.program_id(1)
    nsub = eb // PEER_KEYS
    half = PEER_HEADS * PEER_HALF

    @pl.when(e == 0)
    def _():
        qht = qht_ref[...]
        s1 = _dot(a1_ref[...], qht[:half])
        s2 = _dot(a2_ref[...], qht[half:])
        s1_ref[...] = s1
        s2_ref[...] = s2
        for h in range(PEER_HEADS):
            r = slice(h * PEER_KEYS, (h + 1) * PEER_KEYS)
            e1_ref[r, :] = jnp.exp(s1[r] - st_ref[h:h + 1, :])
            e2_ref[r, :] = jnp.exp(s2[r] - st_ref[8 + h:9 + h, :]) * st_ref[24 + h:25 + h, :] * 0.5
        out_ref[...] = jnp.zeros(out_ref.shape, F32)

    xn = xn_ref[...]
    for ii in range(nsub):
        i = e * nsub + ii
        a = _nt(u_ref[ii * PEER_KEYS:(ii + 1) * PEER_KEYS, :], xn)
        gate = None
        for h in range(PEER_HEADS):
            r = slice(h * PEER_KEYS, (h + 1) * PEER_KEYS)
            pair = s1_ref[pl.ds(h * PEER_KEYS + i, 1), :] + s2_ref[r, :]
            g = e1_ref[pl.ds(h * PEER_KEYS + i, 1), :] * e2_ref[r, :]
            g = jnp.where(pair >= st_ref[16 + h:17 + h, :], g, 0.0)
            gate = g if gate is None else gate + g
        hid = a * (1.0 + lax.erf(a * INV_SQRT2)) * gate
        hid_ref[ii * PEER_KEYS:(ii + 1) * PEER_KEYS, :] = hid.astype(BF16)
    out_ref[...] += _dot(vt_ref[...], hid_ref[...])


def _peer(xn, qht, st, a1, a2, u, vt, tb, eb):
    t = xn.shape[0]
    ne = u.shape[0] // eb
    full = lambda a: pl.BlockSpec(a.shape, lambda i, e: (0, 0))
    return pl.pallas_call(
        functools.partial(_peer_body, eb=eb),
        grid=(t // tb, ne),
        in_specs=[pl.BlockSpec((tb, D_MODEL), lambda i, e: (i, 0)),
                  pl.BlockSpec((D_MODEL, tb), lambda i, e: (0, i)),
                  pl.BlockSpec((32, tb), lambda i, e: (0, i)),
                  full(a1), full(a2),
                  pl.BlockSpec((eb, D_MODEL), lambda i, e: (e, 0)),
                  pl.BlockSpec((D_MODEL, eb), lambda i, e: (0, e))],
        out_specs=pl.BlockSpec((D_MODEL, tb), lambda i, e: (0, i)),
        out_shape=jax.ShapeDtypeStruct((D_MODEL, t), F32),
        scratch_shapes=[pltpu.VMEM((PEER_HEADS * PEER_KEYS, tb), F32)] * 4
                       + [pltpu.VMEM((eb, tb), BF16)],
        compiler_params=_params(("parallel", "arbitrary")),
        name="peer",
    )(xn, qht, st, a1, a2, u, vt)


def _final_body(h_ref, pt_ref, g_ref, y_ref):
    y_ref[...] = _rms(h_ref[...] + pt_ref[...].T, g_ref[...])


def _final(h, pt, g, tb):
    t = h.shape[0]
    return pl.pallas_call(
        _final_body,
        grid=(t // tb,),
        in_specs=[pl.BlockSpec((tb, D_MODEL), lambda i: (i, 0)),
                  pl.BlockSpec((D_MODEL, tb), lambda i: (0, i)),
                  pl.BlockSpec((1, D_MODEL), lambda i: (0, 0))],
        out_specs=pl.BlockSpec((tb, D_MODEL), lambda i: (i, 0)),
        out_shape=jax.ShapeDtypeStruct((t, D_MODEL), F32),
        compiler_params=_params(("parallel",)),
        name="final",
    )(h, pt, g)


def _pick(n, pref):
    b = min(n, pref)
    while n % b:
        b //= 2
    return b


def _head_major(a, nh, dt=BF16):
    t = a.shape[0]
    return a.reshape(t, nh, a.shape[1] // nh).transpose(1, 0, 2).astype(dt)


def _token_stage(x, att, conv, p, tb_mid, tb_peer):
    h, xn, qht, st = _mid(x, att, conv, p["woa"], p["woc"], p["ffn_g"], p["wqt"], p["a1s"], p["a2s"], tb_mid)
    pt = _peer(xn, qht, st, p["a1d"], p["a2d"], p["u"], p["vt"], tb_peer, 1024)
    return _final(h, pt, p["final_g"], tb_mid)


def kernel(x_prompt, x_sample, cache_k, cache_v, cache_kidx, state_conv, attn_norm_g, w_in, conv_w, conv_b,
           conv_ln_g, conv_ln_b, w_out, ffn_norm_g, peer_wq, peer_subkeys, peer_u, peer_v, final_norm_g):
    assert w_in.shape[0] == 1, "single layer"
    nbp, s, _ = x_prompt.shape
    nb, tn, _ = x_sample.shape
    assert nbp == 1
    past = cache_k.shape[2]

    wi = w_in[0]
    o_qi = 3 * ATT_WIDTH
    o_ki = o_qi + N_IDX_HEADS * IDX_DIM
    o_wi = o_ki + IDX_DIM
    o_glu = o_wi + N_IDX_HEADS
    wqkv = wi[:, :o_qi].astype(BF16)
    widx = jnp.concatenate([wi[:, o_qi:o_glu], jnp.zeros((D_MODEL, 128 - IDX_DIM - N_IDX_HEADS), F32)],
                           axis=1).astype(BF16)
    wglu = wi[:, o_glu:].astype(BF16)
    g_attn = attn_norm_g[0][None, :]
    cw = jnp.concatenate([conv_w[0], jnp.zeros((CONV_TAIL - CONV_WIDTH, CONV_CH), F32)], axis=0)
    cb, cg, cbeta = conv_b[0][None, :], conv_ln_g[0][None, :], conv_ln_b[0][None, :]
    eye = jnp.eye(PEER_HEADS, dtype=F32)
    sk = peer_subkeys[0]
    a_s = [jnp.einsum("id,hg->ihgd", sk[c], eye).reshape(PEER_KEYS * PEER_HEADS, PEER_HEADS * PEER_HALF)
           .astype(BF16) for c in range(2)]
    a_d = [jnp.einsum("id,hg->higd", sk[c], eye).reshape(PEER_KEYS * PEER_HEADS, PEER_HEADS * PEER_HALF)
           .astype(BF16) for c in range(2)]
    wq = peer_wq[0].reshape(D_MODEL, PEER_HEADS, 2, PEER_HALF).transpose(2, 1, 3, 0)
    p = {
        "woa": w_out[0][:ATT_WIDTH].astype(BF16), "woc": w_out[0][ATT_WIDTH:].astype(BF16),
        "ffn_g": ffn_norm_g[0][None, :], "final_g": final_norm_g[None, :],
        "wqt": wq.reshape(D_MODEL, D_MODEL).astype(BF16),
        "a1s": a_s[0], "a2s": a_s[1], "a1d": a_d[0], "a2d": a_d[1],
        "u": peer_u[0].astype(BF16), "vt": peer_v[0].T.astype(BF16),
    }
    w_scale = (N_IDX_HEADS ** -0.5) * (IDX_DIM ** -0.5)

    xp = x_prompt[0]
    tb = _pick(s, 256)
    q, k, v, qi, kw, glu = _in_proj(xp, g_attn, wqkv, widx, wglu, tb)
    conv_p, tail_p = _conv(glu, jnp.zeros((1, CONV_TAIL, CONV_CH), F32), cw, cb, cg, cbeta, 1, tb)
    ki = kw[:, :IDX_DIM]
    ki_b = ki.astype(BF16)
    qi_hm = _head_major(qi, N_IDX_HEADS)
    w_t = (kw[:, IDX_DIM:IDX_DIM + N_IDX_HEADS] * w_scale).T
    tq = _pick(s, 256)
    thr, cut = _idx_thr(qi_hm, w_t, ki_b, tq)
    q_hm = _head_major(q, N_HEADS)
    k_hm = _head_major(k, N_HEADS)
    vt_hm = v.reshape(s, N_HEADS, HEAD_DIM).transpose(1, 2, 0).astype(BF16)
    att_t = _attn(q_hm, k_hm, vt_hm, qi_hm, w_t, ki_b, thr, cut, tq, _pick(s, 512))
    y_p = _token_stage(xp, att_t.T, conv_p, p, tb, _pick(s, 512))

    xs = x_sample.reshape(nb * tn, D_MODEL)
    ts = nb * tn
    tbs = _pick(ts, 256)
    qs, ks, vs, qis, kws, glus = _in_proj(xs, g_attn, wqkv, widx, wglu, tbs)
    pad = jnp.zeros((nb, CONV_TAIL - (CONV_WIDTH - 1), CONV_CH), F32)
    conv_s, tail_s = _conv(glus, jnp.concatenate([pad, state_conv[0]], axis=1), cw, cb, cg, cbeta, nb, tn)
    hm4 = lambda a, nh, dt=BF16: a.reshape(nb, tn, nh, a.shape[1] // nh).transpose(0, 2, 1, 3).astype(dt)
    kis = kws[:, :IDX_DIM]
    att_s = _sample_attn(
        hm4(qs, N_HEADS), hm4(qis, N_IDX_HEADS),
        (kws[:, IDX_DIM:IDX_DIM + N_IDX_HEADS] * w_scale).reshape(nb, tn, N_IDX_HEADS),
        cache_kidx[0], kis.reshape(nb, tn, IDX_DIM),
        cache_k[0].transpose(0, 2, 1, 3).astype(BF16), cache_v[0].transpose(0, 2, 1, 3).astype(BF16),
        hm4(ks, N_HEADS), hm4(vs, N_HEADS))
    att_s = att_s.transpose(0, 2, 1, 3).reshape(ts, ATT_WIDTH)
    y_s = _token_stage(xs, att_s, conv_s, p, tbs, tbs)

    hd = (N_HEADS, HEAD_DIM)
    keep = CONV_TAIL - (CONV_WIDTH - 1)
    return (y_p[None], y_s.reshape(nb, tn, D_MODEL),
            k.reshape(1, 1, s, *hd), v.reshape(1, 1, s, *hd), ki[None, None], tail_p[None, :, keep:],
            ks.reshape(1, nb, tn, *hd), vs.reshape(1, nb, tn, *hd), kis.reshape(1, nb, tn, IDX_DIM),
            tail_s[None, :, keep:])
```

```python
import functools

import jax
import jax.numpy as jnp
from jax import lax
from jax.experimental import pallas as pl
from jax.experimental.pallas import tpu as pltpu

F32 = jnp.float32
BF16 = jnp.bfloat16
I32 = jnp.int32

D_MODEL = 1024
N_HEADS = 8
HEAD_DIM = 64
ATT_WIDTH = N_HEADS * HEAD_DIM
N_IDX_HEADS = 4
IDX_DIM = 64
TOPK = 256
CHUNK_SHIFT = 6
CONV_CH = 512
CONV_WIDTH = 31
CONV_TAIL = 32
PEER_HEADS = 8
PEER_KEYS = 128
PEER_HALF = 64
PEER_TOPK = 16
EPS = 1e-6
INT_MIN = -2147483648
ORDER_MASK = 0x7FFFFFFF
INV_SQRT2 = 0.7071067811865476
LOG2E = 1.4426950408889634
NEG_INF = float("-inf")
M_INIT = -1e30

VMEM_LIMIT = 56 * 1024 * 1024


def _params(sem, vmem=VMEM_LIMIT):
    return pltpu.CompilerParams(dimension_semantics=sem, vmem_limit_bytes=vmem)


def _nt(a, b):
    return lax.dot_general(a, b, (((1,), (1,)), ((), ())), preferred_element_type=F32)


def _dot(a, b):
    return jnp.dot(a, b, preferred_element_type=F32)


def _rms(x, g):
    return x * lax.rsqrt(jnp.mean(x * x, axis=-1, keepdims=True) + EPS) * g


def _order_key(x):
    b = pltpu.bitcast(x, I32)
    return b ^ ((b >> 31) & ORDER_MASK)


def _order_unkey(k):
    return pltpu.bitcast(k ^ ((k >> 31) & ORDER_MASK), F32)


def _in_proj_body(x_ref, g_ref, wqkv_ref, widx_ref, wglu_ref,
                  q_ref, k_ref, v_ref, qi_ref, kw_ref, glu_ref):
    xn = _rms(x_ref[...], g_ref[...]).astype(BF16)
    qkv = _dot(xn, wqkv_ref[...])
    q_ref[...] = (qkv[:, :ATT_WIDTH] * (HEAD_DIM ** -0.5 * LOG2E)).astype(BF16)
    k_ref[...] = qkv[:, ATT_WIDTH:2 * ATT_WIDTH]
    v_ref[...] = qkv[:, 2 * ATT_WIDTH:]
    ix = _dot(xn, widx_ref[...])
    qi_ref[...] = ix[:, :N_IDX_HEADS * IDX_DIM].astype(BF16)
    kw_ref[...] = ix[:, N_IDX_HEADS * IDX_DIM:]
    glu_ref[...] = _dot(xn, wglu_ref[...])


def _in_proj(x, g, wqkv, widx, wglu, tb):
    t = x.shape[0]
    row = lambda w: pl.BlockSpec((tb, w), lambda i: (i, 0))
    full = lambda a: pl.BlockSpec(a.shape, lambda i: (0, 0))
    return pl.pallas_call(
        _in_proj_body,
        grid=(t // tb,),
        in_specs=[row(D_MODEL), full(g), full(wqkv), full(widx), full(wglu)],
        out_specs=[row(ATT_WIDTH), row(ATT_WIDTH), row(ATT_WIDTH), row(256), row(128), row(2 * CONV_CH)],
        out_shape=[jax.ShapeDtypeStruct((t, ATT_WIDTH), BF16),
                   jax.ShapeDtypeStruct((t, ATT_WIDTH), F32),
                   jax.ShapeDtypeStruct((t, ATT_WIDTH), F32),
                   jax.ShapeDtypeStruct((t, 256), BF16),
                   jax.ShapeDtypeStruct((t, 128), F32),
                   jax.ShapeDtypeStruct((t, 2 * CONV_CH), F32)],
        compiler_params=_params(("parallel",)),
        name="in_proj",
    )(x, g, wqkv, widx, wglu)


def _conv_body(glu_ref, tail0_ref, w_ref, b_ref, g_ref, beta_ref, out_ref, tail_ref, ubuf, *, tb):
    j = pl.program_id(1)

    @pl.when(j == 0)
    def _():
        ubuf[0:CONV_TAIL, :] = tail0_ref[0]

    @pl.when(j > 0)
    def _():
        ubuf[0:CONV_TAIL, :] = ubuf[tb:tb + CONV_TAIL, :]

    glu = glu_ref[...]
    ubuf[CONV_TAIL:CONV_TAIL + tb, :] = glu[:, :CONV_CH] * jax.nn.sigmoid(glu[:, CONV_CH:])
    off = CONV_TAIL - (CONV_WIDTH - 1)
    acc = ubuf[off:off + tb, :] * w_ref[0:1, :]
    for t in range(1, CONV_WIDTH):
        acc = acc + ubuf[off + t:off + t + tb, :] * w_ref[t:t + 1, :]
    y = acc + b_ref[...]
    mu = jnp.mean(y, axis=-1, keepdims=True)
    d = y - mu
    var = jnp.mean(d * d, axis=-1, keepdims=True)
    z = d * lax.rsqrt(var + EPS) * g_ref[...] + beta_ref[...]
    out_ref[...] = (z * jax.nn.sigmoid(z)).astype(BF16)
    tail_ref[0] = ubuf[tb:tb + CONV_TAIL, :]


def _conv(glu, tail0, w, b, g, beta, nseq, tb):
    t = glu.shape[0]
    nblk = t // (nseq * tb)
    full = lambda a: pl.BlockSpec(a.shape, lambda s, j: (0, 0))
    return pl.pallas_call(
        functools.partial(_conv_body, tb=tb),
        grid=(nseq, nblk),
        in_specs=[pl.BlockSpec((tb, 2 * CONV_CH), lambda s, j: (s * nblk + j, 0)),
                  pl.BlockSpec((1, CONV_TAIL, CONV_CH), lambda s, j: (s, 0, 0)),
                  full(w), full(b), full(g), full(beta)],
        out_specs=[pl.BlockSpec((tb, CONV_CH), lambda s, j: (s * nblk + j, 0)),
                   pl.BlockSpec((1, CONV_TAIL, CONV_CH), lambda s, j: (s, 0, 0))],
        out_shape=[jax.ShapeDtypeStruct((t, CONV_CH), BF16),
                   jax.ShapeDtypeStruct((nseq, CONV_TAIL, CONV_CH), F32)],
        scratch_shapes=[pltpu.VMEM((tb + CONV_TAIL, CONV_CH), F32)],
        compiler_params=_params(("parallel", "arbitrary")),
        name="conv",
    )(glu, tail0, w, b, g, beta)


def _idx_scores_t(ki_tile, qi_ref, w):
    acc = None
    for h in range(N_IDX_HEADS):
        term = w[h:h + 1, :] * jnp.maximum(_nt(ki_tile, qi_ref[h]), 0.0)
        acc = term if acc is None else acc + term
    return acc


def _chunk_limit(q0, tq):
    qpos = q0 + lax.broadcasted_iota(I32, (1, tq), 1)
    return ((qpos >> CHUNK_SHIFT) + 1) << CHUNK_SHIFT


def _bit_planes(words):
    x = list(words)
    mask, j = 0x0000FFFF, 16
    while j:
        k = 0
        while k < 32:
            t = (x[k] ^ lax.shift_right_logical(x[k + j], jnp.int32(j))) & mask
            x[k] = x[k] ^ t
            x[k + j] = x[k + j] ^ (t << j)
            k = (k + j + 1) & ~j
        j >>= 1
        mask ^= (mask << j) & 0xFFFFFFFF
    return x


def _sublane_prefix(x):
    sub = lax.broadcasted_iota(I32, x.shape, 0)
    for sh in (1, 2, 4):
        x = x + jnp.where(sub >= sh, pltpu.roll(x, sh, axis=0), 0)
    return x


def _sel_body(qi_ref, w_ref, ki_ref, sel_ref, planes_ref, cand_ref, gt_ref, *, tq, grp):
    i = pl.program_id(0)
    nt = i + 1
    ng = (nt + grp - 1) // grp
    w = w_ref[...]
    limit = _chunk_limit(i * tq, tq)
    rows = lax.broadcasted_iota(I32, (tq, tq), 0)
    assert tq == 8 * 32

    def fill(t, diagonal):
        r0 = pl.multiple_of(t * tq, tq)
        u = _order_key(_idx_scores_t(ki_ref[pl.ds(r0, tq), :], qi_ref, w)) ^ INT_MIN
        if diagonal:
            u = jnp.where(rows + r0 < limit, u, 0)
        planes = _bit_planes([u[8 * k:8 * k + 8, :] for k in range(32)])
        at = pl.ds(pl.multiple_of(t * 8, 8), 8)
        for p in range(32):
            planes_ref[p, at, :] = planes[p]
        cand_ref[at, :] = jnp.full((8, tq), -1, I32)
        gt_ref[at, :] = jnp.zeros((8, tq), I32)

    def fill_full(t, c):
        fill(t, False)
        return c

    lax.fori_loop(0, nt - 1, fill_full, 0)
    fill(nt - 1, True)

    def pad(t, c):
        at = pl.ds(pl.multiple_of(t * 8, 8), 8)
        planes_ref[:, at, :] = jnp.zeros((32, 8, tq), I32)
        cand_ref[at, :] = jnp.zeros((8, tq), I32)
        gt_ref[at, :] = jnp.zeros((8, tq), I32)
        return c

    lax.fori_loop(nt, ng * grp, pad, 0)

    def sweep(p_prev, flip, p_next):
        def group(g, acc):
            at = pl.ds(pl.multiple_of(g * (grp * 8), grp * 8), grp * 8)
            c = cand_ref[at, :]
            if p_prev is not None:
                prev = planes_ref[p_prev, at, :]
                gt_ref[at, :] = gt_ref[at, :] | (c & prev & flip)
                c = c & (prev ^ flip)
                cand_ref[at, :] = c
            if p_next is not None:
                hit = lax.population_count(c & planes_ref[p_next, at, :])
                acc = acc + jnp.sum(hit.reshape(grp, 8, tq), axis=0)
            return acc
        acc = lax.fori_loop(0, ng, group, jnp.zeros((8, tq), I32))
        return jnp.sum(acc, axis=0, keepdims=True)

    def decide(p, cnt, kth_u, need):
        one = cnt >= need
        kth_u = jnp.where(one, kth_u | lax.shift_right_logical(jnp.int32(INT_MIN), p), kth_u)
        return kth_u, jnp.where(one, need, need - cnt), jnp.where(one, 0, -1)

    zero = jnp.zeros((1, tq), I32)
    kth_u, need, flip = decide(jnp.int32(0), sweep(None, None, 0), zero, jnp.full((1, tq), TOPK, I32))

    def bit_step(p, carry):
        kth_u, need, flip = carry
        return decide(p, sweep(p - 1, flip, p), kth_u, need)

    kth_u, need, flip = lax.fori_loop(1, 32, bit_step, (kth_u, need, flip))
    sweep(31, flip, None)
    need = jnp.where(kth_u == 0, 0, need)

    def tie_count(t):
        c = cand_ref[pl.ds(pl.multiple_of(t * 8, 8), 8), :]
        return c, jnp.sum(lax.population_count(c), axis=0, keepdims=True)

    def find(t, carry):
        seen, part, words = carry
        c, n = tie_count(t)
        hit = (seen < need) & (seen + n > need)
        return seen + n, jnp.where(hit, need - seen, part), jnp.where(hit, c, words)

    _, part, words = lax.fori_loop(0, nt, find, (zero, zero, jnp.zeros((8, tq), I32)))
    keep = jnp.zeros((8, tq), I32)
    before = zero
    for k in range(32):
        bit = lax.shift_right_logical(words, jnp.int32(31 - k)) & 1
        rank = before + _sublane_prefix(bit)
        keep = keep | jnp.where((bit == 1) & (rank <= part), INT_MIN if k == 0 else 1 << (31 - k), 0)
        before = rank[7:8, :]

    def emit(t, seen):
        c, n = tie_count(t)
        at = pl.ds(pl.multiple_of(t * 8, 8), 8)
        whole = seen + n <= need
        partial = (seen < need) & jnp.logical_not(whole)
        sel_ref[0, at, :] = gt_ref[at, :] | jnp.where(whole, c, jnp.where(partial, keep, 0))
        return seen + n

    lax.fori_loop(0, nt, emit, zero)

    def clear(t, c):
        sel_ref[0, pl.ds(pl.multiple_of(t * 8, 8), 8), :] = jnp.zeros((8, tq), I32)
        return c

    lax.fori_loop(nt, sel_ref.shape[1] // 8, clear, 0)


def _idx_sel(qi_hm, w_t, ki, tq):
    s = ki.shape[0]
    nrow = s // tq * 8
    return pl.pallas_call(
        functools.partial(_sel_body, tq=tq, grp=_pick(s // tq, 8)),
        grid=(s // tq,),
        in_specs=[pl.BlockSpec((N_IDX_HEADS, tq, IDX_DIM), lambda i: (0, i, 0)),
                  pl.BlockSpec((N_IDX_HEADS, tq), lambda i: (0, i)),
                  pl.BlockSpec((s, IDX_DIM), lambda i: (0, 0))],
        out_specs=pl.BlockSpec((1, nrow, tq), lambda i: (i, 0, 0)),
        out_shape=jax.ShapeDtypeStruct((s // tq, nrow, tq), I32),
        scratch_shapes=[pltpu.VMEM((32, nrow, tq), I32), pltpu.VMEM((nrow, tq), I32),
                        pltpu.VMEM((nrow, tq), I32)],
        compiler_params=_params(("arbitrary",)),
        name="idx_sel",
    )(qi_hm, w_t, ki)


def _attn_body(qb_ref, kb_ref, q_ref, k_ref, vt_ref, sel_ref,
               out_ref, m_ref, l_ref, acc_ref, s_ref, p_ref, bias_ref, *, tq, tk):
    step = pl.program_id(0)
    i = qb_ref[step]
    j = kb_ref[step]

    @pl.when(j == 0)
    def _():
        m_ref[...] = jnp.full(m_ref.shape, M_INIT, F32)
        l_ref[...] = jnp.zeros(l_ref.shape, F32)
        acc_ref[...] = jnp.zeros(acc_ref.shape, F32)

    for tile in range(tk // tq):
        words = sel_ref[0, tile * 8:(tile + 1) * 8, :]
        for k in range(32):
            bias_ref[pl.ds(tile * tq + 8 * k, 8), :] = jnp.where((words << k) < 0, 0.0, NEG_INF)
    bias = bias_ref[...]
    for h in range(N_HEADS):
        s_ref[h] = _nt(k_ref[h], q_ref[h]) + bias
    m_old = m_ref[...]
    m_new = jnp.maximum(m_old, jnp.concatenate(
        [jnp.max(s_ref[h], axis=0, keepdims=True) for h in range(N_HEADS)], axis=0))
    alpha = jnp.exp2(m_old - m_new)
    sums = []
    for h in range(N_HEADS):
        p = jnp.exp2(s_ref[h] - m_new[h:h + 1, :])
        sums.append(jnp.sum(p, axis=0, keepdims=True))
        p_ref[h] = p.astype(BF16)
    l_ref[...] = alpha * l_ref[...] + jnp.concatenate(sums, axis=0)
    m_ref[...] = m_new
    for h in range(N_HEADS):
        acc_ref[h] = alpha[h:h + 1, :] * acc_ref[h] + _dot(vt_ref[h], p_ref[h])

    @pl.when(j == ((i + 1) * tq - 1) // tk)
    def _():
        for h in range(N_HEADS):
            out_ref[h * HEAD_DIM:(h + 1) * HEAD_DIM, :] = (acc_ref[h] / l_ref[h:h + 1, :]).astype(BF16)


def _attn(q_hm, k_hm, vt_hm, sel, tq, tk):
    s = q_hm.shape[1]
    assert tk % tq == 0
    qb, kb = [], []
    for i in range(s // tq):
        for j in range(((i + 1) * tq - 1) // tk + 1):
            qb.append(i)
            kb.append(j)
    qb = jnp.asarray(qb, I32)
    kb = jnp.asarray(kb, I32)
    grid_spec = pltpu.PrefetchScalarGridSpec(
        num_scalar_prefetch=2,
        grid=(int(qb.shape[0]),),
        in_specs=[pl.BlockSpec((N_HEADS, tq, HEAD_DIM), lambda t, qb, kb: (0, qb[t], 0)),
                  pl.BlockSpec((N_HEADS, tk, HEAD_DIM), lambda t, qb, kb: (0, kb[t], 0)),
                  pl.BlockSpec((N_HEADS, HEAD_DIM, tk), lambda t, qb, kb: (0, 0, kb[t])),
                  pl.BlockSpec((1, tk // tq * 8, tq), lambda t, qb, kb: (qb[t], kb[t], 0))],
        out_specs=pl.BlockSpec((ATT_WIDTH, tq), lambda t, qb, kb: (0, qb[t])),
        scratch_shapes=[pltpu.VMEM((N_HEADS, tq), F32),
                        pltpu.VMEM((N_HEADS, tq), F32),
                        pltpu.VMEM((N_HEADS, HEAD_DIM, tq), F32),
                        pltpu.VMEM((N_HEADS, tk, tq), F32),
                        pltpu.VMEM((N_HEADS, tk, tq), BF16),
                        pltpu.VMEM((tk, tq), F32)],
    )
    return pl.pallas_call(
        functools.partial(_attn_body, tq=tq, tk=tk),
        grid_spec=grid_spec,
        out_shape=jax.ShapeDtypeStruct((ATT_WIDTH, s), BF16),
        compiler_params=_params(("arbitrary",)),
        name="attn",
    )(qb, kb, q_hm, k_hm, vt_hm, sel)


NEW_PAD = 128


def _sample_attn_body(q_ref, qi_ref, w_ref, kic_ref, kin_ref, kc_ref, vc_ref, kn_ref, vn_ref,
                      out_ref, kinp, knp, vnp, *, past, tn):
    kinp[...] = jnp.zeros(kinp.shape, BF16)
    kinp[0:tn, :] = kin_ref[0].astype(BF16)
    knp[...] = jnp.zeros(knp.shape, BF16)
    vnp[...] = jnp.zeros(vnp.shape, BF16)
    for h in range(N_HEADS):
        knp[h, 0:tn, :] = kn_ref[0, h]
        vnp[h, 0:tn, :] = vn_ref[0, h]

    w = w_ref[0]
    kic = kic_ref[0].astype(BF16)
    kin = kinp[...]
    sc = None
    sn = None
    for h in range(N_IDX_HEADS):
        qih = qi_ref[0, h]
        wc = w[:, h:h + 1]
        tc = wc * jnp.maximum(_nt(qih, kic), 0.0)
        tnw = wc * jnp.maximum(_nt(qih, kin), 0.0)
        sc = tc if sc is None else sc + tc
        sn = tnw if sn is None else sn + tnw
    new_ok = lax.broadcasted_iota(I32, (tn, NEW_PAD), 1) < tn
    keyc = _order_key(sc)
    keyn = jnp.where(new_ok, _order_key(sn), INT_MIN)

    def count(pc, pn):
        return (jnp.sum(jnp.where(pc, 1, 0), axis=1, keepdims=True)
                + jnp.sum(jnp.where(pn, 1, 0), axis=1, keepdims=True))

    def bit_step(p, u):
        cand_u = u | (jnp.int32(1) << (31 - p))
        cand = cand_u ^ INT_MIN
        return jnp.where(count(keyc >= cand, keyn >= cand) >= TOPK, cand_u, u)

    kth = lax.fori_loop(0, 32, bit_step, jnp.zeros((tn, 1), I32)) ^ INT_MIN
    need = (TOPK - count(keyc > kth, keyn > kth)).astype(F32)

    cw = 256
    tri = jnp.where(lax.broadcasted_iota(I32, (cw, cw), 0) <= lax.broadcasted_iota(I32, (cw, cw), 1),
                    1.0, 0.0).astype(BF16)
    seen = jnp.zeros((tn, 1), F32)
    bias_c = []
    for c in range(past // cw):
        kc_ = keyc[:, c * cw:(c + 1) * cw]
        eq = kc_ == kth
        pre = _dot(jnp.where(eq, 1.0, 0.0).astype(BF16), tri)
        sel = (kc_ > kth) | (eq & (pre + seen <= need))
        bias_c.append(jnp.where(sel, 0.0, NEG_INF))
        seen = seen + pre[:, cw - 1:cw]
    bias_c = jnp.concatenate(bias_c, axis=1)
    eqn = keyn == kth
    pren = _dot(jnp.where(eqn, 1.0, 0.0).astype(BF16), tri[:NEW_PAD, :NEW_PAD])
    bias_n = jnp.where(new_ok & ((keyn > kth) | (eqn & (pren + seen <= need))), 0.0, NEG_INF)

    kc = kc_ref[0].astype(BF16)
    vc = vc_ref[0].astype(BF16)
    for h in range(N_HEADS):
        qh = q_ref[0, h]
        head = slice(h * HEAD_DIM, (h + 1) * HEAD_DIM)
        s_c = _nt(qh, kc[:, head]) + bias_c
        s_n = _nt(qh, knp[h]) + bias_n
        m = jnp.maximum(jnp.max(s_c, axis=1, keepdims=True), jnp.max(s_n, axis=1, keepdims=True))
        p_c = jnp.exp2(s_c - m)
        p_n = jnp.exp2(s_n - m)
        l = jnp.sum(p_c, axis=1, keepdims=True) + jnp.sum(p_n, axis=1, keepdims=True)
        o = _dot(p_c.astype(BF16), vc[:, head]) + _dot(p_n.astype(BF16), vnp[h])
        out_ref[0, h] = (o / l).astype(BF16)


def _sample_attn(q_hm, qi_hm, w, kic, kin, kc, vc, kn_hm, vn_hm):
    nb, _, tn, _ = q_hm.shape
    past = kic.shape[1]
    b4 = lambda a: pl.BlockSpec((1,) + a.shape[1:], lambda b: (b, 0, 0, 0))
    b3 = lambda a: pl.BlockSpec((1,) + a.shape[1:], lambda b: (b, 0, 0))
    return pl.pallas_call(
        functools.partial(_sample_attn_body, past=past, tn=tn),
        grid=(nb,),
        in_specs=[b4(q_hm), b4(qi_hm), b3(w), b3(kic), b3(kin), b3(kc), b3(vc), b4(kn_hm), b4(vn_hm)],
        out_specs=pl.BlockSpec((1, N_HEADS, tn, HEAD_DIM), lambda b: (b, 0, 0, 0)),
        out_shape=jax.ShapeDtypeStruct((nb, N_HEADS, tn, HEAD_DIM), BF16),
        scratch_shapes=[pltpu.VMEM((NEW_PAD, IDX_DIM), BF16),
                        pltpu.VMEM((N_HEADS, NEW_PAD, HEAD_DIM), BF16),
                        pltpu.VMEM((N_HEADS, NEW_PAD, HEAD_DIM), BF16)],
        compiler_params=_params(("parallel",)),
        name="sample_attn",
    )(q_hm, qi_hm, w, kic, kin, kc, vc, kn_hm, vn_hm)


def _cmpx(a, b):
    return jnp.maximum(a, b), jnp.minimum(a, b)


def _bitonic_merge_desc(v):
    n = len(v)
    v = list(v)
    j = n // 2
    while j >= 1:
        for i in range(n):
            l = i ^ j
            if l > i:
                v[i], v[l] = _cmpx(v[i], v[l])
        j //= 2
    return v


def _bitonic_sort_desc(v):
    n = len(v)
    v = list(v)
    k = 2
    while k <= n:
        j = k // 2
        while j >= 1:
            for i in range(n):
                l = i ^ j
                if l > i:
                    hi, lo = _cmpx(v[i], v[l])
                    v[i], v[l] = (hi, lo) if (i & k) == 0 else (lo, hi)
            j //= 2
        k *= 2
    return v


def _merge_top(a, b):
    n = len(a)
    return _bitonic_merge_desc([jnp.maximum(a[k], b[n - 1 - k]) for k in range(n)])


def _top16_desc(vals):
    groups = [_bitonic_sort_desc(vals[g:g + PEER_TOPK]) for g in range(0, len(vals), PEER_TOPK)]
    while len(groups) > 1:
        groups = [_merge_top(groups[g], groups[g + 1]) for g in range(0, len(groups), 2)]
    return groups[0]


def _mid_body(x_ref, att_ref, conv_ref, woa_ref, woc_ref, g_ref, wqt_ref, a1_ref, a2_ref,
              h_ref, xn_ref, qht_ref, st_ref):
    h = x_ref[...] + _dot(att_ref[...], woa_ref[...]) + _dot(conv_ref[...], woc_ref[...])
    h_ref[...] = h
    xn = _rms(h, g_ref[...]).astype(BF16)
    xn_ref[...] = xn
    qht = _nt(wqt_ref[...], xn).astype(BF16)
    qht_ref[...] = qht
    half = PEER_HEADS * PEER_HALF
    s1 = _dot(a1_ref[...], qht[:half])
    s2 = _dot(a2_ref[...], qht[half:])
    rows = lambda s: [s[r * PEER_HEADS:(r + 1) * PEER_HEADS, :] for r in range(PEER_KEYS)]
    v1 = _top16_desc(rows(s1))
    v2 = _top16_desc(rows(s2))
    ninf = jnp.full(v1[0].shape, NEG_INF, F32)
    top = [v1[0] + v2[b] for b in range(PEER_TOPK)]
    for a in range(1, PEER_TOPK):
        n_a = PEER_TOPK // (a + 1)
        top = _merge_top(top, [v1[a] + v2[b] if b < n_a else ninf for b in range(PEER_TOPK)])
    z = jnp.exp(top[0] - top[0])
    for k in range(1, PEER_TOPK):
        z = z + jnp.exp(top[k] - top[0])
    st_ref[0:8, :] = v1[0]
    st_ref[8:16, :] = v2[0]
    st_ref[16:24, :] = top[PEER_TOPK - 1]
    st_ref[24:32, :] = 1.0 / z


def _mid(x, att, conv, woa, woc, g, wqt, a1, a2, tb):
    t = x.shape[0]
    row = lambda w: pl.BlockSpec((tb, w), lambda i: (i, 0))
    col = lambda r: pl.BlockSpec((r, tb), lambda i: (0, i))
    full = lambda a: pl.BlockSpec(a.shape, lambda i: (0, 0))
    return pl.pallas_call(
        _mid_body,
        grid=(t // tb,),
        in_specs=[row(D_MODEL), row(ATT_WIDTH), row(CONV_CH), full(woa), full(woc), full(g),
                  full(wqt), full(a1), full(a2)],
        out_specs=[row(D_MODEL), row(D_MODEL), col(D_MODEL), col(32)],
        out_shape=[jax.ShapeDtypeStruct((t, D_MODEL), F32),
                   jax.ShapeDtypeStruct((t, D_MODEL), BF16),
                   jax.ShapeDtypeStruct((D_MODEL, t), BF16),
                   jax.ShapeDtypeStruct((32, t), F32)],
        compiler_params=_params(("parallel",)),
        name="mid",
    )(x, att, conv, woa, woc, g, wqt, a1, a2)


def _peer_body(xn_ref, qht_ref, st_ref, a1_ref, a2_ref, u_ref, vt_ref, out_ref,
               s1_ref, s2_ref, e1_ref, e2_ref, hid_ref, *, eb):
    e = pl.program_id(1)
    nsub = eb // PEER_KEYS
    half = PEER_HEADS * PEER_HALF

    @pl.when(e == 0)
    def _():
        qht = qht_ref[...]
        s1 = _dot(a1_ref[...], qht[:half])
        s2 = _dot(a2_ref[...], qht[half:])
        s1_ref[...] = s1
        s2_ref[...] = s2
        for h in range(PEER_HEADS):
            r = slice(h * PEER_KEYS, (h + 1) * PEER_KEYS)
            e1_ref[r, :] = jnp.exp(s1[r] - st_ref[h:h + 1, :])
            e2_ref[r, :] = jnp.exp(s2[r] - st_ref[8 + h:9 + h, :]) * st_ref[24 + h:25 + h, :] * 0.5
        out_ref[...] = jnp.zeros(out_ref.shape, F32)

    xn = xn_ref[...]
    for ii in range(nsub):
        i = e * nsub + ii
        a = _nt(u_ref[ii * PEER_KEYS:(ii + 1) * PEER_KEYS, :], xn)
        gate = None
        for h in range(PEER_HEADS):
            r = slice(h * PEER_KEYS, (h + 1) * PEER_KEYS)
            pair = s1_ref[pl.ds(h * PEER_KEYS + i, 1), :] + s2_ref[r, :]
            g = e1_ref[pl.ds(h * PEER_KEYS + i, 1), :] * e2_ref[r, :]
            g = jnp.where(pair >= st_ref[16 + h:17 + h, :], g, 0.0)
            gate = g if gate is None else gate + g
        hid = a * (1.0 + lax.erf(a * INV_SQRT2)) * gate
        hid_ref[ii * PEER_KEYS:(ii + 1) * PEER_KEYS, :] = hid.astype(BF16)
    out_ref[...] += _dot(vt_ref[...], hid_ref[...])


def _peer(xn, qht, st, a1, a2, u, vt, tb, eb):
    t = xn.shape[0]
    ne = u.shape[0] // eb
    full = lambda a: pl.BlockSpec(a.shape, lambda i, e: (0, 0))
    return pl.pallas_call(
        functools.partial(_peer_body, eb=eb),
        grid=(t // tb, ne),
        in_specs=[pl.BlockSpec((tb, D_MODEL), lambda i, e: (i, 0)),
                  pl.BlockSpec((D_MODEL, tb), lambda i, e: (0, i)),
                  pl.BlockSpec((32, tb), lambda i, e: (0, i)),
                  full(a1), full(a2),
                  pl.BlockSpec((eb, D_MODEL), lambda i, e: (e, 0)),
                  pl.BlockSpec((D_MODEL, eb), lambda i, e: (0, e))],
        out_specs=pl.BlockSpec((D_MODEL, tb), lambda i, e: (0, i)),
        out_shape=jax.ShapeDtypeStruct((D_MODEL, t), F32),
        scratch_shapes=[pltpu.VMEM((PEER_HEADS * PEER_KEYS, tb), F32)] * 4
                       + [pltpu.VMEM((eb, tb), BF16)],
        compiler_params=_params(("parallel", "arbitrary")),
        name="peer",
    )(xn, qht, st, a1, a2, u, vt)


def _final_body(h_ref, pt_ref, g_ref, y_ref):
    y_ref[...] = _rms(h_ref[...] + pt_ref[...].T, g_ref[...])


def _final(h, pt, g, tb):
    t = h.shape[0]
    return pl.pallas_call(
        _final_body,
        grid=(t // tb,),
        in_specs=[pl.BlockSpec((tb, D_MODEL), lambda i: (i, 0)),
                  pl.BlockSpec((D_MODEL, tb), lambda i: (0, i)),
                  pl.BlockSpec((1, D_MODEL), lambda i: (0, 0))],
        out_specs=pl.BlockSpec((tb, D_MODEL), lambda i: (i, 0)),
        out_shape=jax.ShapeDtypeStruct((t, D_MODEL), F32),
        compiler_params=_params(("parallel",)),
        name="final",
    )(h, pt, g)


def _pick(n, pref):
    b = min(n, pref)
    while n % b:
        b //= 2
    return b


def _head_major(a, nh, dt=BF16):
    t = a.shape[0]
    return a.reshape(t, nh, a.shape[1] // nh).transpose(1, 0, 2).astype(dt)


def _token_stage(x, att, conv, p, tb_mid, tb_peer):
    h, xn, qht, st = _mid(x, att, conv, p["woa"], p["woc"], p["ffn_g"], p["wqt"], p["a1s"], p["a2s"], tb_mid)
    pt = _peer(xn, qht, st, p["a1d"], p["a2d"], p["u"], p["vt"], tb_peer, 1024)
    return _final(h, pt, p["final_g"], tb_mid)


def kernel(x_prompt, x_sample, cache_k, cache_v, cache_kidx, state_conv, attn_norm_g, w_in, conv_w, conv_b,
           conv_ln_g, conv_ln_b, w_out, ffn_norm_g, peer_wq, peer_subkeys, peer_u, peer_v, final_norm_g):
    assert w_in.shape[0] == 1, "single layer"
    nbp, s, _ = x_prompt.shape
    nb, tn, _ = x_sample.shape
    assert nbp == 1
    past = cache_k.shape[2]

    wi = w_in[0]
    o_qi = 3 * ATT_WIDTH
    o_ki = o_qi + N_IDX_HEADS * IDX_DIM
    o_wi = o_ki + IDX_DIM
    o_glu = o_wi + N_IDX_HEADS
    wqkv = wi[:, :o_qi].astype(BF16)
    widx = jnp.concatenate([wi[:, o_qi:o_glu], jnp.zeros((D_MODEL, 128 - IDX_DIM - N_IDX_HEADS), F32)],
                           axis=1).astype(BF16)
    wglu = wi[:, o_glu:].astype(BF16)
    g_attn = attn_norm_g[0][None, :]
    cw = jnp.concatenate([conv_w[0], jnp.zeros((CONV_TAIL - CONV_WIDTH, CONV_CH), F32)], axis=0)
    cb, cg, cbeta = conv_b[0][None, :], conv_ln_g[0][None, :], conv_ln_b[0][None, :]
    eye = jnp.eye(PEER_HEADS, dtype=F32)
    sk = peer_subkeys[0]
    a_s = [jnp.einsum("id,hg->ihgd", sk[c], eye).reshape(PEER_KEYS * PEER_HEADS, PEER_HEADS * PEER_HALF)
           .astype(BF16) for c in range(2)]
    a_d = [jnp.einsum("id,hg->higd", sk[c], eye).reshape(PEER_KEYS * PEER_HEADS, PEER_HEADS * PEER_HALF)
           .astype(BF16) for c in range(2)]
    wq = peer_wq[0].reshape(D_MODEL, PEER_HEADS, 2, PEER_HALF).transpose(2, 1, 3, 0)
    p = {
        "woa": w_out[0][:ATT_WIDTH].astype(BF16), "woc": w_out[0][ATT_WIDTH:].astype(BF16),
        "ffn_g": ffn_norm_g[0][None, :], "final_g": final_norm_g[None, :],
        "wqt": wq.reshape(D_MODEL, D_MODEL).astype(BF16),
        "a1s": a_s[0], "a2s": a_s[1], "a1d": a_d[0], "a2d": a_d[1],
        "u": peer_u[0].astype(BF16), "vt": peer_v[0].T.astype(BF16),
    }
    w_scale = (N_IDX_HEADS ** -0.5) * (IDX_DIM ** -0.5)

    xp = x_prompt[0]
    tb = _pick(s, 256)
    q, k, v, qi, kw, glu = _in_proj(xp, g_attn, wqkv, widx, wglu, tb)
    conv_p, tail_p = _conv(glu, jnp.zeros((1, CONV_TAIL, CONV_CH), F32), cw, cb, cg, cbeta, 1, tb)
    ki = kw[:, :IDX_DIM]
    ki_b = ki.astype(BF16)
    qi_hm = _head_major(qi, N_IDX_HEADS)
    w_t = (kw[:, IDX_DIM:IDX_DIM + N_IDX_HEADS] * w_scale).T
    tq = _pick(s, 256)
    sel = _idx_sel(qi_hm, w_t, ki_b, tq)
    q_hm = _head_major(q, N_HEADS)
    k_hm = _head_major(k, N_HEADS)
    vt_hm = v.reshape(s, N_HEADS, HEAD_DIM).transpose(1, 2, 0).astype(BF16)
    att_t = _attn(q_hm, k_hm, vt_hm, sel, tq, _pick(s, 1024))
    y_p = _token_stage(xp, att_t.T, conv_p, p, tb, _pick(s, 512))

    xs = x_sample.reshape(nb * tn, D_MODEL)
    ts = nb * tn
    tbs = _pick(ts, 256)
    qs, ks, vs, qis, kws, glus = _in_proj(xs, g_attn, wqkv, widx, wglu, tbs)
    pad = jnp.zeros((nb, CONV_TAIL - (CONV_WIDTH - 1), CONV_CH), F32)
    conv_s, tail_s = _conv(glus, jnp.concatenate([pad, state_conv[0]], axis=1), cw, cb, cg, cbeta, nb, tn)
    hm4 = lambda a, nh, dt=BF16: a.reshape(nb, tn, nh, a.shape[1] // nh).transpose(0, 2, 1, 3).astype(dt)
    kis = kws[:, :IDX_DIM]
    att_s = _sample_attn(
        hm4(qs, N_HEADS), hm4(qis, N_IDX_HEADS),
        (kws[:, IDX_DIM:IDX_DIM + N_IDX_HEADS] * w_scale).reshape(nb, tn, N_IDX_HEADS),
        cache_kidx[0], kis.reshape(nb, tn, IDX_DIM),
        cache_k[0].reshape(nb, past, ATT_WIDTH), cache_v[0].reshape(nb, past, ATT_WIDTH),
        hm4(ks, N_HEADS), hm4(vs, N_HEADS))
    att_s = att_s.transpose(0, 2, 1, 3).reshape(ts, ATT_WIDTH)
    y_s = _token_stage(xs, att_s, conv_s, p, tbs, tbs)

    hd = (N_HEADS, HEAD_DIM)
    keep = CONV_TAIL - (CONV_WIDTH - 1)
    return (y_p[None], y_s.reshape(nb, tn, D_MODEL),
            k.reshape(1, 1, s, *hd), v.reshape(1, 1, s, *hd), ki[None, None], tail_p[None, :, keep:],
            ks.reshape(1, nb, tn, *hd), vs.reshape(1, nb, tn, *hd), kis.reshape(1, nb, tn, IDX_DIM),
            tail_s[None, :, keep:])
```

```python
import functools

import jax
import jax.numpy as jnp
from jax import lax
from jax.experimental import pallas as pl
from jax.experimental.pallas import tpu as pltpu

F32 = jnp.float32
BF16 = jnp.bfloat16
I32 = jnp.int32

D_MODEL = 1024
N_HEADS = 8
HEAD_DIM = 64
ATT_WIDTH = N_HEADS * HEAD_DIM
N_IDX_HEADS = 4
IDX_DIM = 64
TOPK = 256
CHUNK_SHIFT = 6
CONV_CH = 512
CONV_WIDTH = 31
CONV_TAIL = 32
PEER_HEADS = 8
PEER_KEYS = 128
PEER_HALF = 64
PEER_TOPK = 16
EPS = 1e-6
INT_MIN = -2147483648
ORDER_MASK = 0x7FFFFFFF
INV_SQRT2 = 0.7071067811865476
LOG2E = 1.4426950408889634
NEG_INF = float("-inf")
M_INIT = -1e30

VMEM_LIMIT = 56 * 1024 * 1024


def _params(sem, vmem=VMEM_LIMIT):
    return pltpu.CompilerParams(dimension_semantics=sem, vmem_limit_bytes=vmem)


def _nt(a, b):
    return lax.dot_general(a, b, (((1,), (1,)), ((), ())), preferred_element_type=F32)


def _dot(a, b):
    return jnp.dot(a, b, preferred_element_type=F32)


def _rms(x, g):
    return x * lax.rsqrt(jnp.mean(x * x, axis=-1, keepdims=True) + EPS) * g


def _order_key(x):
    b = pltpu.bitcast(x, I32)
    return b ^ ((b >> 31) & ORDER_MASK)


def _in_proj_body(x_ref, g_ref, wqkv_ref, widx_ref, wglu_ref,
                  k_ref, v_ref, kw_ref, glu_ref, qh_ref, kh_ref, vth_ref, qih_ref):
    xn = _rms(x_ref[...], g_ref[...]).astype(BF16)
    qkv = _dot(xn, wqkv_ref[...])
    q = qkv[:, :ATT_WIDTH] * (HEAD_DIM ** -0.5 * LOG2E)
    k = qkv[:, ATT_WIDTH:2 * ATT_WIDTH]
    v = qkv[:, 2 * ATT_WIDTH:]
    k_ref[...] = k
    v_ref[...] = v
    for h in range(N_HEADS):
        head = slice(h * HEAD_DIM, (h + 1) * HEAD_DIM)
        qh_ref[h] = q[:, head].astype(BF16)
        kh_ref[h] = k[:, head].astype(BF16)
    vth_ref[...] = v.T.reshape(N_HEADS, HEAD_DIM, v.shape[0]).astype(BF16)
    ix = _dot(xn, widx_ref[...])
    for h in range(N_IDX_HEADS):
        qih_ref[h] = ix[:, h * IDX_DIM:(h + 1) * IDX_DIM].astype(BF16)
    kw_ref[...] = ix[:, N_IDX_HEADS * IDX_DIM:]
    glu_ref[...] = _dot(xn, wglu_ref[...])


def _in_proj(x, g, wqkv, widx, wglu, tb):
    t = x.shape[0]
    row = lambda w: pl.BlockSpec((tb, w), lambda i: (i, 0))
    full = lambda a: pl.BlockSpec(a.shape, lambda i: (0, 0))
    heads = lambda n: pl.BlockSpec((n, tb, HEAD_DIM), lambda i: (0, i, 0))
    return pl.pallas_call(
        _in_proj_body,
        grid=(t // tb,),
        in_specs=[row(D_MODEL), full(g), full(wqkv), full(widx), full(wglu)],
        out_specs=[row(ATT_WIDTH), row(ATT_WIDTH), row(128), row(2 * CONV_CH),
                   heads(N_HEADS), heads(N_HEADS),
                   pl.BlockSpec((N_HEADS, HEAD_DIM, tb), lambda i: (0, 0, i)), heads(N_IDX_HEADS)],
        out_shape=[jax.ShapeDtypeStruct((t, ATT_WIDTH), F32),
                   jax.ShapeDtypeStruct((t, ATT_WIDTH), F32),
                   jax.ShapeDtypeStruct((t, 128), F32),
                   jax.ShapeDtypeStruct((t, 2 * CONV_CH), F32),
                   jax.ShapeDtypeStruct((N_HEADS, t, HEAD_DIM), BF16),
                   jax.ShapeDtypeStruct((N_HEADS, t, HEAD_DIM), BF16),
                   jax.ShapeDtypeStruct((N_HEADS, HEAD_DIM, t), BF16),
                   jax.ShapeDtypeStruct((N_IDX_HEADS, t, IDX_DIM), BF16)],
        compiler_params=_params(("parallel",)),
        name="in_proj",
    )(x, g, wqkv, widx, wglu)


def _conv_body(glu_ref, tail0_ref, w_ref, b_ref, g_ref, beta_ref, out_ref, tail_ref, ubuf, *, tb):
    j = pl.program_id(1)

    @pl.when(j == 0)
    def _():
        ubuf[0:CONV_TAIL, :] = tail0_ref[0]

    @pl.when(j > 0)
    def _():
        ubuf[0:CONV_TAIL, :] = ubuf[tb:tb + CONV_TAIL, :]

    glu = glu_ref[...]
    ubuf[CONV_TAIL:CONV_TAIL + tb, :] = glu[:, :CONV_CH] * jax.nn.sigmoid(glu[:, CONV_CH:])
    off = CONV_TAIL - (CONV_WIDTH - 1)
    acc = ubuf[off:off + tb, :] * w_ref[0:1, :]
    for t in range(1, CONV_WIDTH):
        acc = acc + ubuf[off + t:off + t + tb, :] * w_ref[t:t + 1, :]
    y = acc + b_ref[...]
    mu = jnp.mean(y, axis=-1, keepdims=True)
    d = y - mu
    var = jnp.mean(d * d, axis=-1, keepdims=True)
    z = d * lax.rsqrt(var + EPS) * g_ref[...] + beta_ref[...]
    out_ref[...] = (z * jax.nn.sigmoid(z)).astype(BF16)
    tail_ref[0] = ubuf[tb:tb + CONV_TAIL, :]


def _conv(glu, tail0, w, b, g, beta, nseq, tb):
    t = glu.shape[0]
    nblk = t // (nseq * tb)
    full = lambda a: pl.BlockSpec(a.shape, lambda s, j: (0, 0))
    return pl.pallas_call(
        functools.partial(_conv_body, tb=tb),
        grid=(nseq, nblk),
        in_specs=[pl.BlockSpec((tb, 2 * CONV_CH), lambda s, j: (s * nblk + j, 0)),
                  pl.BlockSpec((1, CONV_TAIL, CONV_CH), lambda s, j: (s, 0, 0)),
                  full(w), full(b), full(g), full(beta)],
        out_specs=[pl.BlockSpec((tb, CONV_CH), lambda s, j: (s * nblk + j, 0)),
                   pl.BlockSpec((1, CONV_TAIL, CONV_CH), lambda s, j: (s, 0, 0))],
        out_shape=[jax.ShapeDtypeStruct((t, CONV_CH), BF16),
                   jax.ShapeDtypeStruct((nseq, CONV_TAIL, CONV_CH), F32)],
        scratch_shapes=[pltpu.VMEM((tb + CONV_TAIL, CONV_CH), F32)],
        compiler_params=_params(("parallel", "arbitrary")),
        name="conv",
    )(glu, tail0, w, b, g, beta)


def _idx_scores_t(ki_tile, qi_ref, w):
    acc = None
    for h in range(N_IDX_HEADS):
        term = w[h:h + 1, :] * jnp.maximum(_nt(ki_tile, qi_ref[h]), 0.0)
        acc = term if acc is None else acc + term
    return acc


def _chunk_limit(q0, tq):
    qpos = q0 + lax.broadcasted_iota(I32, (1, tq), 1)
    return ((qpos >> CHUNK_SHIFT) + 1) << CHUNK_SHIFT


def _bit_planes(words):
    x = list(words)
    mask, j = 0x0000FFFF, 16
    while j:
        k = 0
        while k < 32:
            t = (x[k] ^ lax.shift_right_logical(x[k + j], jnp.int32(j))) & mask
            x[k] = x[k] ^ t
            x[k + j] = x[k + j] ^ (t << j)
            k = (k + j + 1) & ~j
        j >>= 1
        mask ^= (mask << j) & 0xFFFFFFFF
    return x


def _sublane_prefix(x):
    sub = lax.broadcasted_iota(I32, x.shape, 0)
    for sh in (1, 2, 4):
        x = x + jnp.where(sub >= sh, pltpu.roll(x, sh, axis=0), 0)
    return x


def _sel_body(qi_ref, w_ref, ki_ref, sel_ref, planes_ref, cand_ref, gt_ref, *, tq, grp):
    i = pl.program_id(0)
    nt = i + 1
    ng = (nt + grp - 1) // grp
    w = w_ref[...]
    limit = _chunk_limit(i * tq, tq)
    rows = lax.broadcasted_iota(I32, (tq, tq), 0)
    assert tq == 8 * 32

    def fill(t, diagonal):
        r0 = pl.multiple_of(t * tq, tq)
        b = pltpu.bitcast(_idx_scores_t(ki_ref[pl.ds(r0, tq), :], qi_ref, w), I32)
        u = b ^ ((b >> 31) | INT_MIN)
        if diagonal:
            u = jnp.where(rows + r0 < limit, u, 0)
        planes = _bit_planes([u[8 * k:8 * k + 8, :] for k in range(32)])
        at = pl.ds(pl.multiple_of(t * 8, 8), 8)
        for p in range(32):
            planes_ref[p, at, :] = planes[p]
        cand_ref[at, :] = jnp.full((8, tq), -1, I32)
        gt_ref[at, :] = jnp.zeros((8, tq), I32)

    def fill_full(t, c):
        fill(t, False)
        return c

    lax.fori_loop(0, nt - 1, fill_full, 0)
    fill(nt - 1, True)

    def pad(t, c):
        at = pl.ds(pl.multiple_of(t * 8, 8), 8)
        planes_ref[:, at, :] = jnp.zeros((32, 8, tq), I32)
        cand_ref[at, :] = jnp.zeros((8, tq), I32)
        gt_ref[at, :] = jnp.zeros((8, tq), I32)
        return c

    lax.fori_loop(nt, ng * grp, pad, 0)

    def sweep(p_prev, flip, p_next):
        def group(g, acc):
            at = pl.ds(pl.multiple_of(g * (grp * 8), grp * 8), grp * 8)
            c = cand_ref[at, :]
            if p_prev is not None:
                prev = planes_ref[p_prev, at, :]
                gt_ref[at, :] = gt_ref[at, :] | (c & prev & flip)
                c = c & (prev ^ flip)
                cand_ref[at, :] = c
            if p_next is not None:
                hit = lax.population_count(c & planes_ref[p_next, at, :])
                acc = acc + jnp.sum(hit.reshape(grp, 8, tq), axis=0)
            return acc
        acc = lax.fori_loop(0, ng, group, jnp.zeros((8, tq), I32))
        return jnp.sum(acc, axis=0, keepdims=True)

    def decide(p, cnt, kth_u, need):
        one = cnt >= need
        kth_u = jnp.where(one, kth_u | lax.shift_right_logical(jnp.int32(INT_MIN), p), kth_u)
        return kth_u, jnp.where(one, need, need - cnt), jnp.where(one, 0, -1)

    zero = jnp.zeros((1, tq), I32)
    kth_u, need, flip = decide(jnp.int32(0), sweep(None, None, 0), zero, jnp.full((1, tq), TOPK, I32))

    def bit_step(p, carry):
        kth_u, need, flip = carry
        return decide(p, sweep(p - 1, flip, p), kth_u, need)

    kth_u, need, flip = lax.fori_loop(1, 32, bit_step, (kth_u, need, flip))
    sweep(31, flip, None)
    need = jnp.where(kth_u == 0, 0, need)

    def tie_count(t):
        c = cand_ref[pl.ds(pl.multiple_of(t * 8, 8), 8), :]
        return c, jnp.sum(lax.population_count(c), axis=0, keepdims=True)

    def find(t, carry):
        seen, part, words = carry
        c, n = tie_count(t)
        hit = (seen < need) & (seen + n > need)
        return seen + n, jnp.where(hit, need - seen, part), jnp.where(hit, c, words)

    _, part, words = lax.fori_loop(0, nt, find, (zero, zero, jnp.zeros((8, tq), I32)))
    keep = jnp.zeros((8, tq), I32)
    before = zero
    for k in range(32):
        bit = lax.shift_right_logical(words, jnp.int32(31 - k)) & 1
        rank = before + _sublane_prefix(bit)
        keep = keep | jnp.where((bit == 1) & (rank <= part), INT_MIN if k == 0 else 1 << (31 - k), 0)
        before = rank[7:8, :]

    def emit(t, seen):
        c, n = tie_count(t)
        at = pl.ds(pl.multiple_of(t * 8, 8), 8)
        whole = seen + n <= need
        partial = (seen < need) & jnp.logical_not(whole)
        sel_ref[0, at, :] = gt_ref[at, :] | jnp.where(whole, c, jnp.where(partial, keep, 0))
        return seen + n

    lax.fori_loop(0, nt, emit, zero)

    def clear(t, c):
        sel_ref[0, pl.ds(pl.multiple_of(t * 8, 8), 8), :] = jnp.zeros((8, tq), I32)
        return c

    lax.fori_loop(nt, sel_ref.shape[1] // 8, clear, 0)


def _idx_sel(qi_hm, w_t, ki, tq):
    s = ki.shape[0]
    nrow = s // tq * 8
    return pl.pallas_call(
        functools.partial(_sel_body, tq=tq, grp=_pick(s // tq, 8)),
        grid=(s // tq,),
        in_specs=[pl.BlockSpec((N_IDX_HEADS, tq, IDX_DIM), lambda i: (0, i, 0)),
                  pl.BlockSpec((N_IDX_HEADS, tq), lambda i: (0, i)),
                  pl.BlockSpec((s, IDX_DIM), lambda i: (0, 0))],
        out_specs=pl.BlockSpec((1, nrow, tq), lambda i: (i, 0, 0)),
        out_shape=jax.ShapeDtypeStruct((s // tq, nrow, tq), I32),
        scratch_shapes=[pltpu.VMEM((32, nrow, tq), I32), pltpu.VMEM((nrow, tq), I32),
                        pltpu.VMEM((nrow, tq), I32)],
        compiler_params=_params(("arbitrary",)),
        name="idx_sel",
    )(qi_hm, w_t, ki)


def _attn_body(qb_ref, kb_ref, q_ref, k_ref, vt_ref, sel_ref,
               out_ref, m_ref, l_ref, acc_ref, s_ref, p_ref, bias_ref, *, tq, tk):
    step = pl.program_id(0)
    i = qb_ref[step]
    j = kb_ref[step]

    @pl.when(j == 0)
    def _():
        m_ref[...] = jnp.full(m_ref.shape, M_INIT, F32)
        l_ref[...] = jnp.zeros(l_ref.shape, F32)
        acc_ref[...] = jnp.zeros(acc_ref.shape, F32)

    for tile in range(tk // tq):
        words = sel_ref[0, tile * 8:(tile + 1) * 8, :]
        for k in range(32):
            bias_ref[pl.ds(tile * tq + 8 * k, 8), :] = jnp.where((words << k) < 0, 0.0, NEG_INF)
    bias = bias_ref[...]
    for h in range(N_HEADS):
        s_ref[h] = _nt(k_ref[h], q_ref[h]) + bias
    m_old = m_ref[...]
    m_new = jnp.maximum(m_old, jnp.concatenate(
        [jnp.max(s_ref[h], axis=0, keepdims=True) for h in range(N_HEADS)], axis=0))
    alpha = jnp.exp2(m_old - m_new)
    sums = []
    for h in range(N_HEADS):
        p = jnp.exp2(s_ref[h] - m_new[h:h + 1, :])
        sums.append(jnp.sum(p, axis=0, keepdims=True))
        p_ref[h] = p.astype(BF16)
    l_ref[...] = alpha * l_ref[...] + jnp.concatenate(sums, axis=0)
    m_ref[...] = m_new
    for h in range(N_HEADS):
        acc_ref[h] = alpha[h:h + 1, :] * acc_ref[h] + _dot(vt_ref[h], p_ref[h])

    @pl.when(j == ((i + 1) * tq - 1) // tk)
    def _():
        o = jnp.concatenate([acc_ref[h] / l_ref[h:h + 1, :] for h in range(N_HEADS)], axis=0)
        out_ref[...] = o.T.astype(BF16)


def _attn(q_hm, k_hm, vt_hm, sel, tq, tk):
    s = q_hm.shape[1]
    assert tk % tq == 0
    qb, kb = [], []
    for i in range(s // tq):
        for j in range(((i + 1) * tq - 1) // tk + 1):
            qb.append(i)
            kb.append(j)
    qb = jnp.asarray(qb, I32)
    kb = jnp.asarray(kb, I32)
    grid_spec = pltpu.PrefetchScalarGridSpec(
        num_scalar_prefetch=2,
        grid=(int(qb.shape[0]),),
        in_specs=[pl.BlockSpec((N_HEADS, tq, HEAD_DIM), lambda t, qb, kb: (0, qb[t], 0)),
                  pl.BlockSpec((N_HEADS, tk, HEAD_DIM), lambda t, qb, kb: (0, kb[t], 0)),
                  pl.BlockSpec((N_HEADS, HEAD_DIM, tk), lambda t, qb, kb: (0, 0, kb[t])),
                  pl.BlockSpec((1, tk // tq * 8, tq), lambda t, qb, kb: (qb[t], kb[t], 0))],
        out_specs=pl.BlockSpec((tq, ATT_WIDTH), lambda t, qb, kb: (qb[t], 0)),
        scratch_shapes=[pltpu.VMEM((N_HEADS, tq), F32),
                        pltpu.VMEM((N_HEADS, tq), F32),
                        pltpu.VMEM((N_HEADS, HEAD_DIM, tq), F32),
                        pltpu.VMEM((N_HEADS, tk, tq), F32),
                        pltpu.VMEM((N_HEADS, tk, tq), BF16),
                        pltpu.VMEM((tk, tq), F32)],
    )
    return pl.pallas_call(
        functools.partial(_attn_body, tq=tq, tk=tk),
        grid_spec=grid_spec,
        out_shape=jax.ShapeDtypeStruct((s, ATT_WIDTH), BF16),
        compiler_params=_params(("arbitrary",)),
        name="attn",
    )(qb, kb, q_hm, k_hm, vt_hm, sel)


NEW_PAD = 128


def _sample_attn_body(q_ref, qi_ref, w_ref, kic_ref, kin_ref, kc_ref, vc_ref, kn_ref, vn_ref,
                      out_ref, kinp, knp, vnp, *, past, tn):
    kinp[...] = jnp.zeros(kinp.shape, BF16)
    kinp[0:tn, :] = kin_ref[0].astype(BF16)
    knp[...] = jnp.zeros(knp.shape, BF16)
    vnp[...] = jnp.zeros(vnp.shape, BF16)
    for h in range(N_HEADS):
        knp[h, 0:tn, :] = kn_ref[0, h]
        vnp[h, 0:tn, :] = vn_ref[0, h]

    w = w_ref[0]
    kic = kic_ref[0].astype(BF16)
    kin = kinp[...]
    sc = None
    sn = None
    for h in range(N_IDX_HEADS):
        qih = qi_ref[0, h]
        wc = w[:, h:h + 1]
        tc = wc * jnp.maximum(_nt(qih, kic), 0.0)
        tnw = wc * jnp.maximum(_nt(qih, kin), 0.0)
        sc = tc if sc is None else sc + tc
        sn = tnw if sn is None else sn + tnw
    new_ok = lax.broadcasted_iota(I32, (tn, NEW_PAD), 1) < tn
    keyc = _order_key(sc)
    keyn = jnp.where(new_ok, _order_key(sn), INT_MIN)

    def count(pc, pn):
        return (jnp.sum(jnp.where(pc, 1, 0), axis=1, keepdims=True)
                + jnp.sum(jnp.where(pn, 1, 0), axis=1, keepdims=True))

    def bit_step(p, u):
        cand_u = u | (jnp.int32(1) << (31 - p))
        cand = cand_u ^ INT_MIN
        return jnp.where(count(keyc >= cand, keyn >= cand) >= TOPK, cand_u, u)

    kth = lax.fori_loop(0, 32, bit_step, jnp.zeros((tn, 1), I32)) ^ INT_MIN
    need = (TOPK - count(keyc > kth, keyn > kth)).astype(F32)

    cw = 256
    tri = jnp.where(lax.broadcasted_iota(I32, (cw, cw), 0) <= lax.broadcasted_iota(I32, (cw, cw), 1),
                    1.0, 0.0).astype(BF16)
    seen = jnp.zeros((tn, 1), F32)
    bias_c = []
    for c in range(past // cw):
        kc_ = keyc[:, c * cw:(c + 1) * cw]
        eq = kc_ == kth
        pre = _dot(jnp.where(eq, 1.0, 0.0).astype(BF16), tri)
        sel = (kc_ > kth) | (eq & (pre + seen <= need))
        bias_c.append(jnp.where(sel, 0.0, NEG_INF))
        seen = seen + pre[:, cw - 1:cw]
    bias_c = jnp.concatenate(bias_c, axis=1)
    eqn = keyn == kth
    pren = _dot(jnp.where(eqn, 1.0, 0.0).astype(BF16), tri[:NEW_PAD, :NEW_PAD])
    bias_n = jnp.where(new_ok & ((keyn > kth) | (eqn & (pren + seen <= need))), 0.0, NEG_INF)

    kc = kc_ref[0].astype(BF16)
    vc = vc_ref[0].astype(BF16)
    for h in range(N_HEADS):
        qh = q_ref[0, h]
        head = slice(h * HEAD_DIM, (h + 1) * HEAD_DIM)
        s_c = _nt(qh, kc[:, head]) + bias_c
        s_n = _nt(qh, knp[h]) + bias_n
        m = jnp.maximum(jnp.max(s_c, axis=1, keepdims=True), jnp.max(s_n, axis=1, keepdims=True))
        p_c = jnp.exp2(s_c - m)
        p_n = jnp.exp2(s_n - m)
        l = jnp.sum(p_c, axis=1, keepdims=True) + jnp.sum(p_n, axis=1, keepdims=True)
        o = _dot(p_c.astype(BF16), vc[:, head]) + _dot(p_n.astype(BF16), vnp[h])
        out_ref[0, h] = (o / l).astype(BF16)


def _sample_attn(q_hm, qi_hm, w, kic, kin, kc, vc, kn_hm, vn_hm):
    nb, _, tn, _ = q_hm.shape
    past = kic.shape[1]
    b4 = lambda a: pl.BlockSpec((1,) + a.shape[1:], lambda b: (b, 0, 0, 0))
    b3 = lambda a: pl.BlockSpec((1,) + a.shape[1:], lambda b: (b, 0, 0))
    return pl.pallas_call(
        functools.partial(_sample_attn_body, past=past, tn=tn),
        grid=(nb,),
        in_specs=[b4(q_hm), b4(qi_hm), b3(w), b3(kic), b3(kin), b3(kc), b3(vc), b4(kn_hm), b4(vn_hm)],
        out_specs=pl.BlockSpec((1, N_HEADS, tn, HEAD_DIM), lambda b: (b, 0, 0, 0)),
        out_shape=jax.ShapeDtypeStruct((nb, N_HEADS, tn, HEAD_DIM), BF16),
        scratch_shapes=[pltpu.VMEM((NEW_PAD, IDX_DIM), BF16),
                        pltpu.VMEM((N_HEADS, NEW_PAD, HEAD_DIM), BF16),
                        pltpu.VMEM((N_HEADS, NEW_PAD, HEAD_DIM), BF16)],
        compiler_params=_params(("parallel",)),
        name="sample_attn",
    )(q_hm, qi_hm, w, kic, kin, kc, vc, kn_hm, vn_hm)


def _cmpx(a, b):
    return jnp.maximum(a, b), jnp.minimum(a, b)


def _bitonic_merge_desc(v):
    n = len(v)
    v = list(v)
    j = n // 2
    while j >= 1:
        for i in range(n):
            l = i ^ j
            if l > i:
                v[i], v[l] = _cmpx(v[i], v[l])
        j //= 2
    return v


def _bitonic_sort_desc(v):
    n = len(v)
    v = list(v)
    k = 2
    while k <= n:
        j = k // 2
        while j >= 1:
            for i in range(n):
                l = i ^ j
                if l > i:
                    hi, lo = _cmpx(v[i], v[l])
                    v[i], v[l] = (hi, lo) if (i & k) == 0 else (lo, hi)
            j //= 2
        k *= 2
    return v


def _merge_top(a, b):
    n = len(a)
    return _bitonic_merge_desc([jnp.maximum(a[k], b[n - 1 - k]) for k in range(n)])


def _top16_desc(vals):
    groups = [_bitonic_sort_desc(vals[g:g + PEER_TOPK]) for g in range(0, len(vals), PEER_TOPK)]
    while len(groups) > 1:
        groups = [_merge_top(groups[g], groups[g + 1]) for g in range(0, len(groups), 2)]
    return groups[0]


def _mid_body(x_ref, att_ref, conv_ref, woa_ref, woc_ref, g_ref, wqt_ref, a1_ref, a2_ref,
              h_ref, xn_ref, qht_ref, st_ref):
    h = x_ref[...] + _dot(att_ref[...], woa_ref[...]) + _dot(conv_ref[...], woc_ref[...])
    h_ref[...] = h
    xn = _rms(h, g_ref[...]).astype(BF16)
    xn_ref[...] = xn
    qht = _nt(wqt_ref[...], xn).astype(BF16)
    qht_ref[...] = qht
    half = PEER_HEADS * PEER_HALF
    s1 = _dot(a1_ref[...], qht[:half])
    s2 = _dot(a2_ref[...], qht[half:])
    rows = lambda s: [s[r * PEER_HEADS:(r + 1) * PEER_HEADS, :] for r in range(PEER_KEYS)]
    v1 = _top16_desc(rows(s1))
    v2 = _top16_desc(rows(s2))
    ninf = jnp.full(v1[0].shape, NEG_INF, F32)
    top = [v1[0] + v2[b] for b in range(PEER_TOPK)]
    for a in range(1, PEER_TOPK):
        n_a = PEER_TOPK // (a + 1)
        top = _merge_top(top, [v1[a] + v2[b] if b < n_a else ninf for b in range(PEER_TOPK)])
    z = jnp.exp(top[0] - top[0])
    for k in range(1, PEER_TOPK):
        z = z + jnp.exp(top[k] - top[0])
    st_ref[0:8, :] = v1[0]
    st_ref[8:16, :] = v2[0]
    st_ref[16:24, :] = top[PEER_TOPK - 1]
    st_ref[24:32, :] = 1.0 / z


def _mid(x, att, conv, woa, woc, g, wqt, a1, a2, tb):
    t = x.shape[0]
    row = lambda w: pl.BlockSpec((tb, w), lambda i: (i, 0))
    col = lambda r: pl.BlockSpec((r, tb), lambda i: (0, i))
    full = lambda a: pl.BlockSpec(a.shape, lambda i: (0, 0))
    return pl.pallas_call(
        _mid_body,
        grid=(t // tb,),
        in_specs=[row(D_MODEL), row(ATT_WIDTH), row(CONV_CH), full(woa), full(woc), full(g),
                  full(wqt), full(a1), full(a2)],
        out_specs=[row(D_MODEL), row(D_MODEL), col(D_MODEL), col(32)],
        out_shape=[jax.ShapeDtypeStruct((t, D_MODEL), F32),
                   jax.ShapeDtypeStruct((t, D_MODEL), BF16),
                   jax.ShapeDtypeStruct((D_MODEL, t), BF16),
                   jax.ShapeDtypeStruct((32, t), F32)],
        compiler_params=_params(("parallel",)),
        name="mid",
    )(x, att, conv, woa, woc, g, wqt, a1, a2)


def _peer_body(xn_ref, qht_ref, st_ref, a1_ref, a2_ref, u_ref, vt_ref, out_ref,
               s1_ref, s2_ref, e1_ref, e2_ref, hid_ref, *, eb):
    e = pl.program_id(1)
    nsub = eb // PEER_KEYS
    half = PEER_HEADS * PEER_HALF

    @pl.when(e == 0)
    def _():
        qht = qht_ref[...]
        s1 = _dot(a1_ref[...], qht[:half])
        s2 = _dot(a2_ref[...], qht[half:])
        s1_ref[...] = s1
        s2_ref[...] = s2
        for h in range(PEER_HEADS):
            r = slice(h * PEER_KEYS, (h + 1) * PEER_KEYS)
            e1_ref[r, :] = jnp.exp(s1[r] - st_ref[h:h + 1, :])
            e2_ref[r, :] = jnp.exp(s2[r] - st_ref[8 + h:9 + h, :]) * st_ref[24 + h:25 + h, :] * 0.5
        out_ref[...] = jnp.zeros(out_ref.shape, F32)

    xn = xn_ref[...]
    for ii in range(nsub):
        i = e * nsub + ii
        a = _nt(u_ref[ii * PEER_KEYS:(ii + 1) * PEER_KEYS, :], xn)
        gate = None
        for h in range(PEER_HEADS):
            r = slice(h * PEER_KEYS, (h + 1) * PEER_KEYS)
            pair = s1_ref[pl.ds(h * PEER_KEYS + i, 1), :] + s2_ref[r, :]
            g = e1_ref[pl.ds(h * PEER_KEYS + i, 1), :] * e2_ref[r, :]
            g = jnp.where(pair >= st_ref[16 + h:17 + h, :], g, 0.0)
            gate = g if gate is None else gate + g
        hid = a * (1.0 + lax.erf(a * INV_SQRT2)) * gate
        hid_ref[ii * PEER_KEYS:(ii + 1) * PEER_KEYS, :] = hid.astype(BF16)
    out_ref[...] += _dot(vt_ref[...], hid_ref[...])


def _peer(xn, qht, st, a1, a2, u, vt, tb, eb):
    t = xn.shape[0]
    ne = u.shape[0] // eb
    full = lambda a: pl.BlockSpec(a.shape, lambda i, e: (0, 0))
    return pl.pallas_call(
        functools.partial(_peer_body, eb=eb),
        grid=(t // tb, ne),
        in_specs=[pl.BlockSpec((tb, D_MODEL), lambda i, e: (i, 0)),
                  pl.BlockSpec((D_MODEL, tb), lambda i, e: (0, i)),
                  pl.BlockSpec((32, tb), lambda i, e: (0, i)),
                  full(a1), full(a2),
                  pl.BlockSpec((eb, D_MODEL), lambda i, e: (e, 0)),
                  pl.BlockSpec((D_MODEL, eb), lambda i, e: (0, e))],
        out_specs=pl.BlockSpec((D_MODEL, tb), lambda i, e: (0, i)),
        out_shape=jax.ShapeDtypeStruct((D_MODEL, t), F32),
        scratch_shapes=[pltpu.VMEM((PEER_HEADS * PEER_KEYS, tb), F32)] * 4
                       + [pltpu.VMEM((eb, tb), BF16)],
        compiler_params=_params(("parallel", "arbitrary")),
        name="peer",
    )(xn, qht, st, a1, a2, u, vt)


def _final_body(h_ref, pt_ref, g_ref, y_ref):
    y_ref[...] = _rms(h_ref[...] + pt_ref[...].T, g_ref[...])


def _final(h, pt, g, tb):
    t = h.shape[0]
    return pl.pallas_call(
        _final_body,
        grid=(t // tb,),
        in_specs=[pl.BlockSpec((tb, D_MODEL), lambda i: (i, 0)),
                  pl.BlockSpec((D_MODEL, tb), lambda i: (0, i)),
                  pl.BlockSpec((1, D_MODEL), lambda i: (0, 0))],
        out_specs=pl.BlockSpec((tb, D_MODEL), lambda i: (i, 0)),
        out_shape=jax.ShapeDtypeStruct((t, D_MODEL), F32),
        compiler_params=_params(("parallel",)),
        name="final",
    )(h, pt, g)


def _pick(n, pref):
    b = min(n, pref)
    while n % b:
        b //= 2
    return b


def _token_stage(x, att, conv, p, tb_mid, tb_peer):
    h, xn, qht, st = _mid(x, att, conv, p["woa"], p["woc"], p["ffn_g"], p["wqt"], p["a1s"], p["a2s"], tb_mid)
    pt = _peer(xn, qht, st, p["a1d"], p["a2d"], p["u"], p["vt"], tb_peer, 2048)
    return _final(h, pt, p["final_g"], tb_mid)


def kernel(x_prompt, x_sample, cache_k, cache_v, cache_kidx, state_conv, attn_norm_g, w_in, conv_w, conv_b,
           conv_ln_g, conv_ln_b, w_out, ffn_norm_g, peer_wq, peer_subkeys, peer_u, peer_v, final_norm_g):
    assert w_in.shape[0] == 1, "single layer"
    nbp, s, _ = x_prompt.shape
    nb, tn, _ = x_sample.shape
    assert nbp == 1
    past = cache_k.shape[2]

    wi = w_in[0]
    o_qi = 3 * ATT_WIDTH
    o_ki = o_qi + N_IDX_HEADS * IDX_DIM
    o_wi = o_ki + IDX_DIM
    o_glu = o_wi + N_IDX_HEADS
    wqkv = wi[:, :o_qi].astype(BF16)
    widx = jnp.concatenate([wi[:, o_qi:o_glu], jnp.zeros((D_MODEL, 128 - IDX_DIM - N_IDX_HEADS), F32)],
                           axis=1).astype(BF16)
    wglu = wi[:, o_glu:].astype(BF16)
    g_attn = attn_norm_g[0][None, :]
    cw = jnp.concatenate([conv_w[0], jnp.zeros((CONV_TAIL - CONV_WIDTH, CONV_CH), F32)], axis=0)
    cb, cg, cbeta = conv_b[0][None, :], conv_ln_g[0][None, :], conv_ln_b[0][None, :]
    eye = jnp.eye(PEER_HEADS, dtype=F32)
    sk = peer_subkeys[0]
    a_s = [jnp.einsum("id,hg->ihgd", sk[c], eye).reshape(PEER_KEYS * PEER_HEADS, PEER_HEADS * PEER_HALF)
           .astype(BF16) for c in range(2)]
    a_d = [jnp.einsum("id,hg->higd", sk[c], eye).reshape(PEER_KEYS * PEER_HEADS, PEER_HEADS * PEER_HALF)
           .astype(BF16) for c in range(2)]
    wq = peer_wq[0].reshape(D_MODEL, PEER_HEADS, 2, PEER_HALF).transpose(2, 1, 3, 0)
    p = {
        "woa": w_out[0][:ATT_WIDTH].astype(BF16), "woc": w_out[0][ATT_WIDTH:].astype(BF16),
        "ffn_g": ffn_norm_g[0][None, :], "final_g": final_norm_g[None, :],
        "wqt": wq.reshape(D_MODEL, D_MODEL).astype(BF16),
        "a1s": a_s[0], "a2s": a_s[1], "a1d": a_d[0], "a2d": a_d[1],
        "u": peer_u[0].astype(BF16), "vt": peer_v[0].T.astype(BF16),
    }
    w_scale = (N_IDX_HEADS ** -0.5) * (IDX_DIM ** -0.5)

    xp = x_prompt[0]
    tb = _pick(s, 256)
    k, v, kw, glu, q_hm, k_hm, vt_hm, qi_hm = _in_proj(xp, g_attn, wqkv, widx, wglu, tb)
    conv_p, tail_p = _conv(glu, jnp.zeros((1, CONV_TAIL, CONV_CH), F32), cw, cb, cg, cbeta, 1, tb)
    ki = kw[:, :IDX_DIM]
    ki_b = ki.astype(BF16)
    w_t = (kw[:, IDX_DIM:IDX_DIM + N_IDX_HEADS] * w_scale).T
    tq = _pick(s, 256)
    sel = _idx_sel(qi_hm, w_t, ki_b, tq)
    att_p = _attn(q_hm, k_hm, vt_hm, sel, tq, _pick(s, 1024))
    y_p = _token_stage(xp, att_p, conv_p, p, tb, _pick(s, 512))

    xs = x_sample.reshape(nb * tn, D_MODEL)
    ts = nb * tn
    tbs = _pick(ts, 256)
    ks, vs, kws, glus, qs_hm, _, _, qis_hm = _in_proj(xs, g_attn, wqkv, widx, wglu, tbs)
    pad = jnp.zeros((nb, CONV_TAIL - (CONV_WIDTH - 1), CONV_CH), F32)
    conv_s, tail_s = _conv(glus, jnp.concatenate([pad, state_conv[0]], axis=1), cw, cb, cg, cbeta, nb, tn)
    hm4 = lambda a, nh, dt=BF16: a.reshape(nb, tn, nh, a.shape[1] // nh).transpose(0, 2, 1, 3).astype(dt)
    per_stream = lambda a: a.reshape(a.shape[0], nb, tn, a.shape[2]).transpose(1, 0, 2, 3)
    kis = kws[:, :IDX_DIM]
    att_s = _sample_attn(
        per_stream(qs_hm), per_stream(qis_hm),
        (kws[:, IDX_DIM:IDX_DIM + N_IDX_HEADS] * w_scale).reshape(nb, tn, N_IDX_HEADS),
        cache_kidx[0], kis.reshape(nb, tn, IDX_DIM),
        cache_k[0].reshape(nb, past, ATT_WIDTH), cache_v[0].reshape(nb, past, ATT_WIDTH),
        hm4(ks, N_HEADS), hm4(vs, N_HEADS))
    att_s = att_s.transpose(0, 2, 1, 3).reshape(ts, ATT_WIDTH)
    y_s = _token_stage(xs, att_s, conv_s, p, tbs, tbs)

    hd = (N_HEADS, HEAD_DIM)
    keep = CONV_TAIL - (CONV_WIDTH - 1)
    return (y_p[None], y_s.reshape(nb, tn, D_MODEL),
            k.reshape(1, 1, s, *hd), v.reshape(1, 1, s, *hd), ki[None, None], tail_p[None, :, keep:],
            ks.reshape(1, nb, tn, *hd), vs.reshape(1, nb, tn, *hd), kis.reshape(1, nb, tn, IDX_DIM),
            tail_s[None, :, keep:])
```

```python
import functools

import jax
import jax.numpy as jnp
from jax import lax
from jax.experimental import pallas as pl
from jax.experimental.pallas import tpu as pltpu

F32 = jnp.float32
BF16 = jnp.bfloat16
I32 = jnp.int32

D_MODEL = 1024
N_HEADS = 8
HEAD_DIM = 64
ATT_WIDTH = N_HEADS * HEAD_DIM
PV_ROWS = HEAD_DIM + 16
N_IDX_HEADS = 4
IDX_DIM = 64
TOPK = 256
CHUNK_SHIFT = 6
CONV_CH = 512
CONV_WIDTH = 31
CONV_TAIL = 32
PEER_HEADS = 8
PEER_KEYS = 128
PEER_HALF = 64
PEER_TOPK = 16
EPS = 1e-6
INT_MIN = -2147483648
ORDER_MASK = 0x7FFFFFFF
INV_SQRT2 = 0.7071067811865476
LOG2E = 1.4426950408889634
NEG_INF = float("-inf")
M_INIT = -1e30

VMEM_LIMIT = 56 * 1024 * 1024


def _params(sem, vmem=VMEM_LIMIT):
    return pltpu.CompilerParams(dimension_semantics=sem, vmem_limit_bytes=vmem)


def _nt(a, b):
    return lax.dot_general(a, b, (((1,), (1,)), ((), ())), preferred_element_type=F32)


def _dot(a, b):
    return jnp.dot(a, b, preferred_element_type=F32)


def _rms(x, g):
    return x * lax.rsqrt(jnp.mean(x * x, axis=-1, keepdims=True) + EPS) * g


def _order_key(x):
    b = pltpu.bitcast(x, I32)
    return b ^ ((b >> 31) & ORDER_MASK)


def _in_proj_body(x_ref, g_ref, wqkv_ref, widx_ref, wglu_ref,
                  k_ref, v_ref, kw_ref, glu_ref, qh_ref, kh_ref, vth_ref, qih_ref):
    xn = _rms(x_ref[...], g_ref[...]).astype(BF16)
    qkv = _dot(xn, wqkv_ref[...])
    q = qkv[:, :ATT_WIDTH] * (HEAD_DIM ** -0.5 * LOG2E)
    k = qkv[:, ATT_WIDTH:2 * ATT_WIDTH]
    v = qkv[:, 2 * ATT_WIDTH:]
    k_ref[...] = k
    v_ref[...] = v
    for h in range(N_HEADS):
        head = slice(h * HEAD_DIM, (h + 1) * HEAD_DIM)
        qh_ref[h] = q[:, head].astype(BF16)
        kh_ref[h] = k[:, head].astype(BF16)
    vth_ref[:, 0:HEAD_DIM, :] = v.T.reshape(N_HEADS, HEAD_DIM, v.shape[0]).astype(BF16)
    vth_ref[:, HEAD_DIM:, :] = jnp.ones((N_HEADS, PV_ROWS - HEAD_DIM, v.shape[0]), BF16)
    ix = _dot(xn, widx_ref[...])
    for h in range(N_IDX_HEADS):
        qih_ref[h] = ix[:, h * IDX_DIM:(h + 1) * IDX_DIM].astype(BF16)
    kw_ref[...] = ix[:, N_IDX_HEADS * IDX_DIM:]
    glu_ref[...] = _dot(xn, wglu_ref[...])


def _in_proj(x, g, wqkv, widx, wglu, tb):
    t = x.shape[0]
    row = lambda w: pl.BlockSpec((tb, w), lambda i: (i, 0))
    full = lambda a: pl.BlockSpec(a.shape, lambda i: (0, 0))
    heads = lambda n: pl.BlockSpec((n, tb, HEAD_DIM), lambda i: (0, i, 0))
    return pl.pallas_call(
        _in_proj_body,
        grid=(t // tb,),
        in_specs=[row(D_MODEL), full(g), full(wqkv), full(widx), full(wglu)],
        out_specs=[row(ATT_WIDTH), row(ATT_WIDTH), row(128), row(2 * CONV_CH),
                   heads(N_HEADS), heads(N_HEADS),
                   pl.BlockSpec((N_HEADS, PV_ROWS, tb), lambda i: (0, 0, i)), heads(N_IDX_HEADS)],
        out_shape=[jax.ShapeDtypeStruct((t, ATT_WIDTH), F32),
                   jax.ShapeDtypeStruct((t, ATT_WIDTH), F32),
                   jax.ShapeDtypeStruct((t, 128), F32),
                   jax.ShapeDtypeStruct((t, 2 * CONV_CH), F32),
                   jax.ShapeDtypeStruct((N_HEADS, t, HEAD_DIM), BF16),
                   jax.ShapeDtypeStruct((N_HEADS, t, HEAD_DIM), BF16),
                   jax.ShapeDtypeStruct((N_HEADS, PV_ROWS, t), BF16),
                   jax.ShapeDtypeStruct((N_IDX_HEADS, t, IDX_DIM), BF16)],
        compiler_params=_params(("parallel",)),
        name="in_proj",
    )(x, g, wqkv, widx, wglu)


def _conv_body(glu_ref, tail0_ref, w_ref, b_ref, g_ref, beta_ref, out_ref, tail_ref, ubuf, shifted, *, tb):
    j = pl.program_id(1)

    @pl.when(j == 0)
    def _():
        ubuf[0:CONV_TAIL, :] = tail0_ref[0]

    @pl.when(j > 0)
    def _():
        ubuf[0:CONV_TAIL, :] = ubuf[tb:tb + CONV_TAIL, :]

    glu = glu_ref[...]
    ubuf[CONV_TAIL:CONV_TAIL + tb, :] = glu[:, :CONV_CH] * jax.nn.sigmoid(glu[:, CONV_CH:])
    off = CONV_TAIL - (CONV_WIDTH - 1)
    acc = None
    for res in range(8):
        taps = [t for t in range(CONV_WIDTH) if (off + t) % 8 == res]
        lo, hi = (off + taps[0]) // 8, (off + taps[-1]) // 8
        n = 8 * (hi - lo) + tb
        shifted[0:n, :] = ubuf[res + 8 * lo:res + 8 * lo + n, :]
        for t in taps:
            a = 8 * ((off + t) // 8 - lo)
            term = shifted[a:a + tb, :] * w_ref[t:t + 1, :]
            acc = term if acc is None else acc + term
    y = acc + b_ref[...]
    mu = jnp.mean(y, axis=-1, keepdims=True)
    d = y - mu
    var = jnp.mean(d * d, axis=-1, keepdims=True)
    z = d * lax.rsqrt(var + EPS) * g_ref[...] + beta_ref[...]
    out_ref[...] = (z * jax.nn.sigmoid(z)).astype(BF16)
    tail_ref[0] = ubuf[tb:tb + CONV_TAIL, :]


def _conv(glu, tail0, w, b, g, beta, nseq, tb):
    t = glu.shape[0]
    nblk = t // (nseq * tb)
    full = lambda a: pl.BlockSpec(a.shape, lambda s, j: (0, 0))
    return pl.pallas_call(
        functools.partial(_conv_body, tb=tb),
        grid=(nseq, nblk),
        in_specs=[pl.BlockSpec((tb, 2 * CONV_CH), lambda s, j: (s * nblk + j, 0)),
                  pl.BlockSpec((1, CONV_TAIL, CONV_CH), lambda s, j: (s, 0, 0)),
                  full(w), full(b), full(g), full(beta)],
        out_specs=[pl.BlockSpec((tb, CONV_CH), lambda s, j: (s * nblk + j, 0)),
                   pl.BlockSpec((1, CONV_TAIL, CONV_CH), lambda s, j: (s, 0, 0))],
        out_shape=[jax.ShapeDtypeStruct((t, CONV_CH), BF16),
                   jax.ShapeDtypeStruct((nseq, CONV_TAIL, CONV_CH), F32)],
        scratch_shapes=[pltpu.VMEM((tb + CONV_TAIL, CONV_CH), F32)] * 2,
        compiler_params=_params(("parallel", "arbitrary")),
        name="conv",
    )(glu, tail0, w, b, g, beta)


def _idx_scores_t(ki_tile, qi_ref, w):
    acc = None
    for h in range(N_IDX_HEADS):
        term = w[h:h + 1, :] * jnp.maximum(_nt(ki_tile, qi_ref[h]), 0.0)
        acc = term if acc is None else acc + term
    return acc


def _chunk_limit(q0, tq):
    qpos = q0 + lax.broadcasted_iota(I32, (1, tq), 1)
    return ((qpos >> CHUNK_SHIFT) + 1) << CHUNK_SHIFT


def _bit_planes(words):
    x = list(words)
    mask, j = 0x0000FFFF, 16
    while j:
        k = 0
        while k < 32:
            t = (x[k] ^ lax.shift_right_logical(x[k + j], jnp.int32(j))) & mask
            x[k] = x[k] ^ t
            x[k + j] = x[k + j] ^ (t << j)
            k = (k + j + 1) & ~j
        j >>= 1
        mask ^= (mask << j) & 0xFFFFFFFF
    return x


def _sublane_prefix(x):
    sub = lax.broadcasted_iota(I32, x.shape, 0)
    for sh in (1, 2, 4):
        x = x + jnp.where(sub >= sh, pltpu.roll(x, sh, axis=0), 0)
    return x


def _sel_body(qi_ref, w_ref, ki_ref, sel_ref, planes_ref, cand_ref, gt_ref, *, tq, grp):
    i = pl.program_id(0)
    nt = i + 1
    ng = (nt + grp - 1) // grp
    w = w_ref[...]
    limit = _chunk_limit(i * tq, tq)
    rows = lax.broadcasted_iota(I32, (tq, tq), 0)
    assert tq == 8 * 32

    def fill(t, diagonal):
        r0 = pl.multiple_of(t * tq, tq)
        b = pltpu.bitcast(_idx_scores_t(ki_ref[pl.ds(r0, tq), :], qi_ref, w), I32)
        u = b ^ ((b >> 31) | INT_MIN)
        if diagonal:
            u = jnp.where(rows + r0 < limit, u, 0)
        planes = _bit_planes([u[8 * k:8 * k + 8, :] for k in range(32)])
        at = pl.ds(pl.multiple_of(t * 8, 8), 8)
        for p in range(32):
            planes_ref[p, at, :] = planes[p]
        cand_ref[at, :] = jnp.full((8, tq), -1, I32)
        gt_ref[at, :] = jnp.zeros((8, tq), I32)

    def fill_full(t, c):
        fill(t, False)
        return c

    lax.fori_loop(0, nt - 1, fill_full, 0)
    fill(nt - 1, True)

    def pad(t, c):
        at = pl.ds(pl.multiple_of(t * 8, 8), 8)
        planes_ref[:, at, :] = jnp.zeros((32, 8, tq), I32)
        cand_ref[at, :] = jnp.zeros((8, tq), I32)
        gt_ref[at, :] = jnp.zeros((8, tq), I32)
        return c

    lax.fori_loop(nt, ng * grp, pad, 0)

    def sweep(p_prev, flip, p_next):
        def group(g, acc):
            at = pl.ds(pl.multiple_of(g * (grp * 8), grp * 8), grp * 8)
            c = cand_ref[at, :]
            if p_prev is not None:
                prev = planes_ref[p_prev, at, :]
                gt_ref[at, :] = gt_ref[at, :] | (c & prev & flip)
                c = c & (prev ^ flip)
                cand_ref[at, :] = c
            if p_next is not None:
                hit = lax.population_count(c & planes_ref[p_next, at, :])
                acc = acc + jnp.sum(hit.reshape(grp, 8, tq), axis=0)
            return acc
        acc = lax.fori_loop(0, ng, group, jnp.zeros((8, tq), I32))
        return jnp.sum(acc, axis=0, keepdims=True)

    def decide(p, cnt, kth_u, need):
        one = cnt >= need
        kth_u = jnp.where(one, kth_u | lax.shift_right_logical(jnp.int32(INT_MIN), p), kth_u)
        return kth_u, jnp.where(one, need, need - cnt), jnp.where(one, 0, -1)

    zero = jnp.zeros((1, tq), I32)
    kth_u, need, flip = decide(jnp.int32(0), sweep(None, None, 0), zero, jnp.full((1, tq), TOPK, I32))

    def bit_step(p, carry):
        kth_u, need, flip = carry
        return decide(p, sweep(p - 1, flip, p), kth_u, need)

    kth_u, need, flip = lax.fori_loop(1, 32, bit_step, (kth_u, need, flip))
    sweep(31, flip, None)
    need = jnp.where(kth_u == 0, 0, need)

    def tie_count(t):
        c = cand_ref[pl.ds(pl.multiple_of(t * 8, 8), 8), :]
        return c, jnp.sum(lax.population_count(c), axis=0, keepdims=True)

    def find(t, carry):
        seen, part, words = carry
        c, n = tie_count(t)
        hit = (seen < need) & (seen + n > need)
        return seen + n, jnp.where(hit, need - seen, part), jnp.where(hit, c, words)

    _, part, words = lax.fori_loop(0, nt, find, (zero, zero, jnp.zeros((8, tq), I32)))
    keep = jnp.zeros((8, tq), I32)
    before = zero
    for k in range(32):
        bit = lax.shift_right_logical(words, jnp.int32(31 - k)) & 1
        rank = before + _sublane_prefix(bit)
        keep = keep | jnp.where((bit == 1) & (rank <= part), INT_MIN if k == 0 else 1 << (31 - k), 0)
        before = rank[7:8, :]

    def emit(t, seen):
        c, n = tie_count(t)
        at = pl.ds(pl.multiple_of(t * 8, 8), 8)
        whole = seen + n <= need
        partial = (seen < need) & jnp.logical_not(whole)
        sel_ref[0, at, :] = gt_ref[at, :] | jnp.where(whole, c, jnp.where(partial, keep, 0))
        return seen + n

    lax.fori_loop(0, nt, emit, zero)

    def clear(t, c):
        sel_ref[0, pl.ds(pl.multiple_of(t * 8, 8), 8), :] = jnp.zeros((8, tq), I32)
        return c

    lax.fori_loop(nt, sel_ref.shape[1] // 8, clear, 0)


def _idx_sel(qi_hm, w_t, ki, tq):
    s = ki.shape[0]
    nrow = s // tq * 8
    return pl.pallas_call(
        functools.partial(_sel_body, tq=tq, grp=_pick(s // tq, 8)),
        grid=(s // tq,),
        in_specs=[pl.BlockSpec((N_IDX_HEADS, tq, IDX_DIM), lambda i: (0, i, 0)),
                  pl.BlockSpec((N_IDX_HEADS, tq), lambda i: (0, i)),
                  pl.BlockSpec((s, IDX_DIM), lambda i: (0, 0))],
        out_specs=pl.BlockSpec((1, nrow, tq), lambda i: (i, 0, 0)),
        out_shape=jax.ShapeDtypeStruct((s // tq, nrow, tq), I32),
        scratch_shapes=[pltpu.VMEM((32, nrow, tq), I32), pltpu.VMEM((nrow, tq), I32),
                        pltpu.VMEM((nrow, tq), I32)],
        compiler_params=_params(("arbitrary",)),
        name="idx_sel",
    )(qi_hm, w_t, ki)


def _attn_body(qb_ref, kb_ref, q_ref, k_ref, vt_ref, sel_ref,
               out_ref, m_ref, acc_ref, s_ref, p_ref, bias_ref, *, tq, tk):
    step = pl.program_id(0)
    i = qb_ref[step]
    j = kb_ref[step]

    @pl.when(j == 0)
    def _():
        m_ref[...] = jnp.full(m_ref.shape, M_INIT, F32)
        acc_ref[...] = jnp.zeros(acc_ref.shape, F32)

    for tile in range(tk // tq):
        words = sel_ref[0, tile * 8:(tile + 1) * 8, :]
        for k in range(32):
            bias_ref[pl.ds(tile * tq + 8 * k, 8), :] = jnp.where((words << k) < 0, 0.0, NEG_INF)
    bias = bias_ref[...]
    tops = []
    for h in range(N_HEADS):
        s = _nt(k_ref[h], q_ref[h]) + bias
        s_ref[h] = s
        tops.append(jnp.max(s, axis=0, keepdims=True))
    m_old = m_ref[...]
    m_new = jnp.maximum(m_old, jnp.concatenate(tops, axis=0))
    alpha = jnp.exp2(m_old - m_new)
    for h in range(N_HEADS):
        p_ref[h] = jnp.exp2(s_ref[h] - m_new[h:h + 1, :]).astype(BF16)
    m_ref[...] = m_new
    for h in range(N_HEADS):
        acc_ref[h] = alpha[h:h + 1, :] * acc_ref[h] + _dot(vt_ref[h], p_ref[h])

    @pl.when(j == ((i + 1) * tq - 1) // tk)
    def _():
        o = jnp.concatenate([acc_ref[h, 0:HEAD_DIM, :] / acc_ref[h, HEAD_DIM:HEAD_DIM + 1, :]
                             for h in range(N_HEADS)], axis=0)
        out_ref[...] = o.T.astype(BF16)


def _attn(q_hm, k_hm, vt_hm, sel, tq, tk):
    s = q_hm.shape[1]
    assert tk % tq == 0
    qb, kb = [], []
    for i in range(s // tq):
        for j in range(((i + 1) * tq - 1) // tk + 1):
            qb.append(i)
            kb.append(j)
    qb = jnp.asarray(qb, I32)
    kb = jnp.asarray(kb, I32)
    grid_spec = pltpu.PrefetchScalarGridSpec(
        num_scalar_prefetch=2,
        grid=(int(qb.shape[0]),),
        in_specs=[pl.BlockSpec((N_HEADS, tq, HEAD_DIM), lambda t, qb, kb: (0, qb[t], 0)),
                  pl.BlockSpec((N_HEADS, tk, HEAD_DIM), lambda t, qb, kb: (0, kb[t], 0)),
                  pl.BlockSpec((N_HEADS, PV_ROWS, tk), lambda t, qb, kb: (0, 0, kb[t])),
                  pl.BlockSpec((1, tk // tq * 8, tq), lambda t, qb, kb: (qb[t], kb[t], 0))],
        out_specs=pl.BlockSpec((tq, ATT_WIDTH), lambda t, qb, kb: (qb[t], 0)),
        scratch_shapes=[pltpu.VMEM((N_HEADS, tq), F32),
                        pltpu.VMEM((N_HEADS, PV_ROWS, tq), F32),
                        pltpu.VMEM((N_HEADS, tk, tq), F32),
                        pltpu.VMEM((N_HEADS, tk, tq), BF16),
                        pltpu.VMEM((tk, tq), F32)],
    )
    return pl.pallas_call(
        functools.partial(_attn_body, tq=tq, tk=tk),
        grid_spec=grid_spec,
        out_shape=jax.ShapeDtypeStruct((s, ATT_WIDTH), BF16),
        compiler_params=_params(("arbitrary",)),
        name="attn",
    )(qb, kb, q_hm, k_hm, vt_hm, sel)


NEW_PAD = 128


def _sample_attn_body(q_ref, qi_ref, w_ref, kic_ref, kin_ref, kc_ref, vc_ref, kn_ref, vn_ref,
                      out_ref, kinp, knp, vnp, *, past, tn):
    kinp[...] = jnp.zeros(kinp.shape, BF16)
    kinp[0:tn, :] = kin_ref[0].astype(BF16)
    knp[...] = jnp.zeros(knp.shape, BF16)
    vnp[...] = jnp.zeros(vnp.shape, BF16)
    for h in range(N_HEADS):
        knp[h, 0:tn, :] = kn_ref[0, h]
        vnp[h, 0:tn, :] = vn_ref[0, h]

    w = w_ref[0]
    kic = kic_ref[0].astype(BF16)
    kin = kinp[...]
    sc = None
    sn = None
    for h in range(N_IDX_HEADS):
        qih = qi_ref[0, h]
        wc = w[:, h:h + 1]
        tc = wc * jnp.maximum(_nt(qih, kic), 0.0)
        tnw = wc * jnp.maximum(_nt(qih, kin), 0.0)
        sc = tc if sc is None else sc + tc
        sn = tnw if sn is None else sn + tnw
    new_ok = lax.broadcasted_iota(I32, (tn, NEW_PAD), 1) < tn
    keyc = _order_key(sc)
    keyn = jnp.where(new_ok, _order_key(sn), INT_MIN)

    def count(pc, pn):
        return (jnp.sum(jnp.where(pc, 1, 0), axis=1, keepdims=True)
                + jnp.sum(jnp.where(pn, 1, 0), axis=1, keepdims=True))

    def bit_step(p, u):
        cand_u = u | (jnp.int32(1) << (31 - p))
        cand = cand_u ^ INT_MIN
        return jnp.where(count(keyc >= cand, keyn >= cand) >= TOPK, cand_u, u)

    kth = lax.fori_loop(0, 32, bit_step, jnp.zeros((tn, 1), I32)) ^ INT_MIN
    need = (TOPK - count(keyc > kth, keyn > kth)).astype(F32)

    cw = 256
    tri = jnp.where(lax.broadcasted_iota(I32, (cw, cw), 0) <= lax.broadcasted_iota(I32, (cw, cw), 1),
                    1.0, 0.0).astype(BF16)
    seen = jnp.zeros((tn, 1), F32)
    bias_c = []
    for c in range(past // cw):
        kc_ = keyc[:, c * cw:(c + 1) * cw]
        eq = kc_ == kth
        pre = _dot(jnp.where(eq, 1.0, 0.0).astype(BF16), tri)
        sel = (kc_ > kth) | (eq & (pre + seen <= need))
        bias_c.append(jnp.where(sel, 0.0, NEG_INF))
        seen = seen + pre[:, cw - 1:cw]
    bias_c = jnp.concatenate(bias_c, axis=1)
    eqn = keyn == kth
    pren = _dot(jnp.where(eqn, 1.0, 0.0).astype(BF16), tri[:NEW_PAD, :NEW_PAD])
    bias_n = jnp.where(new_ok & ((keyn > kth) | (eqn & (pren + seen <= need))), 0.0, NEG_INF)

    kc = kc_ref[0].astype(BF16)
    vc = vc_ref[0].astype(BF16)
    for h in range(N_HEADS):
        qh = q_ref[0, h]
        head = slice(h * HEAD_DIM, (h + 1) * HEAD_DIM)
        s_c = _nt(qh, kc[:, head]) + bias_c
        s_n = _nt(qh, knp[h]) + bias_n
        m = jnp.maximum(jnp.max(s_c, axis=1, keepdims=True), jnp.max(s_n, axis=1, keepdims=True))
        p_c = jnp.exp2(s_c - m)
        p_n = jnp.exp2(s_n - m)
        l = jnp.sum(p_c, axis=1, keepdims=True) + jnp.sum(p_n, axis=1, keepdims=True)
        o = _dot(p_c.astype(BF16), vc[:, head]) + _dot(p_n.astype(BF16), vnp[h])
        out_ref[0, h] = (o / l).astype(BF16)


def _sample_attn(q_hm, qi_hm, w, kic, kin, kc, vc, kn_hm, vn_hm):
    nb, _, tn, _ = q_hm.shape
    past = kic.shape[1]
    b4 = lambda a: pl.BlockSpec((1,) + a.shape[1:], lambda b: (b, 0, 0, 0))
    b3 = lambda a: pl.BlockSpec((1,) + a.shape[1:], lambda b: (b, 0, 0))
    return pl.pallas_call(
        functools.partial(_sample_attn_body, past=past, tn=tn),
        grid=(nb,),
        in_specs=[b4(q_hm), b4(qi_hm), b3(w), b3(kic), b3(kin), b3(kc), b3(vc), b4(kn_hm), b4(vn_hm)],
        out_specs=pl.BlockSpec((1, N_HEADS, tn, HEAD_DIM), lambda b: (b, 0, 0, 0)),
        out_shape=jax.ShapeDtypeStruct((nb, N_HEADS, tn, HEAD_DIM), BF16),
        scratch_shapes=[pltpu.VMEM((NEW_PAD, IDX_DIM), BF16),
                        pltpu.VMEM((N_HEADS, NEW_PAD, HEAD_DIM), BF16),
                        pltpu.VMEM((N_HEADS, NEW_PAD, HEAD_DIM), BF16)],
        compiler_params=_params(("parallel",)),
        name="sample_attn",
    )(q_hm, qi_hm, w, kic, kin, kc, vc, kn_hm, vn_hm)


def _cmpx(a, b):
    return jnp.maximum(a, b), jnp.minimum(a, b)


def _bitonic_merge_desc(v):
    n = len(v)
    v = list(v)
    j = n // 2
    while j >= 1:
        for i in range(n):
            l = i ^ j
            if l > i:
                v[i], v[l] = _cmpx(v[i], v[l])
        j //= 2
    return v


def _bitonic_sort_desc(v):
    n = len(v)
    v = list(v)
    k = 2
    while k <= n:
        j = k // 2
        while j >= 1:
            for i in range(n):
                l = i ^ j
                if l > i:
                    hi, lo = _cmpx(v[i], v[l])
                    v[i], v[l] = (hi, lo) if (i & k) == 0 else (lo, hi)
            j //= 2
        k *= 2
    return v


def _merge_top(a, b):
    n = len(a)
    return _bitonic_merge_desc([jnp.maximum(a[k], b[n - 1 - k]) for k in range(n)])


def _top16_desc(vals):
    groups = [_bitonic_sort_desc(vals[g:g + PEER_TOPK]) for g in range(0, len(vals), PEER_TOPK)]
    while len(groups) > 1:
        groups = [_merge_top(groups[g], groups[g + 1]) for g in range(0, len(groups), 2)]
    return groups[0]


def _mid_body(x_ref, att_ref, conv_ref, woa_ref, woc_ref, g_ref, wqt_ref, a1_ref, a2_ref,
              h_ref, xn_ref, qht_ref, st_ref):
    h = x_ref[...] + _dot(att_ref[...], woa_ref[...]) + _dot(conv_ref[...], woc_ref[...])
    h_ref[...] = h
    xn = _rms(h, g_ref[...]).astype(BF16)
    xn_ref[...] = xn
    qht = _nt(wqt_ref[...], xn).astype(BF16)
    qht_ref[...] = qht
    half = PEER_HEADS * PEER_HALF
    s1 = _dot(a1_ref[...], qht[:half])
    s2 = _dot(a2_ref[...], qht[half:])
    rows = lambda s: [s[r * PEER_HEADS:(r + 1) * PEER_HEADS, :] for r in range(PEER_KEYS)]
    v1 = _top16_desc(rows(s1))
    v2 = _top16_desc(rows(s2))
    ninf = jnp.full(v1[0].shape, NEG_INF, F32)
    top = [v1[0] + v2[b] for b in range(PEER_TOPK)]
    for a in range(1, PEER_TOPK):
        n_a = PEER_TOPK // (a + 1)
        top = _merge_top(top, [v1[a] + v2[b] if b < n_a else ninf for b in range(PEER_TOPK)])
    z = jnp.exp(top[0] - top[0])
    for k in range(1, PEER_TOPK):
        z = z + jnp.exp(top[k] - top[0])
    st_ref[0:8, :] = v1[0]
    st_ref[8:16, :] = v2[0]
    st_ref[16:24, :] = top[PEER_TOPK - 1]
    st_ref[24:32, :] = 1.0 / z


def _mid(x, att, conv, woa, woc, g, wqt, a1, a2, tb):
    t = x.shape[0]
    row = lambda w: pl.BlockSpec((tb, w), lambda i: (i, 0))
    col = lambda r: pl.BlockSpec((r, tb), lambda i: (0, i))
    full = lambda a: pl.BlockSpec(a.shape, lambda i: (0, 0))
    return pl.pallas_call(
        _mid_body,
        grid=(t // tb,),
        in_specs=[row(D_MODEL), row(ATT_WIDTH), row(CONV_CH), full(woa), full(woc), full(g),
                  full(wqt), full(a1), full(a2)],
        out_specs=[row(D_MODEL), row(D_MODEL), col(D_MODEL), col(32)],
        out_shape=[jax.ShapeDtypeStruct((t, D_MODEL), F32),
                   jax.ShapeDtypeStruct((t, D_MODEL), BF16),
                   jax.ShapeDtypeStruct((D_MODEL, t), BF16),
                   jax.ShapeDtypeStruct((32, t), F32)],
        compiler_params=_params(("parallel",)),
        name="mid",
    )(x, att, conv, woa, woc, g, wqt, a1, a2)


def _peer_body(xn_ref, qht_ref, st_ref, a1_ref, a2_ref, u_ref, vt_ref, out_ref,
               s1_ref, s2_ref, e1_ref, e2_ref, hid_ref, *, eb):
    e = pl.program_id(1)
    nsub = eb // PEER_KEYS
    half = PEER_HEADS * PEER_HALF

    @pl.when(e == 0)
    def _():
        qht = qht_ref[...]
        s1 = _dot(a1_ref[...], qht[:half])
        s2 = _dot(a2_ref[...], qht[half:])
        s1_ref[...] = s1
        s2_ref[...] = s2
        for h in range(PEER_HEADS):
            r = slice(h * PEER_KEYS, (h + 1) * PEER_KEYS)
            e1_ref[r, :] = jnp.exp(s1[r] - st_ref[h:h + 1, :])
            e2_ref[r, :] = jnp.exp(s2[r] - st_ref[8 + h:9 + h, :]) * st_ref[24 + h:25 + h, :] * 0.5
        out_ref[...] = jnp.zeros(out_ref.shape, F32)

    xn = xn_ref[...]
    for ii in range(nsub):
        i = e * nsub + ii
        a = _nt(u_ref[ii * PEER_KEYS:(ii + 1) * PEER_KEYS, :], xn)
        gate = None
        for h in range(PEER_HEADS):
            r = slice(h * PEER_KEYS, (h + 1) * PEER_KEYS)
            pair = s1_ref[pl.ds(h * PEER_KEYS + i, 1), :] + s2_ref[r, :]
            g = e1_ref[pl.ds(h * PEER_KEYS + i, 1), :] * e2_ref[r, :]
            g = jnp.where(pair >= st_ref[16 + h:17 + h, :], g, 0.0)
            gate = g if gate is None else gate + g
        hid = a * (1.0 + lax.erf(a * INV_SQRT2)) * gate
        hid_ref[ii * PEER_KEYS:(ii + 1) * PEER_KEYS, :] = hid.astype(BF16)
    out_ref[...] += _dot(vt_ref[...], hid_ref[...])


def _peer(xn, qht, st, a1, a2, u, vt, tb, eb):
    t = xn.shape[0]
    ne = u.shape[0] // eb
    full = lambda a: pl.BlockSpec(a.shape, lambda i, e: (0, 0))
    return pl.pallas_call(
        functools.partial(_peer_body, eb=eb),
        grid=(t // tb, ne),
        in_specs=[pl.BlockSpec((tb, D_MODEL), lambda i, e: (i, 0)),
                  pl.BlockSpec((D_MODEL, tb), lambda i, e: (0, i)),
                  pl.BlockSpec((32, tb), lambda i, e: (0, i)),
                  full(a1), full(a2),
                  pl.BlockSpec((eb, D_MODEL), lambda i, e: (e, 0)),
                  pl.BlockSpec((D_MODEL, eb), lambda i, e: (0, e))],
        out_specs=pl.BlockSpec((D_MODEL, tb), lambda i, e: (0, i)),
        out_shape=jax.ShapeDtypeStruct((D_MODEL, t), F32),
        scratch_shapes=[pltpu.VMEM((PEER_HEADS * PEER_KEYS, tb), F32)] * 4
                       + [pltpu.VMEM((eb, tb), BF16)],
        compiler_params=_params(("parallel", "arbitrary")),
        name="peer",
    )(xn, qht, st, a1, a2, u, vt)


def _final_body(h_ref, pt_ref, g_ref, y_ref):
    y_ref[...] = _rms(h_ref[...] + pt_ref[...].T, g_ref[...])


def _final(h, pt, g, tb):
    t = h.shape[0]
    return pl.pallas_call(
        _final_body,
        grid=(t // tb,),
        in_specs=[pl.BlockSpec((tb, D_MODEL), lambda i: (i, 0)),
                  pl.BlockSpec((D_MODEL, tb), lambda i: (0, i)),
                  pl.BlockSpec((1, D_MODEL), lambda i: (0, 0))],
        out_specs=pl.BlockSpec((tb, D_MODEL), lambda i: (i, 0)),
        out_shape=jax.ShapeDtypeStruct((t, D_MODEL), F32),
        compiler_params=_params(("parallel",)),
        name="final",
    )(h, pt, g)


def _pick(n, pref):
    b = min(n, pref)
    while n % b:
        b //= 2
    return b


def _token_stage(x, att, conv, p, tb_mid, tb_peer):
    h, xn, qht, st = _mid(x, att, conv, p["woa"], p["woc"], p["ffn_g"], p["wqt"], p["a1s"], p["a2s"], tb_mid)
    pt = _peer(xn, qht, st, p["a1d"], p["a2d"], p["u"], p["vt"], tb_peer, 2048)
    return _final(h, pt, p["final_g"], tb_mid)


def kernel(x_prompt, x_sample, cache_k, cache_v, cache_kidx, state_conv, attn_norm_g, w_in, conv_w, conv_b,
           conv_ln_g, conv_ln_b, w_out, ffn_norm_g, peer_wq, peer_subkeys, peer_u, peer_v, final_norm_g):
    assert w_in.shape[0] == 1, "single layer"
    nbp, s, _ = x_prompt.shape
    nb, tn, _ = x_sample.shape
    assert nbp == 1
    past = cache_k.shape[2]

    wi = w_in[0]
    o_qi = 3 * ATT_WIDTH
    o_ki = o_qi + N_IDX_HEADS * IDX_DIM
    o_wi = o_ki + IDX_DIM
    o_glu = o_wi + N_IDX_HEADS
    wqkv = wi[:, :o_qi].astype(BF16)
    widx = jnp.concatenate([wi[:, o_qi:o_glu], jnp.zeros((D_MODEL, 128 - IDX_DIM - N_IDX_HEADS), F32)],
                           axis=1).astype(BF16)
    wglu = wi[:, o_glu:].astype(BF16)
    g_attn = attn_norm_g[0][None, :]
    cw = jnp.concatenate([conv_w[0], jnp.zeros((CONV_TAIL - CONV_WIDTH, CONV_CH), F32)], axis=0)
    cb, cg, cbeta = conv_b[0][None, :], conv_ln_g[0][None, :], conv_ln_b[0][None, :]
    eye = jnp.eye(PEER_HEADS, dtype=F32)
    sk = peer_subkeys[0]
    a_s = [jnp.einsum("id,hg->ihgd", sk[c], eye).reshape(PEER_KEYS * PEER_HEADS, PEER_HEADS * PEER_HALF)
           .astype(BF16) for c in range(2)]
    a_d = [jnp.einsum("id,hg->higd", sk[c], eye).reshape(PEER_KEYS * PEER_HEADS, PEER_HEADS * PEER_HALF)
           .astype(BF16) for c in range(2)]
    wq = peer_wq[0].reshape(D_MODEL, PEER_HEADS, 2, PEER_HALF).transpose(2, 1, 3, 0)
    p = {
        "woa": w_out[0][:ATT_WIDTH].astype(BF16), "woc": w_out[0][ATT_WIDTH:].astype(BF16),
        "ffn_g": ffn_norm_g[0][None, :], "final_g": final_norm_g[None, :],
        "wqt": wq.reshape(D_MODEL, D_MODEL).astype(BF16),
        "a1s": a_s[0], "a2s": a_s[1], "a1d": a_d[0], "a2d": a_d[1],
        "u": peer_u[0].astype(BF16), "vt": peer_v[0].T.astype(BF16),
    }
    w_scale = (N_IDX_HEADS ** -0.5) * (IDX_DIM ** -0.5)

    xp = x_prompt[0]
    tb = _pick(s, 256)
    k, v, kw, glu, q_hm, k_hm, vt_hm, qi_hm = _in_proj(xp, g_attn, wqkv, widx, wglu, tb)
    conv_p, tail_p = _conv(glu, jnp.zeros((1, CONV_TAIL, CONV_CH), F32), cw, cb, cg, cbeta, 1, tb)
    ki = kw[:, :IDX_DIM]
    ki_b = ki.astype(BF16)
    w_t = (kw[:, IDX_DIM:IDX_DIM + N_IDX_HEADS] * w_scale).T
    tq = _pick(s, 256)
    sel = _idx_sel(qi_hm, w_t, ki_b, tq)
    att_p = _attn(q_hm, k_hm, vt_hm, sel, tq, _pick(s, 1024))
    y_p = _token_stage(xp, att_p, conv_p, p, tb, _pick(s, 512))

    xs = x_sample.reshape(nb * tn, D_MODEL)
    ts = nb * tn
    tbs = _pick(ts, 256)
    ks, vs, kws, glus, qs_hm, _, _, qis_hm = _in_proj(xs, g_attn, wqkv, widx, wglu, tbs)
    pad = jnp.zeros((nb, CONV_TAIL - (CONV_WIDTH - 1), CONV_CH), F32)
    conv_s, tail_s = _conv(glus, jnp.concatenate([pad, state_conv[0]], axis=1), cw, cb, cg, cbeta, nb, tn)
    hm4 = lambda a, nh, dt=BF16: a.reshape(nb, tn, nh, a.shape[1] // nh).transpose(0, 2, 1, 3).astype(dt)
    per_stream = lambda a: a.reshape(a.shape[0], nb, tn, a.shape[2]).transpose(1, 0, 2, 3)
    kis = kws[:, :IDX_DIM]
    att_s = _sample_attn(
        per_stream(qs_hm), per_stream(qis_hm),
        (kws[:, IDX_DIM:IDX_DIM + N_IDX_HEADS] * w_scale).reshape(nb, tn, N_IDX_HEADS),
        cache_kidx[0], kis.reshape(nb, tn, IDX_DIM),
        cache_k[0].reshape(nb, past, ATT_WIDTH), cache_v[0].reshape(nb, past, ATT_WIDTH),
        hm4(ks, N_HEADS), hm4(vs, N_HEADS))
    att_s = att_s.transpose(0, 2, 1, 3).reshape(ts, ATT_WIDTH)
    y_s = _token_stage(xs, att_s, conv_s, p, tbs, tbs)

    hd = (N_HEADS, HEAD_DIM)
    keep = CONV_TAIL - (CONV_WIDTH - 1)
    return (y_p[None], y_s.reshape(nb, tn, D_MODEL),
            k.reshape(1, 1, s, *hd), v.reshape(1, 1, s, *hd), ki[None, None], tail_p[None, :, keep:],
            ks.reshape(1, nb, tn, *hd), vs.reshape(1, nb, tn, *hd), kis.reshape(1, nb, tn, IDX_DIM),
            tail_s[None, :, keep:])
```

```python
import functools

import jax
import jax.numpy as jnp
from jax import lax
from jax.experimental import pallas as pl
from jax.experimental.pallas import tpu as pltpu

F32 = jnp.float32
BF16 = jnp.bfloat16
I32 = jnp.int32

D_MODEL = 1024
N_HEADS = 8
HEAD_DIM = 64
ATT_WIDTH = N_HEADS * HEAD_DIM
PV_ROWS = HEAD_DIM + 16
N_IDX_HEADS = 4
IDX_DIM = 64
TOPK = 256
CHUNK_SHIFT = 6
CONV_CH = 512
CONV_WIDTH = 31
CONV_TAIL = 32
PEER_HEADS = 8
PEER_KEYS = 128
PEER_HALF = 64
PEER_TOPK = 16
EPS = 1e-6
INT_MIN = -2147483648
ORDER_MASK = 0x7FFFFFFF
INV_SQRT2 = 0.7071067811865476
LOG2E = 1.4426950408889634
NEG_INF = float("-inf")
M_INIT = -1e30

VMEM_LIMIT = 56 * 1024 * 1024


def _params(sem, vmem=VMEM_LIMIT):
    return pltpu.CompilerParams(dimension_semantics=sem, vmem_limit_bytes=vmem)


def _nt(a, b):
    return lax.dot_general(a, b, (((1,), (1,)), ((), ())), preferred_element_type=F32)


def _dot(a, b):
    return jnp.dot(a, b, preferred_element_type=F32)


def _rms(x, g):
    return x * lax.rsqrt(jnp.mean(x * x, axis=-1, keepdims=True) + EPS) * g


def _order_key(x):
    b = pltpu.bitcast(x, I32)
    return b ^ ((b >> 31) & ORDER_MASK)


def _in_proj_body(x_ref, g_ref, wqkv_ref, widx_ref, wglu_ref,
                  k_ref, v_ref, kw_ref, glu_ref, qh_ref, kh_ref, vth_ref, qih_ref):
    xn = _rms(x_ref[...], g_ref[...]).astype(BF16)
    qkv = _dot(xn, wqkv_ref[...])
    q = qkv[:, :ATT_WIDTH] * (HEAD_DIM ** -0.5 * LOG2E)
    k = qkv[:, ATT_WIDTH:2 * ATT_WIDTH]
    v = qkv[:, 2 * ATT_WIDTH:]
    k_ref[...] = k
    v_ref[...] = v
    for h in range(N_HEADS):
        head = slice(h * HEAD_DIM, (h + 1) * HEAD_DIM)
        qh_ref[h] = q[:, head].astype(BF16)
        kh_ref[h] = k[:, head].astype(BF16)
    vth_ref[:, 0:HEAD_DIM, :] = v.T.reshape(N_HEADS, HEAD_DIM, v.shape[0]).astype(BF16)
    vth_ref[:, HEAD_DIM:, :] = jnp.ones((N_HEADS, PV_ROWS - HEAD_DIM, v.shape[0]), BF16)
    ix = _dot(xn, widx_ref[...])
    for h in range(N_IDX_HEADS):
        qih_ref[h] = ix[:, h * IDX_DIM:(h + 1) * IDX_DIM].astype(BF16)
    kw_ref[...] = ix[:, N_IDX_HEADS * IDX_DIM:]
    glu_ref[...] = _dot(xn, wglu_ref[...])


def _in_proj(x, g, wqkv, widx, wglu, tb):
    t = x.shape[0]
    row = lambda w: pl.BlockSpec((tb, w), lambda i: (i, 0))
    full = lambda a: pl.BlockSpec(a.shape, lambda i: (0, 0))
    heads = lambda n: pl.BlockSpec((n, tb, HEAD_DIM), lambda i: (0, i, 0))
    return pl.pallas_call(
        _in_proj_body,
        grid=(t // tb,),
        in_specs=[row(D_MODEL), full(g), full(wqkv), full(widx), full(wglu)],
        out_specs=[row(ATT_WIDTH), row(ATT_WIDTH), row(128), row(2 * CONV_CH),
                   heads(N_HEADS), heads(N_HEADS),
                   pl.BlockSpec((N_HEADS, PV_ROWS, tb), lambda i: (0, 0, i)), heads(N_IDX_HEADS)],
        out_shape=[jax.ShapeDtypeStruct((t, ATT_WIDTH), F32),
                   jax.ShapeDtypeStruct((t, ATT_WIDTH), F32),
                   jax.ShapeDtypeStruct((t, 128), F32),
                   jax.ShapeDtypeStruct((t, 2 * CONV_CH), F32),
                   jax.ShapeDtypeStruct((N_HEADS, t, HEAD_DIM), BF16),
                   jax.ShapeDtypeStruct((N_HEADS, t, HEAD_DIM), BF16),
                   jax.ShapeDtypeStruct((N_HEADS, PV_ROWS, t), BF16),
                   jax.ShapeDtypeStruct((N_IDX_HEADS, t, IDX_DIM), BF16)],
        compiler_params=_params(("parallel",)),
        name="in_proj",
    )(x, g, wqkv, widx, wglu)


def _conv_body(glu_ref, tail0_ref, w_ref, b_ref, g_ref, beta_ref, out_ref, tail_ref, ubuf, shifted, *, tb):
    j = pl.program_id(1)

    @pl.when(j == 0)
    def _():
        ubuf[0:CONV_TAIL, :] = tail0_ref[0]

    @pl.when(j > 0)
    def _():
        ubuf[0:CONV_TAIL, :] = ubuf[tb:tb + CONV_TAIL, :]

    glu = glu_ref[...]
    ubuf[CONV_TAIL:CONV_TAIL + tb, :] = glu[:, :CONV_CH] * jax.nn.sigmoid(glu[:, CONV_CH:])
    off = CONV_TAIL - (CONV_WIDTH - 1)
    acc = None
    for res in range(8):
        taps = [t for t in range(CONV_WIDTH) if (off + t) % 8 == res]
        lo, hi = (off + taps[0]) // 8, (off + taps[-1]) // 8
        n = 8 * (hi - lo) + tb
        shifted[0:n, :] = ubuf[res + 8 * lo:res + 8 * lo + n, :]
        for t in taps:
            a = 8 * ((off + t) // 8 - lo)
            term = shifted[a:a + tb, :] * w_ref[t:t + 1, :]
            acc = term if acc is None else acc + term
    y = acc + b_ref[...]
    mu = jnp.mean(y, axis=-1, keepdims=True)
    d = y - mu
    var = jnp.mean(d * d, axis=-1, keepdims=True)
    z = d * lax.rsqrt(var + EPS) * g_ref[...] + beta_ref[...]
    out_ref[...] = (z * jax.nn.sigmoid(z)).astype(BF16)
    tail_ref[0] = ubuf[tb:tb + CONV_TAIL, :]


def _conv(glu, tail0, w, b, g, beta, nseq, tb):
    t = glu.shape[0]
    nblk = t // (nseq * tb)
    full = lambda a: pl.BlockSpec(a.shape, lambda s, j: (0, 0))
    return pl.pallas_call(
        functools.partial(_conv_body, tb=tb),
        grid=(nseq, nblk),
        in_specs=[pl.BlockSpec((tb, 2 * CONV_CH), lambda s, j: (s * nblk + j, 0)),
                  pl.BlockSpec((1, CONV_TAIL, CONV_CH), lambda s, j: (s, 0, 0)),
                  full(w), full(b), full(g), full(beta)],
        out_specs=[pl.BlockSpec((tb, CONV_CH), lambda s, j: (s * nblk + j, 0)),
                   pl.BlockSpec((1, CONV_TAIL, CONV_CH), lambda s, j: (s, 0, 0))],
        out_shape=[jax.ShapeDtypeStruct((t, CONV_CH), BF16),
                   jax.ShapeDtypeStruct((nseq, CONV_TAIL, CONV_CH), F32)],
        scratch_shapes=[pltpu.VMEM((tb + CONV_TAIL, CONV_CH), F32)] * 2,
        compiler_params=_params(("parallel", "arbitrary")),
        name="conv",
    )(glu, tail0, w, b, g, beta)


def _idx_scores_t(ki_tile, qi_ref, w):
    acc = None
    for h in range(N_IDX_HEADS):
        term = w[h:h + 1, :] * jnp.maximum(_nt(ki_tile, qi_ref[h]), 0.0)
        acc = term if acc is None else acc + term
    return acc


def _chunk_limit(q0, tq):
    qpos = q0 + lax.broadcasted_iota(I32, (1, tq), 1)
    return ((qpos >> CHUNK_SHIFT) + 1) << CHUNK_SHIFT


def _bit_planes(words):
    x = list(words)
    mask, j = 0x0000FFFF, 16
    while j:
        k = 0
        while k < 32:
            t = (x[k] ^ lax.shift_right_logical(x[k + j], jnp.int32(j))) & mask
            x[k] = x[k] ^ t
            x[k + j] = x[k + j] ^ (t << j)
            k = (k + j + 1) & ~j
        j >>= 1
        mask ^= (mask << j) & 0xFFFFFFFF
    return x


def _sublane_prefix(x):
    sub = lax.broadcasted_iota(I32, x.shape, 0)
    for sh in (1, 2, 4):
        x = x + jnp.where(sub >= sh, pltpu.roll(x, sh, axis=0), 0)
    return x


def _sel_body(qi_ref, w_ref, ki_ref, sel_ref, planes_ref, cand_ref, gt_ref, *, tq, grp):
    i = pl.program_id(0)
    nt = i + 1
    ng = (nt + grp - 1) // grp
    w = w_ref[...]
    limit = _chunk_limit(i * tq, tq)
    rows = lax.broadcasted_iota(I32, (tq, tq), 0)
    assert tq == 8 * 32

    def fill(t, diagonal):
        r0 = pl.multiple_of(t * tq, tq)
        b = pltpu.bitcast(_idx_scores_t(ki_ref[pl.ds(r0, tq), :], qi_ref, w), I32)
        u = b ^ ((b >> 31) | INT_MIN)
        if diagonal:
            u = jnp.where(rows + r0 < limit, u, 0)
        planes = _bit_planes([u[8 * k:8 * k + 8, :] for k in range(32)])
        at = pl.ds(pl.multiple_of(t * 8, 8), 8)
        for p in range(32):
            planes_ref[p, at, :] = planes[p]
        cand_ref[at, :] = jnp.full((8, tq), -1, I32)
        gt_ref[at, :] = jnp.zeros((8, tq), I32)

    def fill_full(t, c):
        fill(t, False)
        return c

    lax.fori_loop(0, nt - 1, fill_full, 0)
    fill(nt - 1, True)

    def pad(t, c):
        at = pl.ds(pl.multiple_of(t * 8, 8), 8)
        planes_ref[:, at, :] = jnp.zeros((32, 8, tq), I32)
        cand_ref[at, :] = jnp.zeros((8, tq), I32)
        gt_ref[at, :] = jnp.zeros((8, tq), I32)
        return c

    lax.fori_loop(nt, ng * grp, pad, 0)

    def sweep(p_prev, flip, p_next):
        def group(g, acc):
            at = pl.ds(pl.multiple_of(g * (grp * 8), grp * 8), grp * 8)
            c = cand_ref[at, :]
            if p_prev is not None:
                prev = planes_ref[p_prev, at, :]
                gt_ref[at, :] = gt_ref[at, :] | (c & prev & flip)
                c = c & (prev ^ flip)
                cand_ref[at, :] = c
            if p_next is not None:
                hit = lax.population_count(c & planes_ref[p_next, at, :])
                acc = acc + jnp.sum(hit.reshape(grp, 8, tq), axis=0)
            return acc
        acc = lax.fori_loop(0, ng, group, jnp.zeros((8, tq), I32))
        return jnp.sum(acc, axis=0, keepdims=True)

    def decide(p, cnt, kth_u, need):
        one = cnt >= need
        kth_u = jnp.where(one, kth_u | lax.shift_right_logical(jnp.int32(INT_MIN), p), kth_u)
        return kth_u, jnp.where(one, need, need - cnt), jnp.where(one, 0, -1)

    zero = jnp.zeros((1, tq), I32)
    kth_u, need, flip = decide(jnp.int32(0), sweep(None, None, 0), zero, jnp.full((1, tq), TOPK, I32))

    def bit_step(p, carry):
        kth_u, need, flip = carry
        return decide(p, sweep(p - 1, flip, p), kth_u, need)

    kth_u, need, flip = lax.fori_loop(1, 32, bit_step, (kth_u, need, flip))
    sweep(31, flip, None)
    need = jnp.where(kth_u == 0, 0, need)

    def tie_count(t):
        c = cand_ref[pl.ds(pl.multiple_of(t * 8, 8), 8), :]
        return c, jnp.sum(lax.population_count(c), axis=0, keepdims=True)

    def find(t, carry):
        seen, part, words = carry
        c, n = tie_count(t)
        hit = (seen < need) & (seen + n > need)
        return seen + n, jnp.where(hit, need - seen, part), jnp.where(hit, c, words)

    _, part, words = lax.fori_loop(0, nt, find, (zero, zero, jnp.zeros((8, tq), I32)))
    keep = jnp.zeros((8, tq), I32)
    before = zero
    for k in range(32):
        bit = lax.shift_right_logical(words, jnp.int32(31 - k)) & 1
        rank = before + _sublane_prefix(bit)
        keep = keep | jnp.where((bit == 1) & (rank <= part), INT_MIN if k == 0 else 1 << (31 - k), 0)
        before = rank[7:8, :]

    def emit(t, seen):
        c, n = tie_count(t)
        at = pl.ds(pl.multiple_of(t * 8, 8), 8)
        whole = seen + n <= need
        partial = (seen < need) & jnp.logical_not(whole)
        sel_ref[0, at, :] = gt_ref[at, :] | jnp.where(whole, c, jnp.where(partial, keep, 0))
        return seen + n

    lax.fori_loop(0, nt, emit, zero)

    def clear(t, c):
        sel_ref[0, pl.ds(pl.multiple_of(t * 8, 8), 8), :] = jnp.zeros((8, tq), I32)
        return c

    lax.fori_loop(nt, sel_ref.shape[1] // 8, clear, 0)


def _idx_sel(qi_hm, w_t, ki, tq):
    s = ki.shape[0]
    nrow = s // tq * 8
    return pl.pallas_call(
        functools.partial(_sel_body, tq=tq, grp=_pick(s // tq, 16)),
        grid=(s // tq,),
        in_specs=[pl.BlockSpec((N_IDX_HEADS, tq, IDX_DIM), lambda i: (0, i, 0)),
                  pl.BlockSpec((N_IDX_HEADS, tq), lambda i: (0, i)),
                  pl.BlockSpec((s, IDX_DIM), lambda i: (0, 0))],
        out_specs=pl.BlockSpec((1, nrow, tq), lambda i: (i, 0, 0)),
        out_shape=jax.ShapeDtypeStruct((s // tq, nrow, tq), I32),
        scratch_shapes=[pltpu.VMEM((32, nrow, tq), I32), pltpu.VMEM((nrow, tq), I32),
                        pltpu.VMEM((nrow, tq), I32)],
        compiler_params=_params(("arbitrary",)),
        name="idx_sel",
    )(qi_hm, w_t, ki)


def _attn_body(qb_ref, kb_ref, q_ref, k_ref, vt_ref, sel_ref,
               out_ref, m_ref, acc_ref, s_ref, p_ref, bias_ref, *, tq, tk):
    step = pl.program_id(0)
    i = qb_ref[step]
    j = kb_ref[step]

    @pl.when(j == 0)
    def _():
        m_ref[...] = jnp.full(m_ref.shape, M_INIT, F32)
        acc_ref[...] = jnp.zeros(acc_ref.shape, F32)

    for tile in range(tk // tq):
        words = sel_ref[0, tile * 8:(tile + 1) * 8, :]
        for k in range(32):
            bias_ref[pl.ds(tile * tq + 8 * k, 8), :] = jnp.where((words << k) < 0, 0.0, NEG_INF)
    bias = bias_ref[...]
    tops = []
    for h in range(N_HEADS):
        s = _nt(k_ref[h], q_ref[h]) + bias
        s_ref[h] = s
        tops.append(jnp.max(s, axis=0, keepdims=True))
    m_old = m_ref[...]
    m_new = jnp.maximum(m_old, jnp.concatenate(tops, axis=0))
    alpha = jnp.exp2(m_old - m_new)
    for h in range(N_HEADS):
        p_ref[h] = jnp.exp2(s_ref[h] - m_new[h:h + 1, :]).astype(BF16)
    m_ref[...] = m_new
    for h in range(N_HEADS):
        acc_ref[h] = alpha[h:h + 1, :] * acc_ref[h] + _dot(vt_ref[h], p_ref[h])

    @pl.when(j == ((i + 1) * tq - 1) // tk)
    def _():
        o = jnp.concatenate([acc_ref[h, 0:HEAD_DIM, :] / acc_ref[h, HEAD_DIM:HEAD_DIM + 1, :]
                             for h in range(N_HEADS)], axis=0)
        out_ref[...] = o.T.astype(BF16)


def _attn(q_hm, k_hm, vt_hm, sel, tq, tk):
    s = q_hm.shape[1]
    assert tk % tq == 0
    qb, kb = [], []
    for i in range(s // tq):
        for j in range(((i + 1) * tq - 1) // tk + 1):
            qb.append(i)
            kb.append(j)
    qb = jnp.asarray(qb, I32)
    kb = jnp.asarray(kb, I32)
    grid_spec = pltpu.PrefetchScalarGridSpec(
        num_scalar_prefetch=2,
        grid=(int(qb.shape[0]),),
        in_specs=[pl.BlockSpec((N_HEADS, tq, HEAD_DIM), lambda t, qb, kb: (0, qb[t], 0)),
                  pl.BlockSpec((N_HEADS, tk, HEAD_DIM), lambda t, qb, kb: (0, kb[t], 0)),
                  pl.BlockSpec((N_HEADS, PV_ROWS, tk), lambda t, qb, kb: (0, 0, kb[t])),
                  pl.BlockSpec((1, tk // tq * 8, tq), lambda t, qb, kb: (qb[t], kb[t], 0))],
        out_specs=pl.BlockSpec((tq, ATT_WIDTH), lambda t, qb, kb: (qb[t], 0)),
        scratch_shapes=[pltpu.VMEM((N_HEADS, tq), F32),
                        pltpu.VMEM((N_HEADS, PV_ROWS, tq), F32),
                        pltpu.VMEM((N_HEADS, tk, tq), F32),
                        pltpu.VMEM((N_HEADS, tk, tq), BF16),
                        pltpu.VMEM((tk, tq), F32)],
    )
    return pl.pallas_call(
        functools.partial(_attn_body, tq=tq, tk=tk),
        grid_spec=grid_spec,
        out_shape=jax.ShapeDtypeStruct((s, ATT_WIDTH), BF16),
        compiler_params=_params(("arbitrary",)),
        name="attn",
    )(qb, kb, q_hm, k_hm, vt_hm, sel)


NEW_PAD = 128


def _sample_attn_body(q_ref, qi_ref, w_ref, kic_ref, kin_ref, kc_ref, vc_ref, kn_ref, vn_ref,
                      out_ref, kinp, knp, vnp, *, past, tn):
    kinp[...] = jnp.zeros(kinp.shape, BF16)
    kinp[0:tn, :] = kin_ref[0].astype(BF16)
    knp[...] = jnp.zeros(knp.shape, BF16)
    vnp[...] = jnp.zeros(vnp.shape, BF16)
    for h in range(N_HEADS):
        knp[h, 0:tn, :] = kn_ref[0, h]
        vnp[h, 0:tn, :] = vn_ref[0, h]

    w = w_ref[0]
    kic = kic_ref[0].astype(BF16)
    kin = kinp[...]
    sc = None
    sn = None
    for h in range(N_IDX_HEADS):
        qih = qi_ref[0, h]
        wc = w[:, h:h + 1]
        tc = wc * jnp.maximum(_nt(qih, kic), 0.0)
        tnw = wc * jnp.maximum(_nt(qih, kin), 0.0)
        sc = tc if sc is None else sc + tc
        sn = tnw if sn is None else sn + tnw
    new_ok = lax.broadcasted_iota(I32, (tn, NEW_PAD), 1) < tn
    keyc = _order_key(sc)
    keyn = jnp.where(new_ok, _order_key(sn), INT_MIN)

    def count(pc, pn):
        return (jnp.sum(jnp.where(pc, 1, 0), axis=1, keepdims=True)
                + jnp.sum(jnp.where(pn, 1, 0), axis=1, keepdims=True))

    def bit_step(p, u):
        cand_u = u | (jnp.int32(1) << (31 - p))
        cand = cand_u ^ INT_MIN
        return jnp.where(count(keyc >= cand, keyn >= cand) >= TOPK, cand_u, u)

    kth = lax.fori_loop(0, 32, bit_step, jnp.zeros((tn, 1), I32)) ^ INT_MIN
    need = (TOPK - count(keyc > kth, keyn > kth)).astype(F32)

    cw = 256
    tri = jnp.where(lax.broadcasted_iota(I32, (cw, cw), 0) <= lax.broadcasted_iota(I32, (cw, cw), 1),
                    1.0, 0.0).astype(BF16)
    seen = jnp.zeros((tn, 1), F32)
    bias_c = []
    for c in range(past // cw):
        kc_ = keyc[:, c * cw:(c + 1) * cw]
        eq = kc_ == kth
        pre = _dot(jnp.where(eq, 1.0, 0.0).astype(BF16), tri)
        sel = (kc_ > kth) | (eq & (pre + seen <= need))
        bias_c.append(jnp.where(sel, 0.0, NEG_INF))
        seen = seen + pre[:, cw - 1:cw]
    bias_c = jnp.concatenate(bias_c, axis=1)
    eqn = keyn == kth
    pren = _dot(jnp.where(eqn, 1.0, 0.0).astype(BF16), tri[:NEW_PAD, :NEW_PAD])
    bias_n = jnp.where(new_ok & ((keyn > kth) | (eqn & (pren + seen <= need))), 0.0, NEG_INF)

    kc = kc_ref[0].astype(BF16)
    vc = vc_ref[0].astype(BF16)
    for h in range(N_HEADS):
        qh = q_ref[0, h]
        head = slice(h * HEAD_DIM, (h + 1) * HEAD_DIM)
        s_c = _nt(qh, kc[:, head]) + bias_c
        s_n = _nt(qh, knp[h]) + bias_n
        m = jnp.maximum(jnp.max(s_c, axis=1, keepdims=True), jnp.max(s_n, axis=1, keepdims=True))
        p_c = jnp.exp2(s_c - m)
        p_n = jnp.exp2(s_n - m)
        l = jnp.sum(p_c, axis=1, keepdims=True) + jnp.sum(p_n, axis=1, keepdims=True)
        o = _dot(p_c.astype(BF16), vc[:, head]) + _dot(p_n.astype(BF16), vnp[h])
        out_ref[0, h] = (o / l).astype(BF16)


def _sample_attn(q_hm, qi_hm, w, kic, kin, kc, vc, kn_hm, vn_hm):
    nb, _, tn, _ = q_hm.shape
    past = kic.shape[1]
    b4 = lambda a: pl.BlockSpec((1,) + a.shape[1:], lambda b: (b, 0, 0, 0))
    b3 = lambda a: pl.BlockSpec((1,) + a.shape[1:], lambda b: (b, 0, 0))
    return pl.pallas_call(
        functools.partial(_sample_attn_body, past=past, tn=tn),
        grid=(nb,),
        in_specs=[b4(q_hm), b4(qi_hm), b3(w), b3(kic), b3(kin), b3(kc), b3(vc), b4(kn_hm), b4(vn_hm)],
        out_specs=pl.BlockSpec((1, N_HEADS, tn, HEAD_DIM), lambda b: (b, 0, 0, 0)),
        out_shape=jax.ShapeDtypeStruct((nb, N_HEADS, tn, HEAD_DIM), BF16),
        scratch_shapes=[pltpu.VMEM((NEW_PAD, IDX_DIM), BF16),
                        pltpu.VMEM((N_HEADS, NEW_PAD, HEAD_DIM), BF16),
                        pltpu.VMEM((N_HEADS, NEW_PAD, HEAD_DIM), BF16)],
        compiler_params=_params(("parallel",)),
        name="sample_attn",
    )(q_hm, qi_hm, w, kic, kin, kc, vc, kn_hm, vn_hm)


def _cmpx(a, b):
    return jnp.maximum(a, b), jnp.minimum(a, b)


def _bitonic_merge_desc(v):
    n = len(v)
    v = list(v)
    j = n // 2
    while j >= 1:
        for i in range(n):
            l = i ^ j
            if l > i:
                v[i], v[l] = _cmpx(v[i], v[l])
        j //= 2
    return v


def _bitonic_sort_desc(v):
    n = len(v)
    v = list(v)
    k = 2
    while k <= n:
        j = k // 2
        while j >= 1:
            for i in range(n):
                l = i ^ j
                if l > i:
                    hi, lo = _cmpx(v[i], v[l])
                    v[i], v[l] = (hi, lo) if (i & k) == 0 else (lo, hi)
            j //= 2
        k *= 2
    return v


def _merge_top(a, b):
    n = len(a)
    return _bitonic_merge_desc([jnp.maximum(a[k], b[n - 1 - k]) for k in range(n)])


def _top16_desc(vals):
    groups = [_bitonic_sort_desc(vals[g:g + PEER_TOPK]) for g in range(0, len(vals), PEER_TOPK)]
    while len(groups) > 1:
        groups = [_merge_top(groups[g], groups[g + 1]) for g in range(0, len(groups), 2)]
    return groups[0]


def _mid_body(x_ref, att_ref, conv_ref, woa_ref, woc_ref, g_ref, wqt_ref, a1_ref, a2_ref,
              h_ref, xn_ref, qht_ref, st_ref):
    h = x_ref[...] + _dot(att_ref[...], woa_ref[...]) + _dot(conv_ref[...], woc_ref[...])
    h_ref[...] = h
    xn = _rms(h, g_ref[...]).astype(BF16)
    xn_ref[...] = xn
    qht = _nt(wqt_ref[...], xn).astype(BF16)
    qht_ref[...] = qht
    half = PEER_HEADS * PEER_HALF
    s1 = _dot(a1_ref[...], qht[:half])
    s2 = _dot(a2_ref[...], qht[half:])
    rows = lambda s: [s[r * PEER_HEADS:(r + 1) * PEER_HEADS, :] for r in range(PEER_KEYS)]
    v1 = _top16_desc(rows(s1))
    v2 = _top16_desc(rows(s2))
    ninf = jnp.full(v1[0].shape, NEG_INF, F32)
    top = [v1[0] + v2[b] for b in range(PEER_TOPK)]
    for a in range(1, PEER_TOPK):
        n_a = PEER_TOPK // (a + 1)
        top = _merge_top(top, [v1[a] + v2[b] if b < n_a else ninf for b in range(PEER_TOPK)])
    z = jnp.ones_like(top[0])
    for k in range(1, PEER_TOPK):
        z = z + jnp.exp(top[k] - top[0])
    st_ref[0:8, :] = v1[0]
    st_ref[8:16, :] = v2[0]
    st_ref[16:24, :] = top[PEER_TOPK - 1]
    st_ref[24:32, :] = 1.0 / z


def _mid(x, att, conv, woa, woc, g, wqt, a1, a2, tb):
    t = x.shape[0]
    row = lambda w: pl.BlockSpec((tb, w), lambda i: (i, 0))
    col = lambda r: pl.BlockSpec((r, tb), lambda i: (0, i))
    full = lambda a: pl.BlockSpec(a.shape, lambda i: (0, 0))
    return pl.pallas_call(
        _mid_body,
        grid=(t // tb,),
        in_specs=[row(D_MODEL), row(ATT_WIDTH), row(CONV_CH), full(woa), full(woc), full(g),
                  full(wqt), full(a1), full(a2)],
        out_specs=[row(D_MODEL), row(D_MODEL), col(D_MODEL), col(32)],
        out_shape=[jax.ShapeDtypeStruct((t, D_MODEL), F32),
                   jax.ShapeDtypeStruct((t, D_MODEL), BF16),
                   jax.ShapeDtypeStruct((D_MODEL, t), BF16),
                   jax.ShapeDtypeStruct((32, t), F32)],
        compiler_params=_params(("parallel",)),
        name="mid",
    )(x, att, conv, woa, woc, g, wqt, a1, a2)


def _peer_body(xn_ref, qht_ref, st_ref, a1_ref, a2_ref, u_ref, vt_ref, out_ref,
               s1_ref, s2_ref, e1_ref, e2_ref, hid_ref, *, eb):
    e = pl.program_id(1)
    nsub = eb // PEER_KEYS
    half = PEER_HEADS * PEER_HALF

    @pl.when(e == 0)
    def _():
        qht = qht_ref[...]
        s1 = _dot(a1_ref[...], qht[:half])
        s2 = _dot(a2_ref[...], qht[half:])
        s1_ref[...] = s1
        s2_ref[...] = s2
        for h in range(PEER_HEADS):
            r = slice(h * PEER_KEYS, (h + 1) * PEER_KEYS)
            e1_ref[r, :] = jnp.exp(s1[r] - st_ref[h:h + 1, :])
            e2_ref[r, :] = jnp.exp(s2[r] - st_ref[8 + h:9 + h, :]) * st_ref[24 + h:25 + h, :] * 0.5
        out_ref[...] = jnp.zeros(out_ref.shape, F32)

    xn = xn_ref[...]
    for ii in range(nsub):
        i = e * nsub + ii
        a = _nt(u_ref[ii * PEER_KEYS:(ii + 1) * PEER_KEYS, :], xn)
        gate = None
        for h in range(PEER_HEADS):
            r = slice(h * PEER_KEYS, (h + 1) * PEER_KEYS)
            pair = s1_ref[pl.ds(h * PEER_KEYS + i, 1), :] + s2_ref[r, :]
            g = e1_ref[pl.ds(h * PEER_KEYS + i, 1), :] * e2_ref[r, :]
            g = jnp.where(pair >= st_ref[16 + h:17 + h, :], g, 0.0)
            gate = g if gate is None else gate + g
        hid = a * (1.0 + lax.erf(a * INV_SQRT2)) * gate
        hid_ref[ii * PEER_KEYS:(ii + 1) * PEER_KEYS, :] = hid.astype(BF16)
    out_ref[...] += _dot(vt_ref[...], hid_ref[...])


def _peer(xn, qht, st, a1, a2, u, vt, tb, eb):
    t = xn.shape[0]
    ne = u.shape[0] // eb
    full = lambda a: pl.BlockSpec(a.shape, lambda i, e: (0, 0))
    return pl.pallas_call(
        functools.partial(_peer_body, eb=eb),
        grid=(t // tb, ne),
        in_specs=[pl.BlockSpec((tb, D_MODEL), lambda i, e: (i, 0)),
                  pl.BlockSpec((D_MODEL, tb), lambda i, e: (0, i)),
                  pl.BlockSpec((32, tb), lambda i, e: (0, i)),
                  full(a1), full(a2),
                  pl.BlockSpec((eb, D_MODEL), lambda i, e: (e, 0)),
                  pl.BlockSpec((D_MODEL, eb), lambda i, e: (0, e))],
        out_specs=pl.BlockSpec((D_MODEL, tb), lambda i, e: (0, i)),
        out_shape=jax.ShapeDtypeStruct((D_MODEL, t), F32),
        scratch_shapes=[pltpu.VMEM((PEER_HEADS * PEER_KEYS, tb), F32)] * 4
                       + [pltpu.VMEM((eb, tb), BF16)],
        compiler_params=_params(("parallel", "arbitrary")),
        name="peer",
    )(xn, qht, st, a1, a2, u, vt)


def _final_body(h_ref, pt_ref, g_ref, y_ref):
    y_ref[...] = _rms(h_ref[...] + pt_ref[...].T, g_ref[...])


def _final(h, pt, g, tb):
    t = h.shape[0]
    return pl.pallas_call(
        _final_body,
        grid=(t // tb,),
        in_specs=[pl.BlockSpec((tb, D_MODEL), lambda i: (i, 0)),
                  pl.BlockSpec((D_MODEL, tb), lambda i: (0, i)),
                  pl.BlockSpec((1, D_MODEL), lambda i: (0, 0))],
        out_specs=pl.BlockSpec((tb, D_MODEL), lambda i: (i, 0)),
        out_shape=jax.ShapeDtypeStruct((t, D_MODEL), F32),
        compiler_params=_params(("parallel",)),
        name="final",
    )(h, pt, g)


def _pick(n, pref):
    b = min(n, pref)
    while n % b:
        b //= 2
    return b


def _token_stage(x, att, conv, p, tb_mid, tb_peer):
    h, xn, qht, st = _mid(x, att, conv, p["woa"], p["woc"], p["ffn_g"], p["wqt"], p["a1s"], p["a2s"], tb_mid)
    pt = _peer(xn, qht, st, p["a1d"], p["a2d"], p["u"], p["vt"], tb_peer, 2048)
    return _final(h, pt, p["final_g"], tb_mid)


def kernel(x_prompt, x_sample, cache_k, cache_v, cache_kidx, state_conv, attn_norm_g, w_in, conv_w, conv_b,
           conv_ln_g, conv_ln_b, w_out, ffn_norm_g, peer_wq, peer_subkeys, peer_u, peer_v, final_norm_g):
    assert w_in.shape[0] == 1, "single layer"
    nbp, s, _ = x_prompt.shape
    nb, tn, _ = x_sample.shape
    assert nbp == 1
    past = cache_k.shape[2]

    wi = w_in[0]
    o_qi = 3 * ATT_WIDTH
    o_ki = o_qi + N_IDX_HEADS * IDX_DIM
    o_wi = o_ki + IDX_DIM
    o_glu = o_wi + N_IDX_HEADS
    wqkv = wi[:, :o_qi].astype(BF16)
    widx = jnp.concatenate([wi[:, o_qi:o_glu], jnp.zeros((D_MODEL, 128 - IDX_DIM - N_IDX_HEADS), F32)],
                           axis=1).astype(BF16)
    wglu = wi[:, o_glu:].astype(BF16)
    g_attn = attn_norm_g[0][None, :]
    cw = jnp.concatenate([conv_w[0], jnp.zeros((CONV_TAIL - CONV_WIDTH, CONV_CH), F32)], axis=0)
    cb, cg, cbeta = conv_b[0][None, :], conv_ln_g[0][None, :], conv_ln_b[0][None, :]
    eye = jnp.eye(PEER_HEADS, dtype=F32)
    sk = peer_subkeys[0]
    a_s = [jnp.einsum("id,hg->ihgd", sk[c], eye).reshape(PEER_KEYS * PEER_HEADS, PEER_HEADS * PEER_HALF)
           .astype(BF16) for c in range(2)]
    a_d = [jnp.einsum("id,hg->higd", sk[c], eye).reshape(PEER_KEYS * PEER_HEADS, PEER_HEADS * PEER_HALF)
           .astype(BF16) for c in range(2)]
    wq = peer_wq[0].reshape(D_MODEL, PEER_HEADS, 2, PEER_HALF).transpose(2, 1, 3, 0)
    p = {
        "woa": w_out[0][:ATT_WIDTH].astype(BF16), "woc": w_out[0][ATT_WIDTH:].astype(BF16),
        "ffn_g": ffn_norm_g[0][None, :], "final_g": final_norm_g[None, :],
        "wqt": wq.reshape(D_MODEL, D_MODEL).astype(BF16),
        "a1s": a_s[0], "a2s": a_s[1], "a1d": a_d[0], "a2d": a_d[1],
        "u": peer_u[0].astype(BF16), "vt": peer_v[0].T.astype(BF16),
    }
    w_scale = (N_IDX_HEADS ** -0.5) * (IDX_DIM ** -0.5)

    xp = x_prompt[0]
    tb = _pick(s, 256)
    k, v, kw, glu, q_hm, k_hm, vt_hm, qi_hm = _in_proj(xp, g_attn, wqkv, widx, wglu, tb)
    conv_p, tail_p = _conv(glu, jnp.zeros((1, CONV_TAIL, CONV_CH), F32), cw, cb, cg, cbeta, 1, tb)
    ki = kw[:, :IDX_DIM]
    ki_b = ki.astype(BF16)
    w_t = (kw[:, IDX_DIM:IDX_DIM + N_IDX_HEADS] * w_scale).T
    tq = _pick(s, 256)
    sel = _idx_sel(qi_hm, w_t, ki_b, tq)
    att_p = _attn(q_hm, k_hm, vt_hm, sel, tq, _pick(s, 1024))
    y_p = _token_stage(xp, att_p, conv_p, p, tb, _pick(s, 512))

    xs = x_sample.reshape(nb * tn, D_MODEL)
    ts = nb * tn
    tbs = _pick(ts, 256)
    ks, vs, kws, glus, qs_hm, _, _, qis_hm = _in_proj(xs, g_attn, wqkv, widx, wglu, tbs)
    pad = jnp.zeros((nb, CONV_TAIL - (CONV_WIDTH - 1), CONV_CH), F32)
    conv_s, tail_s = _conv(glus, jnp.concatenate([pad, state_conv[0]], axis=1), cw, cb, cg, cbeta, nb, tn)
    hm4 = lambda a, nh, dt=BF16: a.reshape(nb, tn, nh, a.shape[1] // nh).transpose(0, 2, 1, 3).astype(dt)
    per_stream = lambda a: a.reshape(a.shape[0], nb, tn, a.shape[2]).transpose(1, 0, 2, 3)
    kis = kws[:, :IDX_DIM]
    att_s = _sample_attn(
        per_stream(qs_hm), per_stream(qis_hm),
        (kws[:, IDX_DIM:IDX_DIM + N_IDX_HEADS] * w_scale).reshape(nb, tn, N_IDX_HEADS),
        cache_kidx[0], kis.reshape(nb, tn, IDX_DIM),
        cache_k[0].reshape(nb, past, ATT_WIDTH), cache_v[0].reshape(nb, past, ATT_WIDTH),
        hm4(ks, N_HEADS), hm4(vs, N_HEADS))
    att_s = att_s.transpose(0, 2, 1, 3).reshape(ts, ATT_WIDTH)
    y_s = _token_stage(xs, att_s, conv_s, p, tbs, tbs)

    hd = (N_HEADS, HEAD_DIM)
    keep = CONV_TAIL - (CONV_WIDTH - 1)
    return (y_p[None], y_s.reshape(nb, tn, D_MODEL),
            k.reshape(1, 1, s, *hd), v.reshape(1, 1, s, *hd), ki[None, None], tail_p[None, :, keep:],
            ks.reshape(1, nb, tn, *hd), vs.reshape(1, nb, tn, *hd), kis.reshape(1, nb, tn, IDX_DIM),
            tail_s[None, :, keep:])
```

```python
import functools

import jax
import jax.numpy as jnp
from jax import lax
from jax.experimental import pallas as pl
from jax.experimental.pallas import tpu as pltpu

F32 = jnp.float32
BF16 = jnp.bfloat16
I32 = jnp.int32

D_MODEL = 1024
N_HEADS = 8
HEAD_DIM = 64
ATT_WIDTH = N_HEADS * HEAD_DIM
PV_ROWS = HEAD_DIM + 16
N_IDX_HEADS = 4
IDX_DIM = 64
TOPK = 256
CHUNK_SHIFT = 6
CONV_CH = 512
CONV_WIDTH = 31
CONV_TAIL = 32
PEER_HEADS = 8
PEER_KEYS = 128
PEER_HALF = 64
PEER_TOPK = 16
EPS = 1e-6
INT_MIN = -2147483648
ORDER_MASK = 0x7FFFFFFF
INV_SQRT2 = 0.7071067811865476
LOG2E = 1.4426950408889634
NEG_INF = float("-inf")
M_INIT = -1e30

VMEM_LIMIT = 56 * 1024 * 1024


def _params(sem, vmem=VMEM_LIMIT):
    return pltpu.CompilerParams(dimension_semantics=sem, vmem_limit_bytes=vmem)


def _nt(a, b):
    return lax.dot_general(a, b, (((1,), (1,)), ((), ())), preferred_element_type=F32)


def _dot(a, b):
    return jnp.dot(a, b, preferred_element_type=F32)


def _rms(x, g):
    return x * lax.rsqrt(jnp.mean(x * x, axis=-1, keepdims=True) + EPS) * g


def _order_key(x):
    b = pltpu.bitcast(x, I32)
    return b ^ ((b >> 31) & ORDER_MASK)


def _in_proj_body(x_ref, g_ref, wqkv_ref, widx_ref, wglu_ref,
                  k_ref, v_ref, kw_ref, glu_ref, qh_ref, kh_ref, vth_ref, qih_ref):
    xn = _rms(x_ref[...], g_ref[...]).astype(BF16)
    qkv = _dot(xn, wqkv_ref[...])
    q = qkv[:, :ATT_WIDTH] * (HEAD_DIM ** -0.5 * LOG2E)
    k = qkv[:, ATT_WIDTH:2 * ATT_WIDTH]
    v = qkv[:, 2 * ATT_WIDTH:]
    k_ref[...] = k
    v_ref[...] = v
    for h in range(N_HEADS):
        head = slice(h * HEAD_DIM, (h + 1) * HEAD_DIM)
        qh_ref[h] = q[:, head].astype(BF16)
        kh_ref[h] = k[:, head].astype(BF16)
    vth_ref[:, 0:HEAD_DIM, :] = v.T.reshape(N_HEADS, HEAD_DIM, v.shape[0]).astype(BF16)
    vth_ref[:, HEAD_DIM:, :] = jnp.ones((N_HEADS, PV_ROWS - HEAD_DIM, v.shape[0]), BF16)
    ix = _dot(xn, widx_ref[...])
    for h in range(N_IDX_HEADS):
        qih_ref[h] = ix[:, h * IDX_DIM:(h + 1) * IDX_DIM].astype(BF16)
    kw_ref[...] = ix[:, N_IDX_HEADS * IDX_DIM:]
    glu_ref[...] = _dot(xn, wglu_ref[...])


def _in_proj(x, g, wqkv, widx, wglu, tb):
    t = x.shape[0]
    row = lambda w: pl.BlockSpec((tb, w), lambda i: (i, 0))
    full = lambda a: pl.BlockSpec(a.shape, lambda i: (0, 0))
    heads = lambda n: pl.BlockSpec((n, tb, HEAD_DIM), lambda i: (0, i, 0))
    return pl.pallas_call(
        _in_proj_body,
        grid=(t // tb,),
        in_specs=[row(D_MODEL), full(g), full(wqkv), full(widx), full(wglu)],
        out_specs=[row(ATT_WIDTH), row(ATT_WIDTH), row(128), row(2 * CONV_CH),
                   heads(N_HEADS), heads(N_HEADS),
                   pl.BlockSpec((N_HEADS, PV_ROWS, tb), lambda i: (0, 0, i)), heads(N_IDX_HEADS)],
        out_shape=[jax.ShapeDtypeStruct((t, ATT_WIDTH), F32),
                   jax.ShapeDtypeStruct((t, ATT_WIDTH), F32),
                   jax.ShapeDtypeStruct((t, 128), F32),
                   jax.ShapeDtypeStruct((t, 2 * CONV_CH), F32),
                   jax.ShapeDtypeStruct((N_HEADS, t, HEAD_DIM), BF16),
                   jax.ShapeDtypeStruct((N_HEADS, t, HEAD_DIM), BF16),
                   jax.ShapeDtypeStruct((N_HEADS, PV_ROWS, t), BF16),
                   jax.ShapeDtypeStruct((N_IDX_HEADS, t, IDX_DIM), BF16)],
        compiler_params=_params(("parallel",)),
        name="in_proj",
    )(x, g, wqkv, widx, wglu)


def _conv_body(glu_ref, tail0_ref, w_ref, b_ref, g_ref, beta_ref, out_ref, tail_ref, ubuf, shifted, *, tb):
    j = pl.program_id(1)

    @pl.when(j == 0)
    def _():
        ubuf[0:CONV_TAIL, :] = tail0_ref[0]

    @pl.when(j > 0)
    def _():
        ubuf[0:CONV_TAIL, :] = ubuf[tb:tb + CONV_TAIL, :]

    glu = glu_ref[...]
    ubuf[CONV_TAIL:CONV_TAIL + tb, :] = glu[:, :CONV_CH] * jax.nn.sigmoid(glu[:, CONV_CH:])
    off = CONV_TAIL - (CONV_WIDTH - 1)
    acc = None
    for res in range(8):
        taps = [t for t in range(CONV_WIDTH) if (off + t) % 8 == res]
        lo, hi = (off + taps[0]) // 8, (off + taps[-1]) // 8
        n = 8 * (hi - lo) + tb
        shifted[0:n, :] = ubuf[res + 8 * lo:res + 8 * lo + n, :]
        for t in taps:
            a = 8 * ((off + t) // 8 - lo)
            term = shifted[a:a + tb, :] * w_ref[t:t + 1, :]
            acc = term if acc is None else acc + term
    y = acc + b_ref[...]
    mu = jnp.mean(y, axis=-1, keepdims=True)
    d = y - mu
    var = jnp.mean(d * d, axis=-1, keepdims=True)
    z = d * lax.rsqrt(var + EPS) * g_ref[...] + beta_ref[...]
    out_ref[...] = (z * jax.nn.sigmoid(z)).astype(BF16)
    tail_ref[0] = ubuf[tb:tb + CONV_TAIL, :]


def _conv(glu, tail0, w, b, g, beta, nseq, tb):
    t = glu.shape[0]
    nblk = t // (nseq * tb)
    full = lambda a: pl.BlockSpec(a.shape, lambda s, j: (0, 0))
    return pl.pallas_call(
        functools.partial(_conv_body, tb=tb),
        grid=(nseq, nblk),
        in_specs=[pl.BlockSpec((tb, 2 * CONV_CH), lambda s, j: (s * nblk + j, 0)),
                  pl.BlockSpec((1, CONV_TAIL, CONV_CH), lambda s, j: (s, 0, 0)),
                  full(w), full(b), full(g), full(beta)],
        out_specs=[pl.BlockSpec((tb, CONV_CH), lambda s, j: (s * nblk + j, 0)),
                   pl.BlockSpec((1, CONV_TAIL, CONV_CH), lambda s, j: (s, 0, 0))],
        out_shape=[jax.ShapeDtypeStruct((t, CONV_CH), BF16),
                   jax.ShapeDtypeStruct((nseq, CONV_TAIL, CONV_CH), F32)],
        scratch_shapes=[pltpu.VMEM((tb + CONV_TAIL, CONV_CH), F32)] * 2,
        compiler_params=_params(("parallel", "arbitrary")),
        name="conv",
    )(glu, tail0, w, b, g, beta)


def _idx_scores_t(ki_tile, qi_ref, w):
    acc = None
    for h in range(N_IDX_HEADS):
        term = w[h:h + 1, :] * jnp.maximum(_nt(ki_tile, qi_ref[h]), 0.0)
        acc = term if acc is None else acc + term
    return acc


def _chunk_limit(q0, tq):
    qpos = q0 + lax.broadcasted_iota(I32, (1, tq), 1)
    return ((qpos >> CHUNK_SHIFT) + 1) << CHUNK_SHIFT


def _bit_planes(words):
    x = list(words)
    mask, j = 0x0000FFFF, 16
    while j:
        k = 0
        while k < 32:
            t = (x[k] ^ lax.shift_right_logical(x[k + j], jnp.int32(j))) & mask
            x[k] = x[k] ^ t
            x[k + j] = x[k + j] ^ (t << j)
            k = (k + j + 1) & ~j
        j >>= 1
        mask ^= (mask << j) & 0xFFFFFFFF
    return x


def _sublane_prefix(x):
    sub = lax.broadcasted_iota(I32, x.shape, 0)
    for sh in (1, 2, 4):
        x = x + jnp.where(sub >= sh, pltpu.roll(x, sh, axis=0), 0)
    return x


def _sel_body(qi_ref, w_ref, ki_ref, sel_ref, planes_ref, cand_ref, gt_ref, *, tq, grp):
    i = pl.program_id(0)
    nt = i + 1
    ng = (nt + grp - 1) // grp
    w = w_ref[...]
    limit = _chunk_limit(i * tq, tq)
    rows = lax.broadcasted_iota(I32, (tq, tq), 0)
    assert tq == 8 * 32

    def fill(t, diagonal):
        r0 = pl.multiple_of(t * tq, tq)
        b = pltpu.bitcast(_idx_scores_t(ki_ref[pl.ds(r0, tq), :], qi_ref, w), I32)
        u = b ^ ((b >> 31) | INT_MIN)
        if diagonal:
            u = jnp.where(rows + r0 < limit, u, 0)
        planes = _bit_planes([u[8 * k:8 * k + 8, :] for k in range(32)])
        at = pl.ds(pl.multiple_of(t * 8, 8), 8)
        for p in range(32):
            planes_ref[p, at, :] = planes[p]
        cand_ref[at, :] = jnp.full((8, tq), -1, I32)
        gt_ref[at, :] = jnp.zeros((8, tq), I32)

    def fill_full(t, c):
        fill(t, False)
        return c

    lax.fori_loop(0, nt - 1, fill_full, 0)
    fill(nt - 1, True)

    def pad(t, c):
        at = pl.ds(pl.multiple_of(t * 8, 8), 8)
        planes_ref[:, at, :] = jnp.zeros((32, 8, tq), I32)
        cand_ref[at, :] = jnp.zeros((8, tq), I32)
        gt_ref[at, :] = jnp.zeros((8, tq), I32)
        return c

    lax.fori_loop(nt, ng * grp, pad, 0)

    def sweep(p_prev, flip, p_next):
        def group(g, acc):
            at = pl.ds(pl.multiple_of(g * (grp * 8), grp * 8), grp * 8)
            c = cand_ref[at, :]
            if p_prev is not None:
                prev = planes_ref[p_prev, at, :]
                gt_ref[at, :] = gt_ref[at, :] | (c & prev & flip)
                c = c & (prev ^ flip)
                cand_ref[at, :] = c
            if p_next is not None:
                hit = lax.population_count(c & planes_ref[p_next, at, :])
                acc = acc + jnp.sum(hit.reshape(grp, 8, tq), axis=0)
            return acc
        acc = lax.fori_loop(0, ng, group, jnp.zeros((8, tq), I32))
        return jnp.sum(acc, axis=0, keepdims=True)

    def decide(p, cnt, kth_u, need):
        one = cnt >= need
        kth_u = jnp.where(one, kth_u | lax.shift_right_logical(jnp.int32(INT_MIN), p), kth_u)
        return kth_u, jnp.where(one, need, need - cnt), jnp.where(one, 0, -1)

    zero = jnp.zeros((1, tq), I32)
    kth_u, need, flip = decide(jnp.int32(0), sweep(None, None, 0), zero, jnp.full((1, tq), TOPK, I32))

    def bit_step(p, carry):
        kth_u, need, flip = carry
        return decide(p, sweep(p - 1, flip, p), kth_u, need)

    kth_u, need, flip = lax.fori_loop(1, 32, bit_step, (kth_u, need, flip))
    sweep(31, flip, None)
    need = jnp.where(kth_u == 0, 0, need)

    def tie_count(t):
        c = cand_ref[pl.ds(pl.multiple_of(t * 8, 8), 8), :]
        return c, jnp.sum(lax.population_count(c), axis=0, keepdims=True)

    def find(t, carry):
        seen, part, words = carry
        c, n = tie_count(t)
        hit = (seen < need) & (seen + n > need)
        return seen + n, jnp.where(hit, need - seen, part), jnp.where(hit, c, words)

    _, part, words = lax.fori_loop(0, nt, find, (zero, zero, jnp.zeros((8, tq), I32)))
    keep = jnp.zeros((8, tq), I32)
    before = zero
    for k in range(32):
        bit = lax.shift_right_logical(words, jnp.int32(31 - k)) & 1
        rank = before + _sublane_prefix(bit)
        keep = keep | jnp.where((bit == 1) & (rank <= part), INT_MIN if k == 0 else 1 << (31 - k), 0)
        before = rank[7:8, :]

    def emit(t, seen):
        c, n = tie_count(t)
        at = pl.ds(pl.multiple_of(t * 8, 8), 8)
        whole = seen + n <= need
        partial = (seen < need) & jnp.logical_not(whole)
        sel_ref[0, at, :] = gt_ref[at, :] | jnp.where(whole, c, jnp.where(partial, keep, 0))
        return seen + n

    lax.fori_loop(0, nt, emit, zero)

    def clear(t, c):
        sel_ref[0, pl.ds(pl.multiple_of(t * 8, 8), 8), :] = jnp.zeros((8, tq), I32)
        return c

    lax.fori_loop(nt, sel_ref.shape[1] // 8, clear, 0)


def _idx_sel(qi_hm, w_t, ki, tq):
    s = ki.shape[0]
    nrow = s // tq * 8
    return pl.pallas_call(
        functools.partial(_sel_body, tq=tq, grp=_pick(s // tq, 16)),
        grid=(s // tq,),
        in_specs=[pl.BlockSpec((N_IDX_HEADS, tq, IDX_DIM), lambda i: (0, i, 0)),
                  pl.BlockSpec((N_IDX_HEADS, tq), lambda i: (0, i)),
                  pl.BlockSpec((s, IDX_DIM), lambda i: (0, 0))],
        out_specs=pl.BlockSpec((1, nrow, tq), lambda i: (i, 0, 0)),
        out_shape=jax.ShapeDtypeStruct((s // tq, nrow, tq), I32),
        scratch_shapes=[pltpu.VMEM((32, nrow, tq), I32), pltpu.VMEM((nrow, tq), I32),
                        pltpu.VMEM((nrow, tq), I32)],
        compiler_params=_params(("arbitrary",)),
        name="idx_sel",
    )(qi_hm, w_t, ki)


def _attn_body(qb_ref, kb_ref, q_ref, k_ref, vt_ref, sel_ref,
               out_ref, acc_ref, s_ref, p_ref, bias_ref, m_ref, *, tq, tk):
    step = pl.program_id(0)
    i = qb_ref[step]
    j = kb_ref[step]

    @pl.when(j == 0)
    def _():
        m_ref[...] = jnp.full(m_ref.shape, M_INIT, F32)
        acc_ref[...] = jnp.zeros(acc_ref.shape, F32)

    for tile in range(tk // tq):
        words = sel_ref[0, tile * 8:(tile + 1) * 8, :]
        for k in range(32):
            bias_ref[pl.ds(tile * tq + 8 * k, 8), :] = jnp.where((words << k) < 0, 0.0, NEG_INF)
    bias = bias_ref[...]
    tops = []
    for h in range(N_HEADS):
        s = _nt(k_ref[h], q_ref[h]) + bias
        s_ref[h] = s
        tops.append(jnp.max(s, axis=0, keepdims=True))
    m_old = m_ref[...]
    m_new = jnp.maximum(m_old, jnp.concatenate(tops, axis=0))
    alpha = jnp.exp2(m_old - m_new)
    for h in range(N_HEADS):
        p_ref[h] = jnp.exp2(s_ref[h] - m_new[h:h + 1, :]).astype(BF16)
    m_ref[...] = m_new
    for h in range(N_HEADS):
        acc_ref[h] = alpha[h:h + 1, :] * acc_ref[h] + _dot(vt_ref[h], p_ref[h])

    @pl.when(j == ((i + 1) * tq - 1) // tk)
    def _():
        o = jnp.concatenate([acc_ref[h, 0:HEAD_DIM, :] / acc_ref[h, HEAD_DIM:HEAD_DIM + 1, :]
                             for h in range(N_HEADS)], axis=0)
        out_ref[...] = o.T.astype(BF16)


def _attn(q_hm, k_hm, vt_hm, sel, tq, tk):
    s = q_hm.shape[1]
    assert tk % tq == 0
    qb, kb = [], []
    for i in range(s // tq):
        for j in range(((i + 1) * tq - 1) // tk + 1):
            qb.append(i)
            kb.append(j)
    qb = jnp.asarray(qb, I32)
    kb = jnp.asarray(kb, I32)
    grid_spec = pltpu.PrefetchScalarGridSpec(
        num_scalar_prefetch=2,
        grid=(int(qb.shape[0]),),
        in_specs=[pl.BlockSpec((N_HEADS, tq, HEAD_DIM), lambda t, qb, kb: (0, qb[t], 0)),
                  pl.BlockSpec((N_HEADS, tk, HEAD_DIM), lambda t, qb, kb: (0, kb[t], 0)),
                  pl.BlockSpec((N_HEADS, PV_ROWS, tk), lambda t, qb, kb: (0, 0, kb[t])),
                  pl.BlockSpec((1, tk // tq * 8, tq), lambda t, qb, kb: (qb[t], kb[t], 0))],
        out_specs=pl.BlockSpec((tq, ATT_WIDTH), lambda t, qb, kb: (qb[t], 0)),
        scratch_shapes=[pltpu.VMEM((N_HEADS, PV_ROWS, tq), F32),
                        pltpu.VMEM((N_HEADS, tk, tq), F32),
                        pltpu.VMEM((N_HEADS, tk, tq), BF16),
                        pltpu.VMEM((tk, tq), F32),
                        pltpu.VMEM((N_HEADS, tq), F32)],
    )
    return pl.pallas_call(
        functools.partial(_attn_body, tq=tq, tk=tk),
        grid_spec=grid_spec,
        out_shape=jax.ShapeDtypeStruct((s, ATT_WIDTH), BF16),
        compiler_params=_params(("arbitrary",)),
        name="attn",
    )(qb, kb, q_hm, k_hm, vt_hm, sel)


NEW_PAD = 128


def _sample_attn_body(q_ref, qi_ref, w_ref, kic_ref, kin_ref, kc_ref, vc_ref, kn_ref, vn_ref,
                      out_ref, kinp, knp, vnp, *, past, tn):
    kinp[...] = jnp.zeros(kinp.shape, BF16)
    kinp[0:tn, :] = kin_ref[0].astype(BF16)
    knp[...] = jnp.zeros(knp.shape, BF16)
    vnp[...] = jnp.zeros(vnp.shape, BF16)
    for h in range(N_HEADS):
        knp[h, 0:tn, :] = kn_ref[0, h]
        vnp[h, 0:tn, :] = vn_ref[0, h]

    w = w_ref[0]
    kic = kic_ref[0].astype(BF16)
    kin = kinp[...]
    sc = None
    sn = None
    for h in range(N_IDX_HEADS):
        qih = qi_ref[0, h]
        wc = w[:, h:h + 1]
        tc = wc * jnp.maximum(_nt(qih, kic), 0.0)
        tnw = wc * jnp.maximum(_nt(qih, kin), 0.0)
        sc = tc if sc is None else sc + tc
        sn = tnw if sn is None else sn + tnw
    new_ok = lax.broadcasted_iota(I32, (tn, NEW_PAD), 1) < tn
    keyc = _order_key(sc)
    keyn = jnp.where(new_ok, _order_key(sn), INT_MIN)

    def count(pc, pn):
        return (jnp.sum(jnp.where(pc, 1, 0), axis=1, keepdims=True)
                + jnp.sum(jnp.where(pn, 1, 0), axis=1, keepdims=True))

    def bit_step(p, u):
        cand_u = u | (jnp.int32(1) << (31 - p))
        cand = cand_u ^ INT_MIN
        return jnp.where(count(keyc >= cand, keyn >= cand) >= TOPK, cand_u, u)

    kth = lax.fori_loop(0, 32, bit_step, jnp.zeros((tn, 1), I32)) ^ INT_MIN
    need = (TOPK - count(keyc > kth, keyn > kth)).astype(F32)

    cw = 256
    tri = jnp.where(lax.broadcasted_iota(I32, (cw, cw), 0) <= lax.broadcasted_iota(I32, (cw, cw), 1),
                    1.0, 0.0).astype(BF16)
    seen = jnp.zeros((tn, 1), F32)
    bias_c = []
    for c in range(past // cw):
        kc_ = keyc[:, c * cw:(c + 1) * cw]
        eq = kc_ == kth
        pre = _dot(jnp.where(eq, 1.0, 0.0).astype(BF16), tri)
        sel = (kc_ > kth) | (eq & (pre + seen <= need))
        bias_c.append(jnp.where(sel, 0.0, NEG_INF))
        seen = seen + pre[:, cw - 1:cw]
    bias_c = jnp.concatenate(bias_c, axis=1)
    eqn = keyn == kth
    pren = _dot(jnp.where(eqn, 1.0, 0.0).astype(BF16), tri[:NEW_PAD, :NEW_PAD])
    bias_n = jnp.where(new_ok & ((keyn > kth) | (eqn & (pren + seen <= need))), 0.0, NEG_INF)

    kc = kc_ref[0].astype(BF16)
    vc = vc_ref[0].astype(BF16)
    for h in range(N_HEADS):
        qh = q_ref[0, h]
        head = slice(h * HEAD_DIM, (h + 1) * HEAD_DIM)
        s_c = _nt(qh, kc[:, head]) + bias_c
        s_n = _nt(qh, knp[h]) + bias_n
        m = jnp.maximum(jnp.max(s_c, axis=1, keepdims=True), jnp.max(s_n, axis=1, keepdims=True))
        p_c = jnp.exp2(s_c - m)
        p_n = jnp.exp2(s_n - m)
        l = jnp.sum(p_c, axis=1, keepdims=True) + jnp.sum(p_n, axis=1, keepdims=True)
        o = _dot(p_c.astype(BF16), vc[:, head]) + _dot(p_n.astype(BF16), vnp[h])
        out_ref[0, h] = (o / l).astype(BF16)


def _sample_attn(q_hm, qi_hm, w, kic, kin, kc, vc, kn_hm, vn_hm):
    nb, _, tn, _ = q_hm.shape
    past = kic.shape[1]
    b4 = lambda a: pl.BlockSpec((1,) + a.shape[1:], lambda b: (b, 0, 0, 0))
    b3 = lambda a: pl.BlockSpec((1,) + a.shape[1:], lambda b: (b, 0, 0))
    return pl.pallas_call(
        functools.partial(_sample_attn_body, past=past, tn=tn),
        grid=(nb,),
        in_specs=[b4(q_hm), b4(qi_hm), b3(w), b3(kic), b3(kin), b3(kc), b3(vc), b4(kn_hm), b4(vn_hm)],
        out_specs=pl.BlockSpec((1, N_HEADS, tn, HEAD_DIM), lambda b: (b, 0, 0, 0)),
        out_shape=jax.ShapeDtypeStruct((nb, N_HEADS, tn, HEAD_DIM), BF16),
        scratch_shapes=[pltpu.VMEM((NEW_PAD, IDX_DIM), BF16),
                        pltpu.VMEM((N_HEADS, NEW_PAD, HEAD_DIM), BF16),
                        pltpu.VMEM((N_HEADS, NEW_PAD, HEAD_DIM), BF16)],
        compiler_params=_params(("parallel",)),
        name="sample_attn",
    )(q_hm, qi_hm, w, kic, kin, kc, vc, kn_hm, vn_hm)


def _cmpx(a, b):
    return jnp.maximum(a, b), jnp.minimum(a, b)


def _bitonic_merge_desc(v):
    n = len(v)
    v = list(v)
    j = n // 2
    while j >= 1:
        for i in range(n):
            l = i ^ j
            if l > i:
                v[i], v[l] = _cmpx(v[i], v[l])
        j //= 2
    return v


def _bitonic_sort_desc(v):
    n = len(v)
    v = list(v)
    k = 2
    while k <= n:
        j = k // 2
        while j >= 1:
            for i in range(n):
                l = i ^ j
                if l > i:
                    hi, lo = _cmpx(v[i], v[l])
                    v[i], v[l] = (hi, lo) if (i & k) == 0 else (lo, hi)
            j //= 2
        k *= 2
    return v


def _merge_top(a, b):
    n = len(a)
    return _bitonic_merge_desc([jnp.maximum(a[k], b[n - 1 - k]) for k in range(n)])


def _top16_desc(vals):
    groups = [_bitonic_sort_desc(vals[g:g + PEER_TOPK]) for g in range(0, len(vals), PEER_TOPK)]
    while len(groups) > 1:
        groups = [_merge_top(groups[g], groups[g + 1]) for g in range(0, len(groups), 2)]
    return groups[0]


def _mid_body(x_ref, att_ref, conv_ref, woa_ref, woc_ref, g_ref, wqt_ref, a1_ref, a2_ref,
              h_ref, xn_ref, qht_ref, st_ref):
    h = x_ref[...] + _dot(att_ref[...], woa_ref[...]) + _dot(conv_ref[...], woc_ref[...])
    h_ref[...] = h
    xn = _rms(h, g_ref[...]).astype(BF16)
    xn_ref[...] = xn
    qht = _nt(wqt_ref[...], xn).astype(BF16)
    qht_ref[...] = qht
    half = PEER_HEADS * PEER_HALF
    s1 = _dot(a1_ref[...], qht[:half])
    s2 = _dot(a2_ref[...], qht[half:])
    rows = lambda s: [s[r * PEER_HEADS:(r + 1) * PEER_HEADS, :] for r in range(PEER_KEYS)]
    v1 = _top16_desc(rows(s1))
    v2 = _top16_desc(rows(s2))
    ninf = jnp.full(v1[0].shape, NEG_INF, F32)
    top = [v1[0] + v2[b] for b in range(PEER_TOPK)]
    for a in range(1, PEER_TOPK):
        n_a = PEER_TOPK // (a + 1)
        top = _merge_top(top, [v1[a] + v2[b] if b < n_a else ninf for b in range(PEER_TOPK)])
    z = jnp.ones_like(top[0])
    for k in range(1, PEER_TOPK):
        z = z + jnp.exp(top[k] - top[0])
    st_ref[0:8, :] = v1[0]
    st_ref[8:16, :] = v2[0]
    st_ref[16:24, :] = top[PEER_TOPK - 1]
    st_ref[24:32, :] = 1.0 / z


def _mid(x, att, conv, woa, woc, g, wqt, a1, a2, tb):
    t = x.shape[0]
    row = lambda w: pl.BlockSpec((tb, w), lambda i: (i, 0))
    col = lambda r: pl.BlockSpec((r, tb), lambda i: (0, i))
    full = lambda a: pl.BlockSpec(a.shape, lambda i: (0, 0))
    return pl.pallas_call(
        _mid_body,
        grid=(t // tb,),
        in_specs=[row(D_MODEL), row(ATT_WIDTH), row(CONV_CH), full(woa), full(woc), full(g),
                  full(wqt), full(a1), full(a2)],
        out_specs=[row(D_MODEL), row(D_MODEL), col(D_MODEL), col(32)],
        out_shape=[jax.ShapeDtypeStruct((t, D_MODEL), F32),
                   jax.ShapeDtypeStruct((t, D_MODEL), BF16),
                   jax.ShapeDtypeStruct((D_MODEL, t), BF16),
                   jax.ShapeDtypeStruct((32, t), F32)],
        compiler_params=_params(("parallel",)),
        name="mid",
    )(x, att, conv, woa, woc, g, wqt, a1, a2)


def _peer_body(xn_ref, qht_ref, st_ref, a1_ref, a2_ref, u_ref, vt_ref, out_ref,
               s1_ref, s2_ref, e1_ref, e2_ref, hid_ref, *, eb):
    e = pl.program_id(1)
    nsub = eb // PEER_KEYS
    half = PEER_HEADS * PEER_HALF

    @pl.when(e == 0)
    def _():
        qht = qht_ref[...]
        s1 = _dot(a1_ref[...], qht[:half])
        s2 = _dot(a2_ref[...], qht[half:])
        s1_ref[...] = s1
        s2_ref[...] = s2
        for h in range(PEER_HEADS):
            r = slice(h * PEER_KEYS, (h + 1) * PEER_KEYS)
            e1_ref[r, :] = jnp.exp(s1[r] - st_ref[h:h + 1, :])
            e2_ref[r, :] = jnp.exp(s2[r] - st_ref[8 + h:9 + h, :]) * st_ref[24 + h:25 + h, :] * 0.5
        out_ref[...] = jnp.zeros(out_ref.shape, F32)

    xn = xn_ref[...]
    for ii in range(nsub):
        i = e * nsub + ii
        a = _nt(u_ref[ii * PEER_KEYS:(ii + 1) * PEER_KEYS, :], xn)
        gate = None
        for h in range(PEER_HEADS):
            r = slice(h * PEER_KEYS, (h + 1) * PEER_KEYS)
            pair = s1_ref[pl.ds(h * PEER_KEYS + i, 1), :] + s2_ref[r, :]
            g = e1_ref[pl.ds(h * PEER_KEYS + i, 1), :] * e2_ref[r, :]
            g = jnp.where(pair >= st_ref[16 + h:17 + h, :], g, 0.0)
            gate = g if gate is None else gate + g
        hid = a * (1.0 + lax.erf(a * INV_SQRT2)) * gate
        hid_ref[ii * PEER_KEYS:(ii + 1) * PEER_KEYS, :] = hid.astype(BF16)
    out_ref[...] += _dot(vt_ref[...], hid_ref[...])


def _peer(xn, qht, st, a1, a2, u, vt, tb, eb):
    t = xn.shape[0]
    ne = u.shape[0] // eb
    full = lambda a: pl.BlockSpec(a.shape, lambda i, e: (0, 0))
    return pl.pallas_call(
        functools.partial(_peer_body, eb=eb),
        grid=(t // tb, ne),
        in_specs=[pl.BlockSpec((tb, D_MODEL), lambda i, e: (i, 0)),
                  pl.BlockSpec((D_MODEL, tb), lambda i, e: (0, i)),
                  pl.BlockSpec((32, tb), lambda i, e: (0, i)),
                  full(a1), full(a2),
                  pl.BlockSpec((eb, D_MODEL), lambda i, e: (e, 0)),
                  pl.BlockSpec((D_MODEL, eb), lambda i, e: (0, e))],
        out_specs=pl.BlockSpec((D_MODEL, tb), lambda i, e: (0, i)),
        out_shape=jax.ShapeDtypeStruct((D_MODEL, t), F32),
        scratch_shapes=[pltpu.VMEM((PEER_HEADS * PEER_KEYS, tb), F32)] * 4
                       + [pltpu.VMEM((eb, tb), BF16)],
        compiler_params=_params(("parallel", "arbitrary")),
        name="peer",
    )(xn, qht, st, a1, a2, u, vt)


def _final_body(h_ref, pt_ref, g_ref, y_ref):
    y_ref[...] = _rms(h_ref[...] + pt_ref[...].T, g_ref[...])


def _final(h, pt, g, tb):
    t = h.shape[0]
    return pl.pallas_call(
        _final_body,
        grid=(t // tb,),
        in_specs=[pl.BlockSpec((tb, D_MODEL), lambda i: (i, 0)),
                  pl.BlockSpec((D_MODEL, tb), lambda i: (0, i)),
                  pl.BlockSpec((1, D_MODEL), lambda i: (0, 0))],
        out_specs=pl.BlockSpec((tb, D_MODEL), lambda i: (i, 0)),
        out_shape=jax.ShapeDtypeStruct((t, D_MODEL), F32),
        compiler_params=_params(("parallel",)),
        name="final",
    )(h, pt, g)


def _pick(n, pref):
    b = min(n, pref)
    while n % b:
        b //= 2
    return b


def _token_stage(x, att, conv, p, tb_mid, tb_peer):
    h, xn, qht, st = _mid(x, att, conv, p["woa"], p["woc"], p["ffn_g"], p["wqt"], p["a1s"], p["a2s"], tb_mid)
    pt = _peer(xn, qht, st, p["a1d"], p["a2d"], p["u"], p["vt"], tb_peer, 2048)
    return _final(h, pt, p["final_g"], tb_mid)


def kernel(x_prompt, x_sample, cache_k, cache_v, cache_kidx, state_conv, attn_norm_g, w_in, conv_w, conv_b,
           conv_ln_g, conv_ln_b, w_out, ffn_norm_g, peer_wq, peer_subkeys, peer_u, peer_v, final_norm_g):
    assert w_in.shape[0] == 1, "single layer"
    nbp, s, _ = x_prompt.shape
    nb, tn, _ = x_sample.shape
    assert nbp == 1
    past = cache_k.shape[2]

    wi = w_in[0]
    o_qi = 3 * ATT_WIDTH
    o_ki = o_qi + N_IDX_HEADS * IDX_DIM
    o_wi = o_ki + IDX_DIM
    o_glu = o_wi + N_IDX_HEADS
    wqkv = wi[:, :o_qi].astype(BF16)
    widx = jnp.concatenate([wi[:, o_qi:o_glu], jnp.zeros((D_MODEL, 128 - IDX_DIM - N_IDX_HEADS), F32)],
                           axis=1).astype(BF16)
    wglu = wi[:, o_glu:].astype(BF16)
    g_attn = attn_norm_g[0][None, :]
    cw = jnp.concatenate([conv_w[0], jnp.zeros((CONV_TAIL - CONV_WIDTH, CONV_CH), F32)], axis=0)
    cb, cg, cbeta = conv_b[0][None, :], conv_ln_g[0][None, :], conv_ln_b[0][None, :]
    eye = jnp.eye(PEER_HEADS, dtype=F32)
    sk = peer_subkeys[0]
    a_s = [jnp.einsum("id,hg->ihgd", sk[c], eye).reshape(PEER_KEYS * PEER_HEADS, PEER_HEADS * PEER_HALF)
           .astype(BF16) for c in range(2)]
    a_d = [jnp.einsum("id,hg->higd", sk[c], eye).reshape(PEER_KEYS * PEER_HEADS, PEER_HEADS * PEER_HALF)
           .astype(BF16) for c in range(2)]
    wq = peer_wq[0].reshape(D_MODEL, PEER_HEADS, 2, PEER_HALF).transpose(2, 1, 3, 0)
    p = {
        "woa": w_out[0][:ATT_WIDTH].astype(BF16), "woc": w_out[0][ATT_WIDTH:].astype(BF16),
        "ffn_g": ffn_norm_g[0][None, :], "final_g": final_norm_g[None, :],
        "wqt": wq.reshape(D_MODEL, D_MODEL).astype(BF16),
        "a1s": a_s[0], "a2s": a_s[1], "a1d": a_d[0], "a2d": a_d[1],
        "u": peer_u[0].astype(BF16), "vt": peer_v[0].T.astype(BF16),
    }
    w_scale = (N_IDX_HEADS ** -0.5) * (IDX_DIM ** -0.5)

    xp = x_prompt[0]
    tb = _pick(s, 256)
    k, v, kw, glu, q_hm, k_hm, vt_hm, qi_hm = _in_proj(xp, g_attn, wqkv, widx, wglu, tb)
    conv_p, tail_p = _conv(glu, jnp.zeros((1, CONV_TAIL, CONV_CH), F32), cw, cb, cg, cbeta, 1, tb)
    ki = kw[:, :IDX_DIM]
    ki_b = ki.astype(BF16)
    w_t = (kw[:, IDX_DIM:IDX_DIM + N_IDX_HEADS] * w_scale).T
    tq = _pick(s, 256)
    sel = _idx_sel(qi_hm, w_t, ki_b, tq)
    att_p = _attn(q_hm, k_hm, vt_hm, sel, tq, _pick(s, 2048))
    y_p = _token_stage(xp, att_p, conv_p, p, tb, _pick(s, 512))

    xs = x_sample.reshape(nb * tn, D_MODEL)
    ts = nb * tn
    tbs = _pick(ts, 256)
    ks, vs, kws, glus, qs_hm, _, _, qis_hm = _in_proj(xs, g_attn, wqkv, widx, wglu, tbs)
    pad = jnp.zeros((nb, CONV_TAIL - (CONV_WIDTH - 1), CONV_CH), F32)
    conv_s, tail_s = _conv(glus, jnp.concatenate([pad, state_conv[0]], axis=1), cw, cb, cg, cbeta, nb, tn)
    hm4 = lambda a, nh, dt=BF16: a.reshape(nb, tn, nh, a.shape[1] // nh).transpose(0, 2, 1, 3).astype(dt)
    per_stream = lambda a: a.reshape(a.shape[0], nb, tn, a.shape[2]).transpose(1, 0, 2, 3)
    kis = kws[:, :IDX_DIM]
    att_s = _sample_attn(
        per_stream(qs_hm), per_stream(qis_hm),
        (kws[:, IDX_DIM:IDX_DIM + N_IDX_HEADS] * w_scale).reshape(nb, tn, N_IDX_HEADS),
        cache_kidx[0], kis.reshape(nb, tn, IDX_DIM),
        cache_k[0].reshape(nb, past, ATT_WIDTH), cache_v[0].reshape(nb, past, ATT_WIDTH),
        hm4(ks, N_HEADS), hm4(vs, N_HEADS))
    att_s = att_s.transpose(0, 2, 1, 3).reshape(ts, ATT_WIDTH)
    y_s = _token_stage(xs, att_s, conv_s, p, tbs, tbs)

    hd = (N_HEADS, HEAD_DIM)
    keep = CONV_TAIL - (CONV_WIDTH - 1)
    return (y_p[None], y_s.reshape(nb, tn, D_MODEL),
            k.reshape(1, 1, s, *hd), v.reshape(1, 1, s, *hd), ki[None, None], tail_p[None, :, keep:],
            ks.reshape(1, nb, tn, *hd), vs.reshape(1, nb, tn, *hd), kis.reshape(1, nb, tn, IDX_DIM),
            tail_s[None, :, keep:])
```

```python
import functools

import jax
import jax.numpy as jnp
from jax import lax
from jax.experimental import pallas as pl
from jax.experimental.pallas import tpu as pltpu

F32 = jnp.float32
BF16 = jnp.bfloat16
I32 = jnp.int32

D_MODEL = 1024
N_HEADS = 8
HEAD_DIM = 64
ATT_WIDTH = N_HEADS * HEAD_DIM
PV_ROWS = HEAD_DIM + 16
N_IDX_HEADS = 4
IDX_DIM = 64
TOPK = 256
CHUNK_SHIFT = 6
CONV_CH = 512
CONV_WIDTH = 31
CONV_TAIL = 32
PEER_HEADS = 8
PEER_KEYS = 128
PEER_HALF = 64
PEER_TOPK = 16
EPS = 1e-6
INT_MIN = -2147483648
ORDER_MASK = 0x7FFFFFFF
INV_SQRT2 = 0.7071067811865476
LOG2E = 1.4426950408889634
NEG_INF = float("-inf")
M_INIT = -1e30

VMEM_LIMIT = 56 * 1024 * 1024


def _params(sem, vmem=VMEM_LIMIT):
    return pltpu.CompilerParams(dimension_semantics=sem, vmem_limit_bytes=vmem)


def _nt(a, b):
    return lax.dot_general(a, b, (((1,), (1,)), ((), ())), preferred_element_type=F32)


def _dot(a, b):
    return jnp.dot(a, b, preferred_element_type=F32)


def _rms(x, g):
    return x * lax.rsqrt(jnp.mean(x * x, axis=-1, keepdims=True) + EPS) * g


def _order_key(x):
    b = pltpu.bitcast(x, I32)
    return b ^ ((b >> 31) & ORDER_MASK)


def _in_proj_body(x_ref, g_ref, wqkv_ref, widx_ref, wglu_ref,
                  k_ref, v_ref, kw_ref, glu_ref, qh_ref, kh_ref, vth_ref, qih_ref):
    xn = _rms(x_ref[...], g_ref[...]).astype(BF16)
    qkv = _dot(xn, wqkv_ref[...])
    q = qkv[:, :ATT_WIDTH] * (HEAD_DIM ** -0.5 * LOG2E)
    k = qkv[:, ATT_WIDTH:2 * ATT_WIDTH]
    v = qkv[:, 2 * ATT_WIDTH:]
    k_ref[...] = k
    v_ref[...] = v
    for h in range(N_HEADS):
        head = slice(h * HEAD_DIM, (h + 1) * HEAD_DIM)
        qh_ref[h] = q[:, head].astype(BF16)
        kh_ref[h] = k[:, head].astype(BF16)
    vth_ref[:, 0:HEAD_DIM, :] = v.T.reshape(N_HEADS, HEAD_DIM, v.shape[0]).astype(BF16)
    vth_ref[:, HEAD_DIM:, :] = jnp.ones((N_HEADS, PV_ROWS - HEAD_DIM, v.shape[0]), BF16)
    ix = _dot(xn, widx_ref[...])
    for h in range(N_IDX_HEADS):
        qih_ref[h] = ix[:, h * IDX_DIM:(h + 1) * IDX_DIM].astype(BF16)
    kw_ref[...] = ix[:, N_IDX_HEADS * IDX_DIM:]
    glu_ref[...] = _dot(xn, wglu_ref[...])


def _in_proj(x, g, wqkv, widx, wglu, tb):
    t = x.shape[0]
    row = lambda w: pl.BlockSpec((tb, w), lambda i: (i, 0))
    full = lambda a: pl.BlockSpec(a.shape, lambda i: (0, 0))
    heads = lambda n: pl.BlockSpec((n, tb, HEAD_DIM), lambda i: (0, i, 0))
    return pl.pallas_call(
        _in_proj_body,
        grid=(t // tb,),
        in_specs=[row(D_MODEL), full(g), full(wqkv), full(widx), full(wglu)],
        out_specs=[row(ATT_WIDTH), row(ATT_WIDTH), row(128), row(2 * CONV_CH),
                   heads(N_HEADS), heads(N_HEADS),
                   pl.BlockSpec((N_HEADS, PV_ROWS, tb), lambda i: (0, 0, i)), heads(N_IDX_HEADS)],
        out_shape=[jax.ShapeDtypeStruct((t, ATT_WIDTH), F32),
                   jax.ShapeDtypeStruct((t, ATT_WIDTH), F32),
                   jax.ShapeDtypeStruct((t, 128), F32),
                   jax.ShapeDtypeStruct((t, 2 * CONV_CH), F32),
                   jax.ShapeDtypeStruct((N_HEADS, t, HEAD_DIM), BF16),
                   jax.ShapeDtypeStruct((N_HEADS, t, HEAD_DIM), BF16),
                   jax.ShapeDtypeStruct((N_HEADS, PV_ROWS, t), BF16),
                   jax.ShapeDtypeStruct((N_IDX_HEADS, t, IDX_DIM), BF16)],
        compiler_params=_params(("parallel",)),
        name="in_proj",
    )(x, g, wqkv, widx, wglu)


def _conv_body(glu_ref, tail0_ref, w_ref, b_ref, g_ref, beta_ref, out_ref, tail_ref, ubuf, shifted, *, tb):
    j = pl.program_id(1)

    @pl.when(j == 0)
    def _():
        ubuf[0:CONV_TAIL, :] = tail0_ref[0]

    @pl.when(j > 0)
    def _():
        ubuf[0:CONV_TAIL, :] = ubuf[tb:tb + CONV_TAIL, :]

    glu = glu_ref[...]
    ubuf[CONV_TAIL:CONV_TAIL + tb, :] = glu[:, :CONV_CH] * jax.nn.sigmoid(glu[:, CONV_CH:])
    off = CONV_TAIL - (CONV_WIDTH - 1)
    acc = None
    for res in range(8):
        taps = [t for t in range(CONV_WIDTH) if (off + t) % 8 == res]
        lo, hi = (off + taps[0]) // 8, (off + taps[-1]) // 8
        n = 8 * (hi - lo) + tb
        shifted[0:n, :] = ubuf[res + 8 * lo:res + 8 * lo + n, :]
        for t in taps:
            a = 8 * ((off + t) // 8 - lo)
            term = shifted[a:a + tb, :] * w_ref[t:t + 1, :]
            acc = term if acc is None else acc + term
    y = acc + b_ref[...]
    mu = jnp.mean(y, axis=-1, keepdims=True)
    d = y - mu
    var = jnp.mean(d * d, axis=-1, keepdims=True)
    z = d * lax.rsqrt(var + EPS) * g_ref[...] + beta_ref[...]
    out_ref[...] = (z * jax.nn.sigmoid(z)).astype(BF16)
    tail_ref[0] = ubuf[tb:tb + CONV_TAIL, :]


def _conv(glu, tail0, w, b, g, beta, nseq, tb):
    t = glu.shape[0]
    nblk = t // (nseq * tb)
    full = lambda a: pl.BlockSpec(a.shape, lambda s, j: (0, 0))
    return pl.pallas_call(
        functools.partial(_conv_body, tb=tb),
        grid=(nseq, nblk),
        in_specs=[pl.BlockSpec((tb, 2 * CONV_CH), lambda s, j: (s * nblk + j, 0)),
                  pl.BlockSpec((1, CONV_TAIL, CONV_CH), lambda s, j: (s, 0, 0)),
                  full(w), full(b), full(g), full(beta)],
        out_specs=[pl.BlockSpec((tb, CONV_CH), lambda s, j: (s * nblk + j, 0)),
                   pl.BlockSpec((1, CONV_TAIL, CONV_CH), lambda s, j: (s, 0, 0))],
        out_shape=[jax.ShapeDtypeStruct((t, CONV_CH), BF16),
                   jax.ShapeDtypeStruct((nseq, CONV_TAIL, CONV_CH), F32)],
        scratch_shapes=[pltpu.VMEM((tb + CONV_TAIL, CONV_CH), F32)] * 2,
        compiler_params=_params(("parallel", "arbitrary")),
        name="conv",
    )(glu, tail0, w, b, g, beta)


def _idx_scores_t(ki_tile, qi_ref, w):
    acc = None
    for h in range(N_IDX_HEADS):
        term = w[h:h + 1, :] * jnp.maximum(_nt(ki_tile, qi_ref[h]), 0.0)
        acc = term if acc is None else acc + term
    return acc


def _chunk_limit(q0, tq):
    qpos = q0 + lax.broadcasted_iota(I32, (1, tq), 1)
    return ((qpos >> CHUNK_SHIFT) + 1) << CHUNK_SHIFT


def _bit_planes(words):
    x = list(words)
    mask, j = 0x0000FFFF, 16
    while j:
        k = 0
        while k < 32:
            t = (x[k] ^ lax.shift_right_logical(x[k + j], jnp.int32(j))) & mask
            x[k] = x[k] ^ t
            x[k + j] = x[k + j] ^ (t << j)
            k = (k + j + 1) & ~j
        j >>= 1
        mask ^= (mask << j) & 0xFFFFFFFF
    return x


def _sublane_prefix(x):
    sub = lax.broadcasted_iota(I32, x.shape, 0)
    for sh in (1, 2, 4):
        x = x + jnp.where(sub >= sh, pltpu.roll(x, sh, axis=0), 0)
    return x


def _sel_body(qi_ref, w_ref, ki_ref, sel_ref, planes_ref, cand_ref, gt_ref, *, tq, grp):
    i = pl.program_id(0)
    nt = i + 1
    ng = (nt + grp - 1) // grp
    w = w_ref[...]
    limit = _chunk_limit(i * tq, tq)
    rows = lax.broadcasted_iota(I32, (tq, tq), 0)
    assert tq == 8 * 32

    def fill(t, diagonal):
        r0 = pl.multiple_of(t * tq, tq)
        b = pltpu.bitcast(_idx_scores_t(ki_ref[pl.ds(r0, tq), :], qi_ref, w), I32)
        u = b ^ ((b >> 31) | INT_MIN)
        if diagonal:
            u = jnp.where(rows + r0 < limit, u, 0)
        planes = _bit_planes([u[8 * k:8 * k + 8, :] for k in range(32)])
        at = pl.ds(pl.multiple_of(t * 8, 8), 8)
        for p in range(32):
            planes_ref[p, at, :] = planes[p]
        cand_ref[at, :] = jnp.full((8, tq), -1, I32)
        gt_ref[at, :] = jnp.zeros((8, tq), I32)

    def fill_full(t, c):
        fill(t, False)
        return c

    lax.fori_loop(0, nt - 1, fill_full, 0)
    fill(nt - 1, True)

    def pad(t, c):
        at = pl.ds(pl.multiple_of(t * 8, 8), 8)
        planes_ref[:, at, :] = jnp.zeros((32, 8, tq), I32)
        cand_ref[at, :] = jnp.zeros((8, tq), I32)
        gt_ref[at, :] = jnp.zeros((8, tq), I32)
        return c

    lax.fori_loop(nt, ng * grp, pad, 0)

    def sweep(p_prev, flip, p_next):
        def group(g, acc):
            at = pl.ds(pl.multiple_of(g * (grp * 8), grp * 8), grp * 8)
            c = cand_ref[at, :]
            if p_prev is not None:
                prev = planes_ref[p_prev, at, :]
                gt_ref[at, :] = gt_ref[at, :] | (c & prev & flip)
                c = c & (prev ^ flip)
                cand_ref[at, :] = c
            if p_next is not None:
                hit = lax.population_count(c & planes_ref[p_next, at, :])
                acc = acc + jnp.sum(hit.reshape(grp, 8, tq), axis=0)
            return acc
        acc = lax.fori_loop(0, ng, group, jnp.zeros((8, tq), I32))
        return jnp.sum(acc, axis=0, keepdims=True)

    def decide(p, cnt, kth_u, need):
        one = cnt >= need
        kth_u = jnp.where(one, kth_u | lax.shift_right_logical(jnp.int32(INT_MIN), p), kth_u)
        return kth_u, jnp.where(one, need, need - cnt), jnp.where(one, 0, -1)

    zero = jnp.zeros((1, tq), I32)
    kth_u, need, flip = decide(jnp.int32(0), sweep(None, None, 0), zero, jnp.full((1, tq), TOPK, I32))

    def bit_step(p, carry):
        kth_u, need, flip = carry
        return decide(p, sweep(p - 1, flip, p), kth_u, need)

    kth_u, need, flip = lax.fori_loop(1, 32, bit_step, (kth_u, need, flip))
    sweep(31, flip, None)
    need = jnp.where(kth_u == 0, 0, need)

    def tie_count(t):
        c = cand_ref[pl.ds(pl.multiple_of(t * 8, 8), 8), :]
        return c, jnp.sum(lax.population_count(c), axis=0, keepdims=True)

    def find(t, carry):
        seen, part, words = carry
        c, n = tie_count(t)
        hit = (seen < need) & (seen + n > need)
        return seen + n, jnp.where(hit, need - seen, part), jnp.where(hit, c, words)

    _, part, words = lax.fori_loop(0, nt, find, (zero, zero, jnp.zeros((8, tq), I32)))
    keep = jnp.zeros((8, tq), I32)
    before = zero
    for k in range(32):
        bit = lax.shift_right_logical(words, jnp.int32(31 - k)) & 1
        rank = before + _sublane_prefix(bit)
        keep = keep | jnp.where((bit == 1) & (rank <= part), INT_MIN if k == 0 else 1 << (31 - k), 0)
        before = rank[7:8, :]

    def emit(t, seen):
        c, n = tie_count(t)
        at = pl.ds(pl.multiple_of(t * 8, 8), 8)
        whole = seen + n <= need
        partial = (seen < need) & jnp.logical_not(whole)
        sel_ref[0, at, :] = gt_ref[at, :] | jnp.where(whole, c, jnp.where(partial, keep, 0))
        return seen + n

    lax.fori_loop(0, nt, emit, zero)

    def clear(t, c):
        sel_ref[0, pl.ds(pl.multiple_of(t * 8, 8), 8), :] = jnp.zeros((8, tq), I32)
        return c

    lax.fori_loop(nt, sel_ref.shape[1] // 8, clear, 0)


def _idx_sel(qi_hm, w_t, ki, tq):
    s = ki.shape[0]
    nrow = s // tq * 8
    return pl.pallas_call(
        functools.partial(_sel_body, tq=tq, grp=_pick(s // tq, 16)),
        grid=(s // tq,),
        in_specs=[pl.BlockSpec((N_IDX_HEADS, tq, IDX_DIM), lambda i: (0, i, 0)),
                  pl.BlockSpec((N_IDX_HEADS, tq), lambda i: (0, i)),
                  pl.BlockSpec((s, IDX_DIM), lambda i: (0, 0))],
        out_specs=pl.BlockSpec((1, nrow, tq), lambda i: (i, 0, 0)),
        out_shape=jax.ShapeDtypeStruct((s // tq, nrow, tq), I32),
        scratch_shapes=[pltpu.VMEM((32, nrow, tq), I32), pltpu.VMEM((nrow, tq), I32),
                        pltpu.VMEM((nrow, tq), I32)],
        compiler_params=_params(("arbitrary",)),
        name="idx_sel",
    )(qi_hm, w_t, ki)


def _attn_body(qb_ref, kb_ref, q_ref, k_ref, vt_ref, sel_ref,
               out_ref, m_ref, acc_ref, s_ref, p_ref, bias_ref, *, tq, tk):
    step = pl.program_id(0)
    i = qb_ref[step]
    j = kb_ref[step]

    @pl.when(j == 0)
    def _():
        m_ref[...] = jnp.full(m_ref.shape, M_INIT, F32)
        acc_ref[...] = jnp.zeros(acc_ref.shape, F32)

    for tile in range(tk // tq):
        words = sel_ref[0, tile * 8:(tile + 1) * 8, :]
        for k in range(32):
            bias_ref[pl.ds(tile * tq + 8 * k, 8), :] = jnp.where((words << k) < 0, 0.0, NEG_INF)
    bias = bias_ref[...]
    tops = []
    for h in range(N_HEADS):
        s = _nt(k_ref[h], q_ref[h]) + bias
        s_ref[h] = s
        tops.append(jnp.max(s, axis=0, keepdims=True))
    m_old = m_ref[...]
    m_new = jnp.maximum(m_old, jnp.concatenate(tops, axis=0))
    alpha = jnp.exp2(m_old - m_new)
    for h in range(N_HEADS):
        p_ref[h] = jnp.exp2(s_ref[h] - m_new[h:h + 1, :]).astype(BF16)
    m_ref[...] = m_new
    for h in range(N_HEADS):
        acc_ref[h] = alpha[h:h + 1, :] * acc_ref[h] + _dot(vt_ref[h], p_ref[h])

    @pl.when(j == ((i + 1) * tq - 1) // tk)
    def _():
        o = jnp.concatenate([acc_ref[h, 0:HEAD_DIM, :] / acc_ref[h, HEAD_DIM:HEAD_DIM + 1, :]
                             for h in range(N_HEADS)], axis=0)
        out_ref[...] = o.T.astype(BF16)


def _attn(q_hm, k_hm, vt_hm, sel, tq, tk):
    s = q_hm.shape[1]
    assert tk % tq == 0
    qb, kb = [], []
    for i in range(s // tq):
        for j in range(((i + 1) * tq - 1) // tk + 1):
            qb.append(i)
            kb.append(j)
    qb = jnp.asarray(qb, I32)
    kb = jnp.asarray(kb, I32)
    grid_spec = pltpu.PrefetchScalarGridSpec(
        num_scalar_prefetch=2,
        grid=(int(qb.shape[0]),),
        in_specs=[pl.BlockSpec((N_HEADS, tq, HEAD_DIM), lambda t, qb, kb: (0, qb[t], 0)),
                  pl.BlockSpec((N_HEADS, tk, HEAD_DIM), lambda t, qb, kb: (0, kb[t], 0)),
                  pl.BlockSpec((N_HEADS, PV_ROWS, tk), lambda t, qb, kb: (0, 0, kb[t])),
                  pl.BlockSpec((1, tk // tq * 8, tq), lambda t, qb, kb: (qb[t], kb[t], 0))],
        out_specs=pl.BlockSpec((tq, ATT_WIDTH), lambda t, qb, kb: (qb[t], 0)),
        scratch_shapes=[pltpu.VMEM((N_HEADS, tq), F32),
                        pltpu.VMEM((N_HEADS, PV_ROWS, tq), F32),
                        pltpu.VMEM((N_HEADS, tk, tq), F32),
                        pltpu.VMEM((N_HEADS, tk, tq), BF16),
                        pltpu.VMEM((tk, tq), F32)],
    )
    return pl.pallas_call(
        functools.partial(_attn_body, tq=tq, tk=tk),
        grid_spec=grid_spec,
        out_shape=jax.ShapeDtypeStruct((s, ATT_WIDTH), BF16),
        compiler_params=_params(("arbitrary",)),
        name="attn",
    )(qb, kb, q_hm, k_hm, vt_hm, sel)


NEW_PAD = 128


def _sample_attn_body(q_ref, qi_ref, w_ref, kic_ref, kin_ref, kc_ref, vc_ref, kn_ref, vn_ref,
                      out_ref, kinp, knp, vnp, *, past, tn):
    kinp[...] = jnp.zeros(kinp.shape, BF16)
    kinp[0:tn, :] = kin_ref[0].astype(BF16)
    knp[...] = jnp.zeros(knp.shape, BF16)
    vnp[...] = jnp.zeros(vnp.shape, BF16)
    for h in range(N_HEADS):
        knp[h, 0:tn, :] = kn_ref[0, h]
        vnp[h, 0:tn, :] = vn_ref[0, h]

    w = w_ref[0]
    kic = kic_ref[0].astype(BF16)
    kin = kinp[...]
    sc = None
    sn = None
    for h in range(N_IDX_HEADS):
        qih = qi_ref[0, h]
        wc = w[:, h:h + 1]
        tc = wc * jnp.maximum(_nt(qih, kic), 0.0)
        tnw = wc * jnp.maximum(_nt(qih, kin), 0.0)
        sc = tc if sc is None else sc + tc
        sn = tnw if sn is None else sn + tnw
    new_ok = lax.broadcasted_iota(I32, (tn, NEW_PAD), 1) < tn
    keyc = _order_key(sc)
    keyn = jnp.where(new_ok, _order_key(sn), INT_MIN)

    def count(pc, pn):
        return (jnp.sum(jnp.where(pc, 1, 0), axis=1, keepdims=True)
                + jnp.sum(jnp.where(pn, 1, 0), axis=1, keepdims=True))

    def bit_step(p, u):
        cand_u = u | (jnp.int32(1) << (31 - p))
        cand = cand_u ^ INT_MIN
        return jnp.where(count(keyc >= cand, keyn >= cand) >= TOPK, cand_u, u)

    kth = lax.fori_loop(0, 32, bit_step, jnp.zeros((tn, 1), I32)) ^ INT_MIN
    need = (TOPK - count(keyc > kth, keyn > kth)).astype(F32)

    cw = 256
    tri = jnp.where(lax.broadcasted_iota(I32, (cw, cw), 0) <= lax.broadcasted_iota(I32, (cw, cw), 1),
                    1.0, 0.0).astype(BF16)
    seen = jnp.zeros((tn, 1), F32)
    bias_c = []
    for c in range(past // cw):
        kc_ = keyc[:, c * cw:(c + 1) * cw]
        eq = kc_ == kth
        pre = _dot(jnp.where(eq, 1.0, 0.0).astype(BF16), tri)
        sel = (kc_ > kth) | (eq & (pre + seen <= need))
        bias_c.append(jnp.where(sel, 0.0, NEG_INF))
        seen = seen + pre[:, cw - 1:cw]
    bias_c = jnp.concatenate(bias_c, axis=1)
    eqn = keyn == kth
    pren = _dot(jnp.where(eqn, 1.0, 0.0).astype(BF16), tri[:NEW_PAD, :NEW_PAD])
    bias_n = jnp.where(new_ok & ((keyn > kth) | (eqn & (pren + seen <= need))), 0.0, NEG_INF)

    kc = kc_ref[0].astype(BF16)
    vc = vc_ref[0].astype(BF16)
    for h in range(N_HEADS):
        qh = q_ref[0, h]
        head = slice(h * HEAD_DIM, (h + 1) * HEAD_DIM)
        s_c = _nt(qh, kc[:, head]) + bias_c
        s_n = _nt(qh, knp[h]) + bias_n
        m = jnp.maximum(jnp.max(s_c, axis=1, keepdims=True), jnp.max(s_n, axis=1, keepdims=True))
        p_c = jnp.exp2(s_c - m)
        p_n = jnp.exp2(s_n - m)
        l = jnp.sum(p_c, axis=1, keepdims=True) + jnp.sum(p_n, axis=1, keepdims=True)
        o = _dot(p_c.astype(BF16), vc[:, head]) + _dot(p_n.astype(BF16), vnp[h])
        out_ref[0, h] = (o / l).astype(BF16)


def _sample_attn(q_hm, qi_hm, w, kic, kin, kc, vc, kn_hm, vn_hm):
    nb, _, tn, _ = q_hm.shape
    past = kic.shape[1]
    b4 = lambda a: pl.BlockSpec((1,) + a.shape[1:], lambda b: (b, 0, 0, 0))
    b3 = lambda a: pl.BlockSpec((1,) + a.shape[1:], lambda b: (b, 0, 0))
    return pl.pallas_call(
        functools.partial(_sample_attn_body, past=past, tn=tn),
        grid=(nb,),
        in_specs=[b4(q_hm), b4(qi_hm), b3(w), b3(kic), b3(kin), b3(kc), b3(vc), b4(kn_hm), b4(vn_hm)],
        out_specs=pl.BlockSpec((1, N_HEADS, tn, HEAD_DIM), lambda b: (b, 0, 0, 0)),
        out_shape=jax.ShapeDtypeStruct((nb, N_HEADS, tn, HEAD_DIM), BF16),
        scratch_shapes=[pltpu.VMEM((NEW_PAD, IDX_DIM), BF16),
                        pltpu.VMEM((N_HEADS, NEW_PAD, HEAD_DIM), BF16),
                        pltpu.VMEM((N_HEADS, NEW_PAD, HEAD_DIM), BF16)],
        compiler_params=_params(("parallel",)),
        name="sample_attn",
    )(q_hm, qi_hm, w, kic, kin, kc, vc, kn_hm, vn_hm)


def _cmpx(a, b):
    return jnp.maximum(a, b), jnp.minimum(a, b)


def _bitonic_merge_desc(v):
    n = len(v)
    v = list(v)
    j = n // 2
    while j >= 1:
        for i in range(n):
            l = i ^ j
            if l > i:
                v[i], v[l] = _cmpx(v[i], v[l])
        j //= 2
    return v


def _bitonic_sort_desc(v):
    n = len(v)
    v = list(v)
    k = 2
    while k <= n:
        j = k // 2
        while j >= 1:
            for i in range(n):
                l = i ^ j
                if l > i:
                    hi, lo = _cmpx(v[i], v[l])
                    v[i], v[l] = (hi, lo) if (i & k) == 0 else (lo, hi)
            j //= 2
        k *= 2
    return v


def _merge_top(a, b):
    n = len(a)
    return _bitonic_merge_desc([jnp.maximum(a[k], b[n - 1 - k]) for k in range(n)])


def _top16_desc(vals):
    groups = [_bitonic_sort_desc(vals[g:g + PEER_TOPK]) for g in range(0, len(vals), PEER_TOPK)]
    while len(groups) > 1:
        groups = [_merge_top(groups[g], groups[g + 1]) for g in range(0, len(groups), 2)]
    return groups[0]


def _mid_body(x_ref, att_ref, conv_ref, woa_ref, woc_ref, g_ref, wqt_ref, a1_ref, a2_ref,
              h_ref, xn_ref, qht_ref, st_ref):
    h = x_ref[...] + _dot(att_ref[...], woa_ref[...]) + _dot(conv_ref[...], woc_ref[...])
    h_ref[...] = h
    xn = _rms(h, g_ref[...]).astype(BF16)
    xn_ref[...] = xn
    qht = _nt(wqt_ref[...], xn).astype(BF16)
    qht_ref[...] = qht
    half = PEER_HEADS * PEER_HALF
    s1 = _dot(a1_ref[...], qht[:half])
    s2 = _dot(a2_ref[...], qht[half:])
    rows = lambda s: [s[r * PEER_HEADS:(r + 1) * PEER_HEADS, :] for r in range(PEER_KEYS)]
    v1 = _top16_desc(rows(s1))
    v2 = _top16_desc(rows(s2))
    ninf = jnp.full(v1[0].shape, NEG_INF, F32)
    top = [v1[0] + v2[b] for b in range(PEER_TOPK)]
    for a in range(1, PEER_TOPK):
        n_a = PEER_TOPK // (a + 1)
        top = _merge_top(top, [v1[a] + v2[b] if b < n_a else ninf for b in range(PEER_TOPK)])
    z = jnp.ones_like(top[0])
    for k in range(1, PEER_TOPK):
        z = z + jnp.exp(top[k] - top[0])
    st_ref[0:8, :] = v1[0]
    st_ref[8:16, :] = v2[0]
    st_ref[16:24, :] = top[PEER_TOPK - 1]
    st_ref[24:32, :] = 1.0 / z


def _mid(x, att, conv, woa, woc, g, wqt, a1, a2, tb):
    t = x.shape[0]
    row = lambda w: pl.BlockSpec((tb, w), lambda i: (i, 0))
    col = lambda r: pl.BlockSpec((r, tb), lambda i: (0, i))
    full = lambda a: pl.BlockSpec(a.shape, lambda i: (0, 0))
    return pl.pallas_call(
        _mid_body,
        grid=(t // tb,),
        in_specs=[row(D_MODEL), row(ATT_WIDTH), row(CONV_CH), full(woa), full(woc), full(g),
                  full(wqt), full(a1), full(a2)],
        out_specs=[row(D_MODEL), row(D_MODEL), col(D_MODEL), col(32)],
        out_shape=[jax.ShapeDtypeStruct((t, D_MODEL), F32),
                   jax.ShapeDtypeStruct((t, D_MODEL), BF16),
                   jax.ShapeDtypeStruct((D_MODEL, t), BF16),
                   jax.ShapeDtypeStruct((32, t), F32)],
        compiler_params=_params(("parallel",)),
        name="mid",
    )(x, att, conv, woa, woc, g, wqt, a1, a2)


def _peer_body(xn_ref, qht_ref, st_ref, a1_ref, a2_ref, u_ref, vt_ref, h_ref, g_ref, y_ref,
               s1_ref, s2_ref, e1_ref, e2_ref, hid_ref, out_ref, *, eb):
    e = pl.program_id(1)
    nsub = eb // PEER_KEYS
    half = PEER_HEADS * PEER_HALF

    @pl.when(e == 0)
    def _():
        qht = qht_ref[...]
        s1 = _dot(a1_ref[...], qht[:half])
        s2 = _dot(a2_ref[...], qht[half:])
        s1_ref[...] = s1
        s2_ref[...] = s2
        for h in range(PEER_HEADS):
            r = slice(h * PEER_KEYS, (h + 1) * PEER_KEYS)
            e1_ref[r, :] = jnp.exp(s1[r] - st_ref[h:h + 1, :])
            e2_ref[r, :] = jnp.exp(s2[r] - st_ref[8 + h:9 + h, :]) * st_ref[24 + h:25 + h, :] * 0.5
        out_ref[...] = jnp.zeros(out_ref.shape, F32)

    xn = xn_ref[...]
    for ii in range(nsub):
        i = e * nsub + ii
        a = _nt(u_ref[ii * PEER_KEYS:(ii + 1) * PEER_KEYS, :], xn)
        gate = None
        for h in range(PEER_HEADS):
            r = slice(h * PEER_KEYS, (h + 1) * PEER_KEYS)
            pair = s1_ref[pl.ds(h * PEER_KEYS + i, 1), :] + s2_ref[r, :]
            g = e1_ref[pl.ds(h * PEER_KEYS + i, 1), :] * e2_ref[r, :]
            g = jnp.where(pair >= st_ref[16 + h:17 + h, :], g, 0.0)
            gate = g if gate is None else gate + g
        hid = a * (1.0 + lax.erf(a * INV_SQRT2)) * gate
        hid_ref[ii * PEER_KEYS:(ii + 1) * PEER_KEYS, :] = hid.astype(BF16)
    out_ref[...] += _dot(vt_ref[...], hid_ref[...])

    @pl.when(e == pl.num_programs(1) - 1)
    def _():
        y_ref[...] = _rms(h_ref[...] + out_ref[...].T, g_ref[...])


def _peer(xn, qht, st, a1, a2, u, vt, h, g, tb, eb):
    t = xn.shape[0]
    ne = u.shape[0] // eb
    full = lambda a: pl.BlockSpec(a.shape, lambda i, e: (0, 0))
    return pl.pallas_call(
        functools.partial(_peer_body, eb=eb),
        grid=(t // tb, ne),
        in_specs=[pl.BlockSpec((tb, D_MODEL), lambda i, e: (i, 0)),
                  pl.BlockSpec((D_MODEL, tb), lambda i, e: (0, i)),
                  pl.BlockSpec((32, tb), lambda i, e: (0, i)),
                  full(a1), full(a2),
                  pl.BlockSpec((eb, D_MODEL), lambda i, e: (e, 0)),
                  pl.BlockSpec((D_MODEL, eb), lambda i, e: (0, e)),
                  pl.BlockSpec((tb, D_MODEL), lambda i, e: (i, 0)),
                  full(g)],
        out_specs=pl.BlockSpec((tb, D_MODEL), lambda i, e: (i, 0)),
        out_shape=jax.ShapeDtypeStruct((t, D_MODEL), F32),
        scratch_shapes=[pltpu.VMEM((PEER_HEADS * PEER_KEYS, tb), F32)] * 4
                       + [pltpu.VMEM((eb, tb), BF16), pltpu.VMEM((D_MODEL, tb), F32)],
        compiler_params=_params(("parallel", "arbitrary")),
        name="peer",
    )(xn, qht, st, a1, a2, u, vt, h, g)


def _pick(n, pref):
    b = min(n, pref)
    while n % b:
        b //= 2
    return b


def _token_stage(x, att, conv, p, tb_mid, tb_peer):
    h, xn, qht, st = _mid(x, att, conv, p["woa"], p["woc"], p["ffn_g"], p["wqt"], p["a1s"], p["a2s"], tb_mid)
    return _peer(xn, qht, st, p["a1d"], p["a2d"], p["u"], p["vt"], h, p["final_g"], tb_peer, 2048)


def kernel(x_prompt, x_sample, cache_k, cache_v, cache_kidx, state_conv, attn_norm_g, w_in, conv_w, conv_b,
           conv_ln_g, conv_ln_b, w_out, ffn_norm_g, peer_wq, peer_subkeys, peer_u, peer_v, final_norm_g):
    assert w_in.shape[0] == 1, "single layer"
    nbp, s, _ = x_prompt.shape
    nb, tn, _ = x_sample.shape
    assert nbp == 1
    past = cache_k.shape[2]

    wi = w_in[0]
    o_qi = 3 * ATT_WIDTH
    o_ki = o_qi + N_IDX_HEADS * IDX_DIM
    o_wi = o_ki + IDX_DIM
    o_glu = o_wi + N_IDX_HEADS
    wqkv = wi[:, :o_qi].astype(BF16)
    widx = jnp.concatenate([wi[:, o_qi:o_glu], jnp.zeros((D_MODEL, 128 - IDX_DIM - N_IDX_HEADS), F32)],
                           axis=1).astype(BF16)
    wglu = wi[:, o_glu:].astype(BF16)
    g_attn = attn_norm_g[0][None, :]
    cw = jnp.concatenate([conv_w[0], jnp.zeros((CONV_TAIL - CONV_WIDTH, CONV_CH), F32)], axis=0)
    cb, cg, cbeta = conv_b[0][None, :], conv_ln_g[0][None, :], conv_ln_b[0][None, :]
    eye = jnp.eye(PEER_HEADS, dtype=F32)
    sk = peer_subkeys[0]
    a_s = [jnp.einsum("id,hg->ihgd", sk[c], eye).reshape(PEER_KEYS * PEER_HEADS, PEER_HEADS * PEER_HALF)
           .astype(BF16) for c in range(2)]
    a_d = [jnp.einsum("id,hg->higd", sk[c], eye).reshape(PEER_KEYS * PEER_HEADS, PEER_HEADS * PEER_HALF)
           .astype(BF16) for c in range(2)]
    wq = peer_wq[0].reshape(D_MODEL, PEER_HEADS, 2, PEER_HALF).transpose(2, 1, 3, 0)
    p = {
        "woa": w_out[0][:ATT_WIDTH].astype(BF16), "woc": w_out[0][ATT_WIDTH:].astype(BF16),
        "ffn_g": ffn_norm_g[0][None, :], "final_g": final_norm_g[None, :],
        "wqt": wq.reshape(D_MODEL, D_MODEL).astype(BF16),
        "a1s": a_s[0], "a2s": a_s[1], "a1d": a_d[0], "a2d": a_d[1],
        "u": peer_u[0].astype(BF16), "vt": peer_v[0].T.astype(BF16),
    }
    w_scale = (N_IDX_HEADS ** -0.5) * (IDX_DIM ** -0.5)

    xp = x_prompt[0]
    tb = _pick(s, 256)
    k, v, kw, glu, q_hm, k_hm, vt_hm, qi_hm = _in_proj(xp, g_attn, wqkv, widx, wglu, tb)
    conv_p, tail_p = _conv(glu, jnp.zeros((1, CONV_TAIL, CONV_CH), F32), cw, cb, cg, cbeta, 1, tb)
    ki = kw[:, :IDX_DIM]
    ki_b = ki.astype(BF16)
    w_t = (kw[:, IDX_DIM:IDX_DIM + N_IDX_HEADS] * w_scale).T
    tq = _pick(s, 256)
    sel = _idx_sel(qi_hm, w_t, ki_b, tq)
    att_p = _attn(q_hm, k_hm, vt_hm, sel, tq, _pick(s, 1024))
    y_p = _token_stage(xp, att_p, conv_p, p, tb, _pick(s, 512))

    xs = x_sample.reshape(nb * tn, D_MODEL)
    ts = nb * tn
    tbs = _pick(ts, 256)
    ks, vs, kws, glus, qs_hm, _, _, qis_hm = _in_proj(xs, g_attn, wqkv, widx, wglu, tbs)
    pad = jnp.zeros((nb, CONV_TAIL - (CONV_WIDTH - 1), CONV_CH), F32)
    conv_s, tail_s = _conv(glus, jnp.concatenate([pad, state_conv[0]], axis=1), cw, cb, cg, cbeta, nb, tn)
    hm4 = lambda a, nh, dt=BF16: a.reshape(nb, tn, nh, a.shape[1] // nh).transpose(0, 2, 1, 3).astype(dt)
    per_stream = lambda a: a.reshape(a.shape[0], nb, tn, a.shape[2]).transpose(1, 0, 2, 3)
    kis = kws[:, :IDX_DIM]
    att_s = _sample_attn(
        per_stream(qs_hm), per_stream(qis_hm),
        (kws[:, IDX_DIM:IDX_DIM + N_IDX_HEADS] * w_scale).reshape(nb, tn, N_IDX_HEADS),
        cache_kidx[0], kis.reshape(nb, tn, IDX_DIM),
        cache_k[0].reshape(nb, past, ATT_WIDTH), cache_v[0].reshape(nb, past, ATT_WIDTH),
        hm4(ks, N_HEADS), hm4(vs, N_HEADS))
    att_s = att_s.transpose(0, 2, 1, 3).reshape(ts, ATT_WIDTH)
    y_s = _token_stage(xs, att_s, conv_s, p, tbs, tbs)

    hd = (N_HEADS, HEAD_DIM)
    keep = CONV_TAIL - (CONV_WIDTH - 1)
    return (y_p[None], y_s.reshape(nb, tn, D_MODEL),
            k.reshape(1, 1, s, *hd), v.reshape(1, 1, s, *hd), ki[None, None], tail_p[None, :, keep:],
            ks.reshape(1, nb, tn, *hd), vs.reshape(1, nb, tn, *hd), kis.reshape(1, nb, tn, IDX_DIM),
            tail_s[None, :, keep:])
```

```python
import functools

import jax
import jax.numpy as jnp
from jax import lax
from jax.experimental import pallas as pl
from jax.experimental.pallas import tpu as pltpu

F32 = jnp.float32
BF16 = jnp.bfloat16
I32 = jnp.int32

D_MODEL = 1024
N_HEADS = 8
HEAD_DIM = 64
ATT_WIDTH = N_HEADS * HEAD_DIM
PV_ROWS = HEAD_DIM + 16
N_IDX_HEADS = 4
IDX_DIM = 64
TOPK = 256
CHUNK_SHIFT = 6
CONV_CH = 512
CONV_WIDTH = 31
CONV_TAIL = 32
PEER_HEADS = 8
PEER_KEYS = 128
PEER_HALF = 64
PEER_TOPK = 16
EPS = 1e-6
INT_MIN = -2147483648
ORDER_MASK = 0x7FFFFFFF
INV_SQRT2 = 0.7071067811865476
LOG2E = 1.4426950408889634
NEG_INF = float("-inf")
M_INIT = -1e30

VMEM_LIMIT = 56 * 1024 * 1024


def _params(sem, vmem=VMEM_LIMIT):
    return pltpu.CompilerParams(dimension_semantics=sem, vmem_limit_bytes=vmem)


def _nt(a, b):
    return lax.dot_general(a, b, (((1,), (1,)), ((), ())), preferred_element_type=F32)


def _dot(a, b):
    return jnp.dot(a, b, preferred_element_type=F32)


def _rms(x, g):
    return x * lax.rsqrt(jnp.mean(x * x, axis=-1, keepdims=True) + EPS) * g


def _order_key(x):
    b = pltpu.bitcast(x, I32)
    return b ^ ((b >> 31) & ORDER_MASK)


def _in_proj_body(x_ref, g_ref, wqkv_ref, widx_ref, wglu_ref,
                  k_ref, v_ref, kw_ref, glu_ref, qh_ref, kh_ref, vth_ref, qih_ref):
    glu_ref[...] = _project(x_ref, g_ref, wqkv_ref, widx_ref, wglu_ref,
                            k_ref, v_ref, kw_ref, qh_ref, kh_ref, vth_ref, qih_ref)


def _project(x_ref, g_ref, wqkv_ref, widx_ref, wglu_ref, k_ref, v_ref, kw_ref, qh_ref, kh_ref, vth_ref, qih_ref):
    xn = _rms(x_ref[...], g_ref[...]).astype(BF16)
    qkv = _dot(xn, wqkv_ref[...])
    q = qkv[:, :ATT_WIDTH] * (HEAD_DIM ** -0.5 * LOG2E)
    k = qkv[:, ATT_WIDTH:2 * ATT_WIDTH]
    v = qkv[:, 2 * ATT_WIDTH:]
    k_ref[...] = k
    v_ref[...] = v
    for h in range(N_HEADS):
        head = slice(h * HEAD_DIM, (h + 1) * HEAD_DIM)
        qh_ref[h] = q[:, head].astype(BF16)
        kh_ref[h] = k[:, head].astype(BF16)
    vth_ref[:, 0:HEAD_DIM, :] = v.T.reshape(N_HEADS, HEAD_DIM, v.shape[0]).astype(BF16)
    vth_ref[:, HEAD_DIM:, :] = jnp.ones((N_HEADS, PV_ROWS - HEAD_DIM, v.shape[0]), BF16)
    ix = _dot(xn, widx_ref[...])
    for h in range(N_IDX_HEADS):
        qih_ref[h] = ix[:, h * IDX_DIM:(h + 1) * IDX_DIM].astype(BF16)
    kw_ref[...] = ix[:, N_IDX_HEADS * IDX_DIM:]
    return _dot(xn, wglu_ref[...])


def _in_proj(x, g, wqkv, widx, wglu, tb):
    t = x.shape[0]
    row = lambda w: pl.BlockSpec((tb, w), lambda i: (i, 0))
    full = lambda a: pl.BlockSpec(a.shape, lambda i: (0, 0))
    heads = lambda n: pl.BlockSpec((n, tb, HEAD_DIM), lambda i: (0, i, 0))
    return pl.pallas_call(
        _in_proj_body,
        grid=(t // tb,),
        in_specs=[row(D_MODEL), full(g), full(wqkv), full(widx), full(wglu)],
        out_specs=[row(ATT_WIDTH), row(ATT_WIDTH), row(128), row(2 * CONV_CH),
                   heads(N_HEADS), heads(N_HEADS),
                   pl.BlockSpec((N_HEADS, PV_ROWS, tb), lambda i: (0, 0, i)), heads(N_IDX_HEADS)],
        out_shape=[jax.ShapeDtypeStruct((t, ATT_WIDTH), F32),
                   jax.ShapeDtypeStruct((t, ATT_WIDTH), F32),
                   jax.ShapeDtypeStruct((t, 128), F32),
                   jax.ShapeDtypeStruct((t, 2 * CONV_CH), F32),
                   jax.ShapeDtypeStruct((N_HEADS, t, HEAD_DIM), BF16),
                   jax.ShapeDtypeStruct((N_HEADS, t, HEAD_DIM), BF16),
                   jax.ShapeDtypeStruct((N_HEADS, PV_ROWS, t), BF16),
                   jax.ShapeDtypeStruct((N_IDX_HEADS, t, IDX_DIM), BF16)],
        compiler_params=_params(("parallel",)),
        name="in_proj",
    )(x, g, wqkv, widx, wglu)


def _conv_body(glu_ref, tail0_ref, w_ref, b_ref, g_ref, beta_ref, out_ref, tail_ref, ubuf, shifted, *, tb):
    _conv_core(glu_ref[...], pl.program_id(1), tail0_ref, w_ref, b_ref, g_ref, beta_ref,
               out_ref, tail_ref, ubuf, shifted, tb)


def _conv_core(glu, j, tail0_ref, w_ref, b_ref, g_ref, beta_ref, out_ref, tail_ref, ubuf, shifted, tb):
    @pl.when(j == 0)
    def _():
        ubuf[0:CONV_TAIL, :] = tail0_ref[0]

    @pl.when(j > 0)
    def _():
        ubuf[0:CONV_TAIL, :] = ubuf[tb:tb + CONV_TAIL, :]

    ubuf[CONV_TAIL:CONV_TAIL + tb, :] = glu[:, :CONV_CH] * jax.nn.sigmoid(glu[:, CONV_CH:])
    off = CONV_TAIL - (CONV_WIDTH - 1)
    acc = None
    for res in range(8):
        taps = [t for t in range(CONV_WIDTH) if (off + t) % 8 == res]
        lo, hi = (off + taps[0]) // 8, (off + taps[-1]) // 8
        n = 8 * (hi - lo) + tb
        shifted[0:n, :] = ubuf[res + 8 * lo:res + 8 * lo + n, :]
        for t in taps:
            a = 8 * ((off + t) // 8 - lo)
            term = shifted[a:a + tb, :] * w_ref[t:t + 1, :]
            acc = term if acc is None else acc + term
    y = acc + b_ref[...]
    mu = jnp.mean(y, axis=-1, keepdims=True)
    d = y - mu
    var = jnp.mean(d * d, axis=-1, keepdims=True)
    z = d * lax.rsqrt(var + EPS) * g_ref[...] + beta_ref[...]
    out_ref[...] = (z * jax.nn.sigmoid(z)).astype(BF16)
    tail_ref[0] = ubuf[tb:tb + CONV_TAIL, :]


def _conv(glu, tail0, w, b, g, beta, nseq, tb):
    t = glu.shape[0]
    nblk = t // (nseq * tb)
    full = lambda a: pl.BlockSpec(a.shape, lambda s, j: (0, 0))
    return pl.pallas_call(
        functools.partial(_conv_body, tb=tb),
        grid=(nseq, nblk),
        in_specs=[pl.BlockSpec((tb, 2 * CONV_CH), lambda s, j: (s * nblk + j, 0)),
                  pl.BlockSpec((1, CONV_TAIL, CONV_CH), lambda s, j: (s, 0, 0)),
                  full(w), full(b), full(g), full(beta)],
        out_specs=[pl.BlockSpec((tb, CONV_CH), lambda s, j: (s * nblk + j, 0)),
                   pl.BlockSpec((1, CONV_TAIL, CONV_CH), lambda s, j: (s, 0, 0))],
        out_shape=[jax.ShapeDtypeStruct((t, CONV_CH), BF16),
                   jax.ShapeDtypeStruct((nseq, CONV_TAIL, CONV_CH), F32)],
        scratch_shapes=[pltpu.VMEM((tb + CONV_TAIL, CONV_CH), F32)] * 2,
        compiler_params=_params(("parallel", "arbitrary")),
        name="conv",
    )(glu, tail0, w, b, g, beta)


def _in_proj_conv_body(x_ref, g_ref, wqkv_ref, widx_ref, wglu_ref, tail0_ref, cw_ref, cb_ref, cg_ref, cbeta_ref,
                       k_ref, v_ref, kw_ref, qh_ref, kh_ref, vth_ref, qih_ref, conv_ref, tail_ref,
                       ubuf, shifted, *, tb):
    glu = _project(x_ref, g_ref, wqkv_ref, widx_ref, wglu_ref, k_ref, v_ref, kw_ref, qh_ref, kh_ref, vth_ref, qih_ref)
    _conv_core(glu, pl.program_id(0), tail0_ref, cw_ref, cb_ref, cg_ref, cbeta_ref,
               conv_ref, tail_ref, ubuf, shifted, tb)


def _in_proj_conv(x, g, wqkv, widx, wglu, tail0, cw, cb, cg, cbeta, tb):
    t = x.shape[0]
    row = lambda w: pl.BlockSpec((tb, w), lambda i: (i, 0))
    full = lambda a: pl.BlockSpec(a.shape, lambda i: (0,) * a.ndim)
    heads = lambda n: pl.BlockSpec((n, tb, HEAD_DIM), lambda i: (0, i, 0))
    return pl.pallas_call(
        functools.partial(_in_proj_conv_body, tb=tb),
        grid=(t // tb,),
        in_specs=[row(D_MODEL), full(g), full(wqkv), full(widx), full(wglu),
                  full(tail0), full(cw), full(cb), full(cg), full(cbeta)],
        out_specs=[row(ATT_WIDTH), row(ATT_WIDTH), row(128),
                   heads(N_HEADS), heads(N_HEADS),
                   pl.BlockSpec((N_HEADS, PV_ROWS, tb), lambda i: (0, 0, i)), heads(N_IDX_HEADS),
                   row(CONV_CH), pl.BlockSpec((1, CONV_TAIL, CONV_CH), lambda i: (0, 0, 0))],
        out_shape=[jax.ShapeDtypeStruct((t, ATT_WIDTH), F32),
                   jax.ShapeDtypeStruct((t, ATT_WIDTH), F32),
                   jax.ShapeDtypeStruct((t, 128), F32),
                   jax.ShapeDtypeStruct((N_HEADS, t, HEAD_DIM), BF16),
                   jax.ShapeDtypeStruct((N_HEADS, t, HEAD_DIM), BF16),
                   jax.ShapeDtypeStruct((N_HEADS, PV_ROWS, t), BF16),
                   jax.ShapeDtypeStruct((N_IDX_HEADS, t, IDX_DIM), BF16),
                   jax.ShapeDtypeStruct((t, CONV_CH), BF16),
                   jax.ShapeDtypeStruct((1, CONV_TAIL, CONV_CH), F32)],
        scratch_shapes=[pltpu.VMEM((tb + CONV_TAIL, CONV_CH), F32)] * 2,
        compiler_params=_params(("arbitrary",)),
        name="in_proj_conv",
    )(x, g, wqkv, widx, wglu, tail0, cw, cb, cg, cbeta)


def _idx_scores_t(ki_tile, qi_ref, w):
    acc = None
    for h in range(N_IDX_HEADS):
        term = w[h:h + 1, :] * jnp.maximum(_nt(ki_tile, qi_ref[h]), 0.0)
        acc = term if acc is None else acc + term
    return acc


def _chunk_limit(q0, tq):
    qpos = q0 + lax.broadcasted_iota(I32, (1, tq), 1)
    return ((qpos >> CHUNK_SHIFT) + 1) << CHUNK_SHIFT


def _bit_planes(words):
    x = list(words)
    mask, j = 0x0000FFFF, 16
    while j:
        k = 0
        while k < 32:
            t = (x[k] ^ lax.shift_right_logical(x[k + j], jnp.int32(j))) & mask
            x[k] = x[k] ^ t
            x[k + j] = x[k + j] ^ (t << j)
            k = (k + j + 1) & ~j
        j >>= 1
        mask ^= (mask << j) & 0xFFFFFFFF
    return x


def _sublane_prefix(x):
    sub = lax.broadcasted_iota(I32, x.shape, 0)
    for sh in (1, 2, 4):
        x = x + jnp.where(sub >= sh, pltpu.roll(x, sh, axis=0), 0)
    return x


def _sel_body(qi_ref, w_ref, ki_ref, sel_ref, planes_ref, cand_ref, gt_ref, *, tq, grp):
    i = pl.program_id(0)
    nt = i + 1
    ng = (nt + grp - 1) // grp
    w = w_ref[...]
    limit = _chunk_limit(i * tq, tq)
    rows = lax.broadcasted_iota(I32, (tq, tq), 0)
    assert tq == 8 * 32

    def fill(t, diagonal):
        r0 = pl.multiple_of(t * tq, tq)
        b = pltpu.bitcast(_idx_scores_t(ki_ref[pl.ds(r0, tq), :], qi_ref, w), I32)
        u = b ^ ((b >> 31) | INT_MIN)
        if diagonal:
            u = jnp.where(rows + r0 < limit, u, 0)
        planes = _bit_planes([u[8 * k:8 * k + 8, :] for k in range(32)])
        at = pl.ds(pl.multiple_of(t * 8, 8), 8)
        for p in range(32):
            planes_ref[p, at, :] = planes[p]
        cand_ref[at, :] = jnp.full((8, tq), -1, I32)
        gt_ref[at, :] = jnp.zeros((8, tq), I32)

    def fill_full(t, c):
        fill(t, False)
        return c

    lax.fori_loop(0, nt - 1, fill_full, 0)
    fill(nt - 1, True)

    def pad(t, c):
        at = pl.ds(pl.multiple_of(t * 8, 8), 8)
        planes_ref[:, at, :] = jnp.zeros((32, 8, tq), I32)
        cand_ref[at, :] = jnp.zeros((8, tq), I32)
        gt_ref[at, :] = jnp.zeros((8, tq), I32)
        return c

    lax.fori_loop(nt, ng * grp, pad, 0)

    def sweep(p_prev, flip, p_next):
        def group(g, acc):
            at = pl.ds(pl.multiple_of(g * (grp * 8), grp * 8), grp * 8)
            c = cand_ref[at, :]
            if p_prev is not None:
                prev = planes_ref[p_prev, at, :]
                gt_ref[at, :] = gt_ref[at, :] | (c & prev & flip)
                c = c & (prev ^ flip)
                cand_ref[at, :] = c
            if p_next is not None:
                hit = lax.population_count(c & planes_ref[p_next, at, :])
                acc = acc + jnp.sum(hit.reshape(grp, 8, tq), axis=0)
            return acc
        acc = lax.fori_loop(0, ng, group, jnp.zeros((8, tq), I32))
        return jnp.sum(acc, axis=0, keepdims=True)

    def decide(p, cnt, kth_u, need):
        one = cnt >= need
        kth_u = jnp.where(one, kth_u | lax.shift_right_logical(jnp.int32(INT_MIN), p), kth_u)
        return kth_u, jnp.where(one, need, need - cnt), jnp.where(one, 0, -1)

    zero = jnp.zeros((1, tq), I32)
    kth_u, need, flip = decide(jnp.int32(0), sweep(None, None, 0), zero, jnp.full((1, tq), TOPK, I32))

    def bit_step(p, carry):
        kth_u, need, flip = carry
        return decide(p, sweep(p - 1, flip, p), kth_u, need)

    kth_u, need, flip = lax.fori_loop(1, 32, bit_step, (kth_u, need, flip))
    sweep(31, flip, None)
    need = jnp.where(kth_u == 0, 0, need)

    def tie_count(t):
        c = cand_ref[pl.ds(pl.multiple_of(t * 8, 8), 8), :]
        return c, jnp.sum(lax.population_count(c), axis=0, keepdims=True)

    def find(t, carry):
        seen, part, words = carry
        c, n = tie_count(t)
        hit = (seen < need) & (seen + n > need)
        return seen + n, jnp.where(hit, need - seen, part), jnp.where(hit, c, words)

    _, part, words = lax.fori_loop(0, nt, find, (zero, zero, jnp.zeros((8, tq), I32)))
    keep = jnp.zeros((8, tq), I32)
    before = zero
    for k in range(32):
        bit = lax.shift_right_logical(words, jnp.int32(31 - k)) & 1
        rank = before + _sublane_prefix(bit)
        keep = keep | jnp.where((bit == 1) & (rank <= part), INT_MIN if k == 0 else 1 << (31 - k), 0)
        before = rank[7:8, :]

    def emit(t, seen):
        c, n = tie_count(t)
        at = pl.ds(pl.multiple_of(t * 8, 8), 8)
        whole = seen + n <= need
        partial = (seen < need) & jnp.logical_not(whole)
        sel_ref[0, at, :] = gt_ref[at, :] | jnp.where(whole, c, jnp.where(partial, keep, 0))
        return seen + n

    lax.fori_loop(0, nt, emit, zero)

    def clear(t, c):
        sel_ref[0, pl.ds(pl.multiple_of(t * 8, 8), 8), :] = jnp.zeros((8, tq), I32)
        return c

    lax.fori_loop(nt, sel_ref.shape[1] // 8, clear, 0)


def _idx_sel(qi_hm, w_t, ki, tq):
    s = ki.shape[0]
    nrow = s // tq * 8
    return pl.pallas_call(
        functools.partial(_sel_body, tq=tq, grp=_pick(s // tq, 16)),
        grid=(s // tq,),
        in_specs=[pl.BlockSpec((N_IDX_HEADS, tq, IDX_DIM), lambda i: (0, i, 0)),
                  pl.BlockSpec((N_IDX_HEADS, tq), lambda i: (0, i)),
                  pl.BlockSpec((s, IDX_DIM), lambda i: (0, 0))],
        out_specs=pl.BlockSpec((1, nrow, tq), lambda i: (i, 0, 0)),
        out_shape=jax.ShapeDtypeStruct((s // tq, nrow, tq), I32),
        scratch_shapes=[pltpu.VMEM((32, nrow, tq), I32), pltpu.VMEM((nrow, tq), I32),
                        pltpu.VMEM((nrow, tq), I32)],
        compiler_params=_params(("arbitrary",)),
        name="idx_sel",
    )(qi_hm, w_t, ki)


def _attn_body(qb_ref, kb_ref, q_ref, k_ref, vt_ref, sel_ref,
               out_ref, m_ref, acc_ref, s_ref, p_ref, bias_ref, *, tq, tk):
    step = pl.program_id(0)
    i = qb_ref[step]
    j = kb_ref[step]

    @pl.when(j == 0)
    def _():
        m_ref[...] = jnp.full(m_ref.shape, M_INIT, F32)
        acc_ref[...] = jnp.zeros(acc_ref.shape, F32)

    for tile in range(tk // tq):
        words = sel_ref[0, tile * 8:(tile + 1) * 8, :]
        for k in range(32):
            bias_ref[pl.ds(tile * tq + 8 * k, 8), :] = jnp.where((words << k) < 0, 0.0, NEG_INF)
    bias = bias_ref[...]
    tops = []
    for h in range(N_HEADS):
        s = _nt(k_ref[h], q_ref[h]) + bias
        s_ref[h] = s
        tops.append(jnp.max(s, axis=0, keepdims=True))
    m_old = m_ref[...]
    m_new = jnp.maximum(m_old, jnp.concatenate(tops, axis=0))
    alpha = jnp.exp2(m_old - m_new)
    for h in range(N_HEADS):
        p_ref[h] = jnp.exp2(s_ref[h] - m_new[h:h + 1, :]).astype(BF16)
    m_ref[...] = m_new
    for h in range(N_HEADS):
        acc_ref[h] = alpha[h:h + 1, :] * acc_ref[h] + _dot(vt_ref[h], p_ref[h])

    @pl.when(j == ((i + 1) * tq - 1) // tk)
    def _():
        o = jnp.concatenate([acc_ref[h, 0:HEAD_DIM, :] / acc_ref[h, HEAD_DIM:HEAD_DIM + 1, :]
                             for h in range(N_HEADS)], axis=0)
        out_ref[...] = o.T.astype(BF16)


def _attn(q_hm, k_hm, vt_hm, sel, tq, tk):
    s = q_hm.shape[1]
    assert tk % tq == 0
    qb, kb = [], []
    for i in range(s // tq):
        for j in range(((i + 1) * tq - 1) // tk + 1):
            qb.append(i)
            kb.append(j)
    qb = jnp.asarray(qb, I32)
    kb = jnp.asarray(kb, I32)
    grid_spec = pltpu.PrefetchScalarGridSpec(
        num_scalar_prefetch=2,
        grid=(int(qb.shape[0]),),
        in_specs=[pl.BlockSpec((N_HEADS, tq, HEAD_DIM), lambda t, qb, kb: (0, qb[t], 0)),
                  pl.BlockSpec((N_HEADS, tk, HEAD_DIM), lambda t, qb, kb: (0, kb[t], 0)),
                  pl.BlockSpec((N_HEADS, PV_ROWS, tk), lambda t, qb, kb: (0, 0, kb[t])),
                  pl.BlockSpec((1, tk // tq * 8, tq), lambda t, qb, kb: (qb[t], kb[t], 0))],
        out_specs=pl.BlockSpec((tq, ATT_WIDTH), lambda t, qb, kb: (qb[t], 0)),
        scratch_shapes=[pltpu.VMEM((N_HEADS, tq), F32),
                        pltpu.VMEM((N_HEADS, PV_ROWS, tq), F32),
                        pltpu.VMEM((N_HEADS, tk, tq), F32),
                        pltpu.VMEM((N_HEADS, tk, tq), BF16),
                        pltpu.VMEM((tk, tq), F32)],
    )
    return pl.pallas_call(
        functools.partial(_attn_body, tq=tq, tk=tk),
        grid_spec=grid_spec,
        out_shape=jax.ShapeDtypeStruct((s, ATT_WIDTH), BF16),
        compiler_params=_params(("arbitrary",)),
        name="attn",
    )(qb, kb, q_hm, k_hm, vt_hm, sel)


NEW_PAD = 128


def _sample_attn_body(q_ref, qi_ref, w_ref, kic_ref, kin_ref, kc_ref, vc_ref, kn_ref, vn_ref,
                      out_ref, kinp, knp, vnp, *, past, tn):
    kinp[...] = jnp.zeros(kinp.shape, BF16)
    kinp[0:tn, :] = kin_ref[0].astype(BF16)
    knp[...] = jnp.zeros(knp.shape, BF16)
    vnp[...] = jnp.zeros(vnp.shape, BF16)
    for h in range(N_HEADS):
        knp[h, 0:tn, :] = kn_ref[0, h]
        vnp[h, 0:tn, :] = vn_ref[0, h]

    w = w_ref[0]
    kic = kic_ref[0].astype(BF16)
    kin = kinp[...]
    sc = None
    sn = None
    for h in range(N_IDX_HEADS):
        qih = qi_ref[0, h]
        wc = w[:, h:h + 1]
        tc = wc * jnp.maximum(_nt(qih, kic), 0.0)
        tnw = wc * jnp.maximum(_nt(qih, kin), 0.0)
        sc = tc if sc is None else sc + tc
        sn = tnw if sn is None else sn + tnw
    new_ok = lax.broadcasted_iota(I32, (tn, NEW_PAD), 1) < tn
    keyc = _order_key(sc)
    keyn = jnp.where(new_ok, _order_key(sn), INT_MIN)

    def count(pc, pn):
        return (jnp.sum(jnp.where(pc, 1, 0), axis=1, keepdims=True)
                + jnp.sum(jnp.where(pn, 1, 0), axis=1, keepdims=True))

    def bit_step(p, u):
        cand_u = u | (jnp.int32(1) << (31 - p))
        cand = cand_u ^ INT_MIN
        return jnp.where(count(keyc >= cand, keyn >= cand) >= TOPK, cand_u, u)

    kth = lax.fori_loop(0, 32, bit_step, jnp.zeros((tn, 1), I32)) ^ INT_MIN
    need = (TOPK - count(keyc > kth, keyn > kth)).astype(F32)

    cw = 256
    tri = jnp.where(lax.broadcasted_iota(I32, (cw, cw), 0) <= lax.broadcasted_iota(I32, (cw, cw), 1),
                    1.0, 0.0).astype(BF16)
    seen = jnp.zeros((tn, 1), F32)
    bias_c = []
    for c in range(past // cw):
        kc_ = keyc[:, c * cw:(c + 1) * cw]
        eq = kc_ == kth
        pre = _dot(jnp.where(eq, 1.0, 0.0).astype(BF16), tri)
        sel = (kc_ > kth) | (eq & (pre + seen <= need))
        bias_c.append(jnp.where(sel, 0.0, NEG_INF))
        seen = seen + pre[:, cw - 1:cw]
    bias_c = jnp.concatenate(bias_c, axis=1)
    eqn = keyn == kth
    pren = _dot(jnp.where(eqn, 1.0, 0.0).astype(BF16), tri[:NEW_PAD, :NEW_PAD])
    bias_n = jnp.where(new_ok & ((keyn > kth) | (eqn & (pren + seen <= need))), 0.0, NEG_INF)

    kc = kc_ref[0].astype(BF16)
    vc = vc_ref[0].astype(BF16)
    for h in range(N_HEADS):
        qh = q_ref[0, h]
        head = slice(h * HEAD_DIM, (h + 1) * HEAD_DIM)
        s_c = _nt(qh, kc[:, head]) + bias_c
        s_n = _nt(qh, knp[h]) + bias_n
        m = jnp.maximum(jnp.max(s_c, axis=1, keepdims=True), jnp.max(s_n, axis=1, keepdims=True))
        p_c = jnp.exp2(s_c - m)
        p_n = jnp.exp2(s_n - m)
        l = jnp.sum(p_c, axis=1, keepdims=True) + jnp.sum(p_n, axis=1, keepdims=True)
        o = _dot(p_c.astype(BF16), vc[:, head]) + _dot(p_n.astype(BF16), vnp[h])
        out_ref[0, h] = (o / l).astype(BF16)


def _sample_attn(q_hm, qi_hm, w, kic, kin, kc, vc, kn_hm, vn_hm):
    nb, _, tn, _ = q_hm.shape
    past = kic.shape[1]
    b4 = lambda a: pl.BlockSpec((1,) + a.shape[1:], lambda b: (b, 0, 0, 0))
    b3 = lambda a: pl.BlockSpec((1,) + a.shape[1:], lambda b: (b, 0, 0))
    return pl.pallas_call(
        functools.partial(_sample_attn_body, past=past, tn=tn),
        grid=(nb,),
        in_specs=[b4(q_hm), b4(qi_hm), b3(w), b3(kic), b3(kin), b3(kc), b3(vc), b4(kn_hm), b4(vn_hm)],
        out_specs=pl.BlockSpec((1, N_HEADS, tn, HEAD_DIM), lambda b: (b, 0, 0, 0)),
        out_shape=jax.ShapeDtypeStruct((nb, N_HEADS, tn, HEAD_DIM), BF16),
        scratch_shapes=[pltpu.VMEM((NEW_PAD, IDX_DIM), BF16),
                        pltpu.VMEM((N_HEADS, NEW_PAD, HEAD_DIM), BF16),
                        pltpu.VMEM((N_HEADS, NEW_PAD, HEAD_DIM), BF16)],
        compiler_params=_params(("parallel",)),
        name="sample_attn",
    )(q_hm, qi_hm, w, kic, kin, kc, vc, kn_hm, vn_hm)


def _cmpx(a, b):
    return jnp.maximum(a, b), jnp.minimum(a, b)


def _bitonic_merge_desc(v):
    n = len(v)
    v = list(v)
    j = n // 2
    while j >= 1:
        for i in range(n):
            l = i ^ j
            if l > i:
                v[i], v[l] = _cmpx(v[i], v[l])
        j //= 2
    return v


def _bitonic_sort_desc(v):
    n = len(v)
    v = list(v)
    k = 2
    while k <= n:
        j = k // 2
        while j >= 1:
            for i in range(n):
                l = i ^ j
                if l > i:
                    hi, lo = _cmpx(v[i], v[l])
                    v[i], v[l] = (hi, lo) if (i & k) == 0 else (lo, hi)
            j //= 2
        k *= 2
    return v


def _merge_top(a, b):
    n = len(a)
    return _bitonic_merge_desc([jnp.maximum(a[k], b[n - 1 - k]) for k in range(n)])


def _top16_desc(vals):
    groups = [_bitonic_sort_desc(vals[g:g + PEER_TOPK]) for g in range(0, len(vals), PEER_TOPK)]
    while len(groups) > 1:
        groups = [_merge_top(groups[g], groups[g + 1]) for g in range(0, len(groups), 2)]
    return groups[0]


def _mid_body(x_ref, att_ref, conv_ref, woa_ref, woc_ref, g_ref, wqt_ref, a1_ref, a2_ref,
              h_ref, xn_ref, qht_ref, st_ref):
    h = x_ref[...] + _dot(att_ref[...], woa_ref[...]) + _dot(conv_ref[...], woc_ref[...])
    h_ref[...] = h
    xn = _rms(h, g_ref[...]).astype(BF16)
    xn_ref[...] = xn
    qht = _nt(wqt_ref[...], xn).astype(BF16)
    qht_ref[...] = qht
    half = PEER_HEADS * PEER_HALF
    s1 = _dot(a1_ref[...], qht[:half])
    s2 = _dot(a2_ref[...], qht[half:])
    rows = lambda s: [s[r * PEER_HEADS:(r + 1) * PEER_HEADS, :] for r in range(PEER_KEYS)]
    v1 = _top16_desc(rows(s1))
    v2 = _top16_desc(rows(s2))
    ninf = jnp.full(v1[0].shape, NEG_INF, F32)
    top = [v1[0] + v2[b] for b in range(PEER_TOPK)]
    for a in range(1, PEER_TOPK):
        n_a = PEER_TOPK // (a + 1)
        top = _merge_top(top, [v1[a] + v2[b] if b < n_a else ninf for b in range(PEER_TOPK)])
    z = jnp.ones_like(top[0])
    for k in range(1, PEER_TOPK):
        z = z + jnp.exp(top[k] - top[0])
    st_ref[0:8, :] = v1[0]
    st_ref[8:16, :] = v2[0]
    st_ref[16:24, :] = top[PEER_TOPK - 1]
    st_ref[24:32, :] = 1.0 / z


def _mid(x, att, conv, woa, woc, g, wqt, a1, a2, tb):
    t = x.shape[0]
    row = lambda w: pl.BlockSpec((tb, w), lambda i: (i, 0))
    col = lambda r: pl.BlockSpec((r, tb), lambda i: (0, i))
    full = lambda a: pl.BlockSpec(a.shape, lambda i: (0, 0))
    return pl.pallas_call(
        _mid_body,
        grid=(t // tb,),
        in_specs=[row(D_MODEL), row(ATT_WIDTH), row(CONV_CH), full(woa), full(woc), full(g),
                  full(wqt), full(a1), full(a2)],
        out_specs=[row(D_MODEL), row(D_MODEL), col(D_MODEL), col(32)],
        out_shape=[jax.ShapeDtypeStruct((t, D_MODEL), F32),
                   jax.ShapeDtypeStruct((t, D_MODEL), BF16),
                   jax.ShapeDtypeStruct((D_MODEL, t), BF16),
                   jax.ShapeDtypeStruct((32, t), F32)],
        compiler_params=_params(("parallel",)),
        name="mid",
    )(x, att, conv, woa, woc, g, wqt, a1, a2)


def _peer_body(xn_ref, qht_ref, st_ref, a1_ref, a2_ref, u_ref, vt_ref, h_ref, g_ref, y_ref,
               s1_ref, s2_ref, e1_ref, e2_ref, hid_ref, out_ref, *, eb):
    e = pl.program_id(1)
    nsub = eb // PEER_KEYS
    half = PEER_HEADS * PEER_HALF

    @pl.when(e == 0)
    def _():
        qht = qht_ref[...]
        s1 = _dot(a1_ref[...], qht[:half])
        s2 = _dot(a2_ref[...], qht[half:])
        s1_ref[...] = s1
        s2_ref[...] = s2
        for h in range(PEER_HEADS):
            r = slice(h * PEER_KEYS, (h + 1) * PEER_KEYS)
            e1_ref[r, :] = jnp.exp(s1[r] - st_ref[h:h + 1, :])
            e2_ref[r, :] = jnp.exp(s2[r] - st_ref[8 + h:9 + h, :]) * st_ref[24 + h:25 + h, :] * 0.5
        out_ref[...] = jnp.zeros(out_ref.shape, F32)

    xn = xn_ref[...]
    for ii in range(nsub):
        i = e * nsub + ii
        a = _nt(u_ref[ii * PEER_KEYS:(ii + 1) * PEER_KEYS, :], xn)
        gate = None
        for h in range(PEER_HEADS):
            r = slice(h * PEER_KEYS, (h + 1) * PEER_KEYS)
            pair = s1_ref[pl.ds(h * PEER_KEYS + i, 1), :] + s2_ref[r, :]
            g = e1_ref[pl.ds(h * PEER_KEYS + i, 1), :] * e2_ref[r, :]
            g = jnp.where(pair >= st_ref[16 + h:17 + h, :], g, 0.0)
            gate = g if gate is None else gate + g
        hid = a * (1.0 + lax.erf(a * INV_SQRT2)) * gate
        hid_ref[ii * PEER_KEYS:(ii + 1) * PEER_KEYS, :] = hid.astype(BF16)
    out_ref[...] += _dot(vt_ref[...], hid_ref[...])

    @pl.when(e == pl.num_programs(1) - 1)
    def _():
        y_ref[...] = _rms(h_ref[...] + out_ref[...].T, g_ref[...])


def _peer(xn, qht, st, a1, a2, u, vt, h, g, tb, eb):
    t = xn.shape[0]
    ne = u.shape[0] // eb
    full = lambda a: pl.BlockSpec(a.shape, lambda i, e: (0, 0))
    return pl.pallas_call(
        functools.partial(_peer_body, eb=eb),
        grid=(t // tb, ne),
        in_specs=[pl.BlockSpec((tb, D_MODEL), lambda i, e: (i, 0)),
                  pl.BlockSpec((D_MODEL, tb), lambda i, e: (0, i)),
                  pl.BlockSpec((32, tb), lambda i, e: (0, i)),
                  full(a1), full(a2),
                  pl.BlockSpec((eb, D_MODEL), lambda i, e: (e, 0)),
                  pl.BlockSpec((D_MODEL, eb), lambda i, e: (0, e)),
                  pl.BlockSpec((tb, D_MODEL), lambda i, e: (i, 0)),
                  full(g)],
        out_specs=pl.BlockSpec((tb, D_MODEL), lambda i, e: (i, 0)),
        out_shape=jax.ShapeDtypeStruct((t, D_MODEL), F32),
        scratch_shapes=[pltpu.VMEM((PEER_HEADS * PEER_KEYS, tb), F32)] * 4
                       + [pltpu.VMEM((eb, tb), BF16), pltpu.VMEM((D_MODEL, tb), F32)],
        compiler_params=_params(("parallel", "arbitrary")),
        name="peer",
    )(xn, qht, st, a1, a2, u, vt, h, g)


def _pick(n, pref):
    b = min(n, pref)
    while n % b:
        b //= 2
    return b


def _token_stage(x, att, conv, p, tb_mid, tb_peer):
    h, xn, qht, st = _mid(x, att, conv, p["woa"], p["woc"], p["ffn_g"], p["wqt"], p["a1s"], p["a2s"], tb_mid)
    return _peer(xn, qht, st, p["a1d"], p["a2d"], p["u"], p["vt"], h, p["final_g"], tb_peer, 2048)


def kernel(x_prompt, x_sample, cache_k, cache_v, cache_kidx, state_conv, attn_norm_g, w_in, conv_w, conv_b,
           conv_ln_g, conv_ln_b, w_out, ffn_norm_g, peer_wq, peer_subkeys, peer_u, peer_v, final_norm_g):
    assert w_in.shape[0] == 1, "single layer"
    nbp, s, _ = x_prompt.shape
    nb, tn, _ = x_sample.shape
    assert nbp == 1
    past = cache_k.shape[2]

    wi = w_in[0]
    o_qi = 3 * ATT_WIDTH
    o_ki = o_qi + N_IDX_HEADS * IDX_DIM
    o_wi = o_ki + IDX_DIM
    o_glu = o_wi + N_IDX_HEADS
    wqkv = wi[:, :o_qi].astype(BF16)
    widx = jnp.concatenate([wi[:, o_qi:o_glu], jnp.zeros((D_MODEL, 128 - IDX_DIM - N_IDX_HEADS), F32)],
                           axis=1).astype(BF16)
    wglu = wi[:, o_glu:].astype(BF16)
    g_attn = attn_norm_g[0][None, :]
    cw = jnp.concatenate([conv_w[0], jnp.zeros((CONV_TAIL - CONV_WIDTH, CONV_CH), F32)], axis=0)
    cb, cg, cbeta = conv_b[0][None, :], conv_ln_g[0][None, :], conv_ln_b[0][None, :]
    eye = jnp.eye(PEER_HEADS, dtype=F32)
    sk = peer_subkeys[0]
    a_s = [jnp.einsum("id,hg->ihgd", sk[c], eye).reshape(PEER_KEYS * PEER_HEADS, PEER_HEADS * PEER_HALF)
           .astype(BF16) for c in range(2)]
    a_d = [jnp.einsum("id,hg->higd", sk[c], eye).reshape(PEER_KEYS * PEER_HEADS, PEER_HEADS * PEER_HALF)
           .astype(BF16) for c in range(2)]
    wq = peer_wq[0].reshape(D_MODEL, PEER_HEADS, 2, PEER_HALF).transpose(2, 1, 3, 0)
    p = {
        "woa": w_out[0][:ATT_WIDTH].astype(BF16), "woc": w_out[0][ATT_WIDTH:].astype(BF16),
        "ffn_g": ffn_norm_g[0][None, :], "final_g": final_norm_g[None, :],
        "wqt": wq.reshape(D_MODEL, D_MODEL).astype(BF16),
        "a1s": a_s[0], "a2s": a_s[1], "a1d": a_d[0], "a2d": a_d[1],
        "u": peer_u[0].astype(BF16), "vt": peer_v[0].T.astype(BF16),
    }
    w_scale = (N_IDX_HEADS ** -0.5) * (IDX_DIM ** -0.5)

    xp = x_prompt[0]
    tb = _pick(s, 256)
    k, v, kw, q_hm, k_hm, vt_hm, qi_hm, conv_p, tail_p = _in_proj_conv(
        xp, g_attn, wqkv, widx, wglu, jnp.zeros((1, CONV_TAIL, CONV_CH), F32), cw, cb, cg, cbeta, tb)
    ki = kw[:, :IDX_DIM]
    ki_b = ki.astype(BF16)
    w_t = (kw[:, IDX_DIM:IDX_DIM + N_IDX_HEADS] * w_scale).T
    tq = _pick(s, 256)
    sel = _idx_sel(qi_hm, w_t, ki_b, tq)
    att_p = _attn(q_hm, k_hm, vt_hm, sel, tq, _pick(s, 1024))
    y_p = _token_stage(xp, att_p, conv_p, p, tb, _pick(s, 512))

    xs = x_sample.reshape(nb * tn, D_MODEL)
    ts = nb * tn
    tbs = _pick(ts, 256)
    ks, vs, kws, glus, qs_hm, _, _, qis_hm = _in_proj(xs, g_attn, wqkv, widx, wglu, tbs)
    pad = jnp.zeros((nb, CONV_TAIL - (CONV_WIDTH - 1), CONV_CH), F32)
    conv_s, tail_s = _conv(glus, jnp.concatenate([pad, state_conv[0]], axis=1), cw, cb, cg, cbeta, nb, tn)
    hm4 = lambda a, nh, dt=BF16: a.reshape(nb, tn, nh, a.shape[1] // nh).transpose(0, 2, 1, 3).astype(dt)
    per_stream = lambda a: a.reshape(a.shape[0], nb, tn, a.shape[2]).transpose(1, 0, 2, 3)
    kis = kws[:, :IDX_DIM]
    att_s = _sample_attn(
        per_stream(qs_hm), per_stream(qis_hm),
        (kws[:, IDX_DIM:IDX_DIM + N_IDX_HEADS] * w_scale).reshape(nb, tn, N_IDX_HEADS),
        cache_kidx[0], kis.reshape(nb, tn, IDX_DIM),
        cache_k[0].reshape(nb, past, ATT_WIDTH), cache_v[0].reshape(nb, past, ATT_WIDTH),
        hm4(ks, N_HEADS), hm4(vs, N_HEADS))
    att_s = att_s.transpose(0, 2, 1, 3).reshape(ts, ATT_WIDTH)
    y_s = _token_stage(xs, att_s, conv_s, p, tbs, tbs)

    hd = (N_HEADS, HEAD_DIM)
    keep = CONV_TAIL - (CONV_WIDTH - 1)
    return (y_p[None], y_s.reshape(nb, tn, D_MODEL),
            k.reshape(1, 1, s, *hd), v.reshape(1, 1, s, *hd), ki[None, None], tail_p[None, :, keep:],
            ks.reshape(1, nb, tn, *hd), vs.reshape(1, nb, tn, *hd), kis.reshape(1, nb, tn, IDX_DIM),
            tail_s[None, :, keep:])
```

```python
import functools

import jax
import jax.numpy as jnp
from jax import lax
from jax.experimental import pallas as pl
from jax.experimental.pallas import tpu as pltpu

F32 = jnp.float32
BF16 = jnp.bfloat16
I32 = jnp.int32

D_MODEL = 1024
N_HEADS = 8
HEAD_DIM = 64
ATT_WIDTH = N_HEADS * HEAD_DIM
PV_ROWS = HEAD_DIM + 16
N_IDX_HEADS = 4
IDX_DIM = 64
TOPK = 256
CHUNK_SHIFT = 6
CONV_CH = 512
CONV_WIDTH = 31
CONV_TAIL = 32
PEER_HEADS = 8
PEER_KEYS = 128
PEER_HALF = 64
PEER_TOPK = 16
EPS = 1e-6
INT_MIN = -2147483648
ORDER_MASK = 0x7FFFFFFF
INV_SQRT2 = 0.7071067811865476
LOG2E = 1.4426950408889634
NEG_INF = float("-inf")
M_INIT = -1e30

VMEM_LIMIT = 56 * 1024 * 1024


def _params(sem, vmem=VMEM_LIMIT):
    return pltpu.CompilerParams(dimension_semantics=sem, vmem_limit_bytes=vmem)


def _nt(a, b):
    return lax.dot_general(a, b, (((1,), (1,)), ((), ())), preferred_element_type=F32)


def _dot(a, b):
    return jnp.dot(a, b, preferred_element_type=F32)


def _rms(x, g):
    return x * lax.rsqrt(jnp.mean(x * x, axis=-1, keepdims=True) + EPS) * g


def _order_key(x):
    b = pltpu.bitcast(x, I32)
    return b ^ ((b >> 31) & ORDER_MASK)


def _in_proj_body(x_ref, g_ref, wqkv_ref, widx_ref, wglu_ref,
                  k_ref, v_ref, kw_ref, glu_ref, qh_ref, kh_ref, vth_ref, qih_ref):
    glu_ref[...] = _project(x_ref, g_ref, wqkv_ref, widx_ref, wglu_ref,
                            k_ref, v_ref, kw_ref, qh_ref, kh_ref, vth_ref, qih_ref)


def _project(x_ref, g_ref, wqkv_ref, widx_ref, wglu_ref, k_ref, v_ref, kw_ref, qh_ref, kh_ref, vth_ref, qih_ref):
    xn = _rms(x_ref[...], g_ref[...]).astype(BF16)
    qkv = _dot(xn, wqkv_ref[...])
    q = qkv[:, :ATT_WIDTH] * (HEAD_DIM ** -0.5 * LOG2E)
    k = qkv[:, ATT_WIDTH:2 * ATT_WIDTH]
    v = qkv[:, 2 * ATT_WIDTH:]
    k_ref[...] = k
    v_ref[...] = v
    for h in range(N_HEADS):
        head = slice(h * HEAD_DIM, (h + 1) * HEAD_DIM)
        qh_ref[h] = q[:, head].astype(BF16)
        kh_ref[h] = k[:, head].astype(BF16)
    vth_ref[:, 0:HEAD_DIM, :] = v.T.reshape(N_HEADS, HEAD_DIM, v.shape[0]).astype(BF16)
    vth_ref[:, HEAD_DIM:, :] = jnp.ones((N_HEADS, PV_ROWS - HEAD_DIM, v.shape[0]), BF16)
    ix = _dot(xn, widx_ref[...])
    for h in range(N_IDX_HEADS):
        qih_ref[h] = ix[:, h * IDX_DIM:(h + 1) * IDX_DIM].astype(BF16)
    kw_ref[...] = ix[:, N_IDX_HEADS * IDX_DIM:]
    return _dot(xn, wglu_ref[...])


def _in_proj(x, g, wqkv, widx, wglu, tb):
    t = x.shape[0]
    row = lambda w: pl.BlockSpec((tb, w), lambda i: (i, 0))
    full = lambda a: pl.BlockSpec(a.shape, lambda i: (0, 0))
    heads = lambda n: pl.BlockSpec((n, tb, HEAD_DIM), lambda i: (0, i, 0))
    return pl.pallas_call(
        _in_proj_body,
        grid=(t // tb,),
        in_specs=[row(D_MODEL), full(g), full(wqkv), full(widx), full(wglu)],
        out_specs=[row(ATT_WIDTH), row(ATT_WIDTH), row(128), row(2 * CONV_CH),
                   heads(N_HEADS), heads(N_HEADS),
                   pl.BlockSpec((N_HEADS, PV_ROWS, tb), lambda i: (0, 0, i)), heads(N_IDX_HEADS)],
        out_shape=[jax.ShapeDtypeStruct((t, ATT_WIDTH), F32),
                   jax.ShapeDtypeStruct((t, ATT_WIDTH), F32),
                   jax.ShapeDtypeStruct((t, 128), F32),
                   jax.ShapeDtypeStruct((t, 2 * CONV_CH), F32),
                   jax.ShapeDtypeStruct((N_HEADS, t, HEAD_DIM), BF16),
                   jax.ShapeDtypeStruct((N_HEADS, t, HEAD_DIM), BF16),
                   jax.ShapeDtypeStruct((N_HEADS, PV_ROWS, t), BF16),
                   jax.ShapeDtypeStruct((N_IDX_HEADS, t, IDX_DIM), BF16)],
        compiler_params=_params(("parallel",)),
        name="in_proj",
    )(x, g, wqkv, widx, wglu)


def _conv_body(glu_ref, tail0_ref, w_ref, b_ref, g_ref, beta_ref, out_ref, tail_ref, ubuf, shifted, *, tb):
    _conv_core(glu_ref[...], pl.program_id(1), tail0_ref, w_ref, b_ref, g_ref, beta_ref,
               out_ref, tail_ref, ubuf, shifted, tb)


def _conv_core(glu, j, tail0_ref, w_ref, b_ref, g_ref, beta_ref, out_ref, tail_ref, ubuf, shifted, tb):
    @pl.when(j == 0)
    def _():
        ubuf[0:CONV_TAIL, :] = tail0_ref[0]

    @pl.when(j > 0)
    def _():
        ubuf[0:CONV_TAIL, :] = ubuf[tb:tb + CONV_TAIL, :]

    ubuf[CONV_TAIL:CONV_TAIL + tb, :] = glu[:, :CONV_CH] * jax.nn.sigmoid(glu[:, CONV_CH:])
    off = CONV_TAIL - (CONV_WIDTH - 1)
    acc = None
    for res in range(8):
        taps = [t for t in range(CONV_WIDTH) if (off + t) % 8 == res]
        lo, hi = (off + taps[0]) // 8, (off + taps[-1]) // 8
        n = 8 * (hi - lo) + tb
        shifted[0:n, :] = ubuf[res + 8 * lo:res + 8 * lo + n, :]
        for t in taps:
            a = 8 * ((off + t) // 8 - lo)
            term = shifted[a:a + tb, :] * w_ref[t:t + 1, :]
            acc = term if acc is None else acc + term
    y = acc + b_ref[...]
    mu = jnp.mean(y, axis=-1, keepdims=True)
    d = y - mu
    var = jnp.mean(d * d, axis=-1, keepdims=True)
    z = d * lax.rsqrt(var + EPS) * g_ref[...] + beta_ref[...]
    out_ref[...] = (z * jax.nn.sigmoid(z)).astype(BF16)
    tail_ref[0] = ubuf[tb:tb + CONV_TAIL, :]


def _conv(glu, tail0, w, b, g, beta, nseq, tb):
    t = glu.shape[0]
    nblk = t // (nseq * tb)
    full = lambda a: pl.BlockSpec(a.shape, lambda s, j: (0, 0))
    return pl.pallas_call(
        functools.partial(_conv_body, tb=tb),
        grid=(nseq, nblk),
        in_specs=[pl.BlockSpec((tb, 2 * CONV_CH), lambda s, j: (s * nblk + j, 0)),
                  pl.BlockSpec((1, CONV_TAIL, CONV_CH), lambda s, j: (s, 0, 0)),
                  full(w), full(b), full(g), full(beta)],
        out_specs=[pl.BlockSpec((tb, CONV_CH), lambda s, j: (s * nblk + j, 0)),
                   pl.BlockSpec((1, CONV_TAIL, CONV_CH), lambda s, j: (s, 0, 0))],
        out_shape=[jax.ShapeDtypeStruct((t, CONV_CH), BF16),
                   jax.ShapeDtypeStruct((nseq, CONV_TAIL, CONV_CH), F32)],
        scratch_shapes=[pltpu.VMEM((tb + CONV_TAIL, CONV_CH), F32)] * 2,
        compiler_params=_params(("parallel", "arbitrary")),
        name="conv",
    )(glu, tail0, w, b, g, beta)


def _in_proj_conv_body(x_ref, g_ref, wqkv_ref, widx_ref, wglu_ref, tail0_ref, cw_ref, cb_ref, cg_ref, cbeta_ref,
                       k_ref, v_ref, kw_ref, qh_ref, kh_ref, vth_ref, qih_ref, conv_ref, tail_ref,
                       ubuf, shifted, *, tb):
    glu = _project(x_ref, g_ref, wqkv_ref, widx_ref, wglu_ref, k_ref, v_ref, kw_ref, qh_ref, kh_ref, vth_ref, qih_ref)
    _conv_core(glu, pl.program_id(0), tail0_ref, cw_ref, cb_ref, cg_ref, cbeta_ref,
               conv_ref, tail_ref, ubuf, shifted, tb)


def _in_proj_conv(x, g, wqkv, widx, wglu, tail0, cw, cb, cg, cbeta, tb):
    t = x.shape[0]
    row = lambda w: pl.BlockSpec((tb, w), lambda i: (i, 0))
    full = lambda a: pl.BlockSpec(a.shape, lambda i: (0,) * a.ndim)
    heads = lambda n: pl.BlockSpec((n, tb, HEAD_DIM), lambda i: (0, i, 0))
    return pl.pallas_call(
        functools.partial(_in_proj_conv_body, tb=tb),
        grid=(t // tb,),
        in_specs=[row(D_MODEL), full(g), full(wqkv), full(widx), full(wglu),
                  full(tail0), full(cw), full(cb), full(cg), full(cbeta)],
        out_specs=[row(ATT_WIDTH), row(ATT_WIDTH), row(128),
                   heads(N_HEADS), heads(N_HEADS),
                   pl.BlockSpec((N_HEADS, PV_ROWS, tb), lambda i: (0, 0, i)), heads(N_IDX_HEADS),
                   row(CONV_CH), pl.BlockSpec((1, CONV_TAIL, CONV_CH), lambda i: (0, 0, 0))],
        out_shape=[jax.ShapeDtypeStruct((t, ATT_WIDTH), F32),
                   jax.ShapeDtypeStruct((t, ATT_WIDTH), F32),
                   jax.ShapeDtypeStruct((t, 128), F32),
                   jax.ShapeDtypeStruct((N_HEADS, t, HEAD_DIM), BF16),
                   jax.ShapeDtypeStruct((N_HEADS, t, HEAD_DIM), BF16),
                   jax.ShapeDtypeStruct((N_HEADS, PV_ROWS, t), BF16),
                   jax.ShapeDtypeStruct((N_IDX_HEADS, t, IDX_DIM), BF16),
                   jax.ShapeDtypeStruct((t, CONV_CH), BF16),
                   jax.ShapeDtypeStruct((1, CONV_TAIL, CONV_CH), F32)],
        scratch_shapes=[pltpu.VMEM((tb + CONV_TAIL, CONV_CH), F32)] * 2,
        compiler_params=_params(("arbitrary",)),
        name="in_proj_conv",
    )(x, g, wqkv, widx, wglu, tail0, cw, cb, cg, cbeta)


def _idx_scores_t(ki_tile, qi_ref, w):
    acc = None
    for h in range(N_IDX_HEADS):
        term = w[h:h + 1, :] * jnp.maximum(_nt(ki_tile, qi_ref[h]), 0.0)
        acc = term if acc is None else acc + term
    return acc


def _chunk_limit(q0, tq):
    qpos = q0 + lax.broadcasted_iota(I32, (1, tq), 1)
    return ((qpos >> CHUNK_SHIFT) + 1) << CHUNK_SHIFT


def _bit_planes(words):
    x = list(words)
    mask, j = 0x0000FFFF, 16
    while j:
        k = 0
        while k < 32:
            t = (x[k] ^ lax.shift_right_logical(x[k + j], jnp.int32(j))) & mask
            x[k] = x[k] ^ t
            x[k + j] = x[k + j] ^ (t << j)
            k = (k + j + 1) & ~j
        j >>= 1
        mask ^= (mask << j) & 0xFFFFFFFF
    return x


def _sublane_prefix(x):
    sub = lax.broadcasted_iota(I32, x.shape, 0)
    for sh in (1, 2, 4):
        x = x + jnp.where(sub >= sh, pltpu.roll(x, sh, axis=0), 0)
    return x


def _sel_body(qi_ref, w_ref, ki_ref, sel_ref, planes_ref, cand_ref, gt_ref, *, tq, grp):
    i = pl.program_id(0)
    nt = i + 1
    ng = (nt + grp - 1) // grp
    w = w_ref[...]
    limit = _chunk_limit(i * tq, tq)
    rows = lax.broadcasted_iota(I32, (tq, tq), 0)
    assert tq == 8 * 32

    def fill(t, diagonal):
        r0 = pl.multiple_of(t * tq, tq)
        b = pltpu.bitcast(_idx_scores_t(ki_ref[pl.ds(r0, tq), :], qi_ref, w), I32)
        u = b ^ ((b >> 31) | INT_MIN)
        if diagonal:
            u = jnp.where(rows + r0 < limit, u, 0)
        planes = _bit_planes([u[8 * k:8 * k + 8, :] for k in range(32)])
        at = pl.ds(pl.multiple_of(t * 8, 8), 8)
        for p in range(32):
            planes_ref[p, at, :] = planes[p]
        cand_ref[at, :] = jnp.full((8, tq), -1, I32)
        gt_ref[at, :] = jnp.zeros((8, tq), I32)

    def fill_full(t, c):
        fill(t, False)
        return c

    lax.fori_loop(0, nt - 1, fill_full, 0)
    fill(nt - 1, True)

    def pad(t, c):
        at = pl.ds(pl.multiple_of(t * 8, 8), 8)
        planes_ref[:, at, :] = jnp.zeros((32, 8, tq), I32)
        cand_ref[at, :] = jnp.zeros((8, tq), I32)
        gt_ref[at, :] = jnp.zeros((8, tq), I32)
        return c

    lax.fori_loop(nt, ng * grp, pad, 0)

    def sweep(p_prev, flip, p_next):
        def group(g, acc):
            at = pl.ds(pl.multiple_of(g * (grp * 8), grp * 8), grp * 8)
            c = cand_ref[at, :]
            if p_prev is not None:
                prev = planes_ref[p_prev, at, :]
                gt_ref[at, :] = gt_ref[at, :] | (c & prev & flip)
                c = c & (prev ^ flip)
                cand_ref[at, :] = c
            if p_next is not None:
                hit = lax.population_count(c & planes_ref[p_next, at, :])
                acc = acc + jnp.sum(hit.reshape(grp, 8, tq), axis=0)
            return acc
        acc = lax.fori_loop(0, ng, group, jnp.zeros((8, tq), I32))
        return jnp.sum(acc, axis=0, keepdims=True)

    def decide(p, cnt, kth_u, need):
        one = cnt >= need
        kth_u = jnp.where(one, kth_u | lax.shift_right_logical(jnp.int32(INT_MIN), p), kth_u)
        return kth_u, jnp.where(one, need, need - cnt), jnp.where(one, 0, -1)

    zero = jnp.zeros((1, tq), I32)
    kth_u, need, flip = decide(jnp.int32(0), sweep(None, None, 0), zero, jnp.full((1, tq), TOPK, I32))

    def bit_step(p, carry):
        kth_u, need, flip = carry
        return decide(p, sweep(p - 1, flip, p), kth_u, need)

    kth_u, need, flip = lax.fori_loop(1, 32, bit_step, (kth_u, need, flip))
    sweep(31, flip, None)
    need = jnp.where(kth_u == 0, 0, need)

    def tie_count(t):
        c = cand_ref[pl.ds(pl.multiple_of(t * 8, 8), 8), :]
        return c, jnp.sum(lax.population_count(c), axis=0, keepdims=True)

    def find(t, carry):
        seen, part, words = carry
        c, n = tie_count(t)
        hit = (seen < need) & (seen + n > need)
        return seen + n, jnp.where(hit, need - seen, part), jnp.where(hit, c, words)

    _, part, words = lax.fori_loop(0, nt, find, (zero, zero, jnp.zeros((8, tq), I32)))
    keep = jnp.zeros((8, tq), I32)
    before = zero
    for k in range(32):
        bit = lax.shift_right_logical(words, jnp.int32(31 - k)) & 1
        rank = before + _sublane_prefix(bit)
        keep = keep | jnp.where((bit == 1) & (rank <= part), INT_MIN if k == 0 else 1 << (31 - k), 0)
        before = rank[7:8, :]

    def emit(t, seen):
        c, n = tie_count(t)
        at = pl.ds(pl.multiple_of(t * 8, 8), 8)
        whole = seen + n <= need
        partial = (seen < need) & jnp.logical_not(whole)
        sel_ref[0, at, :] = gt_ref[at, :] | jnp.where(whole, c, jnp.where(partial, keep, 0))
        return seen + n

    lax.fori_loop(0, nt, emit, zero)

    def clear(t, c):
        sel_ref[0, pl.ds(pl.multiple_of(t * 8, 8), 8), :] = jnp.zeros((8, tq), I32)
        return c

    lax.fori_loop(nt, sel_ref.shape[1] // 8, clear, 0)


def _idx_sel(qi_hm, w_t, ki, tq):
    s = ki.shape[0]
    nrow = s // tq * 8
    return pl.pallas_call(
        functools.partial(_sel_body, tq=tq, grp=_pick(s // tq, 16)),
        grid=(s // tq,),
        in_specs=[pl.BlockSpec((N_IDX_HEADS, tq, IDX_DIM), lambda i: (0, i, 0)),
                  pl.BlockSpec((N_IDX_HEADS, tq), lambda i: (0, i)),
                  pl.BlockSpec((s, IDX_DIM), lambda i: (0, 0))],
        out_specs=pl.BlockSpec((1, nrow, tq), lambda i: (i, 0, 0)),
        out_shape=jax.ShapeDtypeStruct((s // tq, nrow, tq), I32),
        scratch_shapes=[pltpu.VMEM((32, nrow, tq), I32), pltpu.VMEM((nrow, tq), I32),
                        pltpu.VMEM((nrow, tq), I32)],
        compiler_params=_params(("arbitrary",)),
        name="idx_sel",
    )(qi_hm, w_t, ki)


def _attn_body(qb_ref, kb_ref, q_ref, k_ref, vt_ref, sel_ref,
               out_ref, m_ref, acc_ref, s_ref, p_ref, bias_ref, *, tq, tk):
    step = pl.program_id(0)
    i = qb_ref[step]
    j = kb_ref[step]

    @pl.when(j == 0)
    def _():
        m_ref[...] = jnp.full(m_ref.shape, M_INIT, F32)
        acc_ref[...] = jnp.zeros(acc_ref.shape, F32)

    for tile in range(tk // tq):
        words = sel_ref[0, tile * 8:(tile + 1) * 8, :]
        for k in range(32):
            bias_ref[pl.ds(tile * tq + 8 * k, 8), :] = jnp.where((words << k) < 0, 0.0, NEG_INF)
    bias = bias_ref[...]
    tops = []
    for h in range(N_HEADS):
        s = _nt(k_ref[h], q_ref[h]) + bias
        s_ref[h] = s
        tops.append(jnp.max(s, axis=0, keepdims=True))
    m_old = m_ref[...]
    m_new = jnp.maximum(m_old, jnp.concatenate(tops, axis=0))
    alpha = jnp.exp2(m_old - m_new)
    for h in range(N_HEADS):
        p_ref[h] = jnp.exp2(s_ref[h] - m_new[h:h + 1, :]).astype(BF16)
    m_ref[...] = m_new
    for h in range(N_HEADS):
        acc_ref[h] = alpha[h:h + 1, :] * acc_ref[h] + _dot(vt_ref[h], p_ref[h])

    @pl.when(j == ((i + 1) * tq - 1) // tk)
    def _():
        o = jnp.concatenate([acc_ref[h, 0:HEAD_DIM, :] / acc_ref[h, HEAD_DIM:HEAD_DIM + 1, :]
                             for h in range(N_HEADS)], axis=0)
        out_ref[...] = o.T.astype(BF16)


def _attn(q_hm, k_hm, vt_hm, sel, tq, tk):
    s = q_hm.shape[1]
    assert tk % tq == 0
    qb, kb = [], []
    for i in range(s // tq):
        for j in range(((i + 1) * tq - 1) // tk + 1):
            qb.append(i)
            kb.append(j)
    qb = jnp.asarray(qb, I32)
    kb = jnp.asarray(kb, I32)
    grid_spec = pltpu.PrefetchScalarGridSpec(
        num_scalar_prefetch=2,
        grid=(int(qb.shape[0]),),
        in_specs=[pl.BlockSpec((N_HEADS, tq, HEAD_DIM), lambda t, qb, kb: (0, qb[t], 0)),
                  pl.BlockSpec((N_HEADS, tk, HEAD_DIM), lambda t, qb, kb: (0, kb[t], 0)),
                  pl.BlockSpec((N_HEADS, PV_ROWS, tk), lambda t, qb, kb: (0, 0, kb[t])),
                  pl.BlockSpec((1, tk // tq * 8, tq), lambda t, qb, kb: (qb[t], kb[t], 0))],
        out_specs=pl.BlockSpec((tq, ATT_WIDTH), lambda t, qb, kb: (qb[t], 0)),
        scratch_shapes=[pltpu.VMEM((N_HEADS, tq), F32),
                        pltpu.VMEM((N_HEADS, PV_ROWS, tq), F32),
                        pltpu.VMEM((N_HEADS, tk, tq), F32),
                        pltpu.VMEM((N_HEADS, tk, tq), BF16),
                        pltpu.VMEM((tk, tq), F32)],
    )
    return pl.pallas_call(
        functools.partial(_attn_body, tq=tq, tk=tk),
        grid_spec=grid_spec,
        out_shape=jax.ShapeDtypeStruct((s, ATT_WIDTH), BF16),
        compiler_params=_params(("arbitrary",)),
        name="attn",
    )(qb, kb, q_hm, k_hm, vt_hm, sel)


NEW_PAD = 128


def _sample_attn_body(q_ref, qi_ref, w_ref, kic_ref, kin_ref, kc_ref, vc_ref, kn_ref, vn_ref,
                      out_ref, kinp, knp, vnp, *, past, tn):
    kinp[...] = jnp.zeros(kinp.shape, BF16)
    kinp[0:tn, :] = kin_ref[0].astype(BF16)
    knp[...] = jnp.zeros(knp.shape, BF16)
    vnp[...] = jnp.zeros(vnp.shape, BF16)
    for h in range(N_HEADS):
        knp[h, 0:tn, :] = kn_ref[0, h]
        vnp[h, 0:tn, :] = vn_ref[0, h]

    w = w_ref[0]
    kic = kic_ref[0].astype(BF16)
    kin = kinp[...]
    sc = None
    sn = None
    for h in range(N_IDX_HEADS):
        qih = qi_ref[0, h]
        wc = w[:, h:h + 1]
        tc = wc * jnp.maximum(_nt(qih, kic), 0.0)
        tnw = wc * jnp.maximum(_nt(qih, kin), 0.0)
        sc = tc if sc is None else sc + tc
        sn = tnw if sn is None else sn + tnw
    new_ok = lax.broadcasted_iota(I32, (tn, NEW_PAD), 1) < tn
    keyc = _order_key(sc)
    keyn = jnp.where(new_ok, _order_key(sn), INT_MIN)

    def count(pc, pn):
        return (jnp.sum(jnp.where(pc, 1, 0), axis=1, keepdims=True)
                + jnp.sum(jnp.where(pn, 1, 0), axis=1, keepdims=True))

    def bit_step(p, u):
        cand_u = u | (jnp.int32(1) << (31 - p))
        cand = cand_u ^ INT_MIN
        return jnp.where(count(keyc >= cand, keyn >= cand) >= TOPK, cand_u, u)

    kth = lax.fori_loop(0, 32, bit_step, jnp.zeros((tn, 1), I32)) ^ INT_MIN
    need = (TOPK - count(keyc > kth, keyn > kth)).astype(F32)

    cw = 256
    tri = jnp.where(lax.broadcasted_iota(I32, (cw, cw), 0) <= lax.broadcasted_iota(I32, (cw, cw), 1),
                    1.0, 0.0).astype(BF16)
    seen = jnp.zeros((tn, 1), F32)
    bias_c = []
    for c in range(past // cw):
        kc_ = keyc[:, c * cw:(c + 1) * cw]
        eq = kc_ == kth
        pre = _dot(jnp.where(eq, 1.0, 0.0).astype(BF16), tri)
        sel = (kc_ > kth) | (eq & (pre + seen <= need))
        bias_c.append(jnp.where(sel, 0.0, NEG_INF))
        seen = seen + pre[:, cw - 1:cw]
    bias_c = jnp.concatenate(bias_c, axis=1)
    eqn = keyn == kth
    pren = _dot(jnp.where(eqn, 1.0, 0.0).astype(BF16), tri[:NEW_PAD, :NEW_PAD])
    bias_n = jnp.where(new_ok & ((keyn > kth) | (eqn & (pren + seen <= need))), 0.0, NEG_INF)

    kc = kc_ref[0].astype(BF16)
    vc = vc_ref[0].astype(BF16)
    for h in range(N_HEADS):
        qh = q_ref[0, h]
        head = slice(h * HEAD_DIM, (h + 1) * HEAD_DIM)
        s_c = _nt(qh, kc[:, head]) + bias_c
        s_n = _nt(qh, knp[h]) + bias_n
        m = jnp.maximum(jnp.max(s_c, axis=1, keepdims=True), jnp.max(s_n, axis=1, keepdims=True))
        p_c = jnp.exp2(s_c - m)
        p_n = jnp.exp2(s_n - m)
        l = jnp.sum(p_c, axis=1, keepdims=True) + jnp.sum(p_n, axis=1, keepdims=True)
        o = _dot(p_c.astype(BF16), vc[:, head]) + _dot(p_n.astype(BF16), vnp[h])
        out_ref[0, h] = (o / l).astype(BF16)


def _sample_attn(q_hm, qi_hm, w, kic, kin, kc, vc, kn_hm, vn_hm):
    nb, _, tn, _ = q_hm.shape
    past = kic.shape[1]
    b4 = lambda a: pl.BlockSpec((1,) + a.shape[1:], lambda b: (b, 0, 0, 0))
    b3 = lambda a: pl.BlockSpec((1,) + a.shape[1:], lambda b: (b, 0, 0))
    return pl.pallas_call(
        functools.partial(_sample_attn_body, past=past, tn=tn),
        grid=(nb,),
        in_specs=[b4(q_hm), b4(qi_hm), b3(w), b3(kic), b3(kin), b3(kc), b3(vc), b4(kn_hm), b4(vn_hm)],
        out_specs=pl.BlockSpec((1, N_HEADS, tn, HEAD_DIM), lambda b: (b, 0, 0, 0)),
        out_shape=jax.ShapeDtypeStruct((nb, N_HEADS, tn, HEAD_DIM), BF16),
        scratch_shapes=[pltpu.VMEM((NEW_PAD, IDX_DIM), BF16),
                        pltpu.VMEM((N_HEADS, NEW_PAD, HEAD_DIM), BF16),
                        pltpu.VMEM((N_HEADS, NEW_PAD, HEAD_DIM), BF16)],
        compiler_params=_params(("parallel",)),
        name="sample_attn",
    )(q_hm, qi_hm, w, kic, kin, kc, vc, kn_hm, vn_hm)


def _cmpx(a, b):
    return jnp.maximum(a, b), jnp.minimum(a, b)


def _bitonic_merge_desc(v):
    n = len(v)
    v = list(v)
    j = n // 2
    while j >= 1:
        for i in range(n):
            l = i ^ j
            if l > i:
                v[i], v[l] = _cmpx(v[i], v[l])
        j //= 2
    return v


def _bitonic_sort_desc(v):
    n = len(v)
    v = list(v)
    k = 2
    while k <= n:
        j = k // 2
        while j >= 1:
            for i in range(n):
                l = i ^ j
                if l > i:
                    hi, lo = _cmpx(v[i], v[l])
                    v[i], v[l] = (hi, lo) if (i & k) == 0 else (lo, hi)
            j //= 2
        k *= 2
    return v


def _merge_top(a, b):
    n = len(a)
    return _bitonic_merge_desc([jnp.maximum(a[k], b[n - 1 - k]) for k in range(n)])


def _top16_desc(vals):
    groups = [_bitonic_sort_desc(vals[g:g + PEER_TOPK]) for g in range(0, len(vals), PEER_TOPK)]
    while len(groups) > 1:
        groups = [_merge_top(groups[g], groups[g + 1]) for g in range(0, len(groups), 2)]
    return groups[0]


def _mid_body(x_ref, att_ref, conv_ref, woa_ref, woc_ref, g_ref, wqt_ref, a1_ref, a2_ref,
              h_ref, xn_ref, qht_ref, st_ref):
    h = x_ref[...] + _dot(att_ref[...], woa_ref[...]) + _dot(conv_ref[...], woc_ref[...])
    h_ref[...] = h
    xn = _rms(h, g_ref[...]).astype(BF16)
    xn_ref[...] = xn
    qht = _nt(wqt_ref[...], xn).astype(BF16)
    qht_ref[...] = qht
    half = PEER_HEADS * PEER_HALF
    s1 = _dot(a1_ref[...], qht[:half])
    s2 = _dot(a2_ref[...], qht[half:])
    rows = lambda s: [s[r * PEER_HEADS:(r + 1) * PEER_HEADS, :] for r in range(PEER_KEYS)]
    v1 = _top16_desc(rows(s1))
    v2 = _top16_desc(rows(s2))
    ninf = jnp.full(v1[0].shape, NEG_INF, F32)
    top = [v1[0] + v2[b] for b in range(PEER_TOPK)]
    for a in range(1, PEER_TOPK):
        n_a = PEER_TOPK // (a + 1)
        top = _merge_top(top, [v1[a] + v2[b] if b < n_a else ninf for b in range(PEER_TOPK)])
    z = jnp.ones_like(top[0])
    for k in range(1, PEER_TOPK):
        z = z + jnp.exp(top[k] - top[0])
    st_ref[0:8, :] = v1[0]
    st_ref[8:16, :] = v2[0]
    st_ref[16:24, :] = top[PEER_TOPK - 1]
    st_ref[24:32, :] = 1.0 / z


def _mid(x, att, conv, woa, woc, g, wqt, a1, a2, tb):
    t = x.shape[0]
    row = lambda w: pl.BlockSpec((tb, w), lambda i: (i, 0))
    col = lambda r: pl.BlockSpec((r, tb), lambda i: (0, i))
    full = lambda a: pl.BlockSpec(a.shape, lambda i: (0, 0))
    return pl.pallas_call(
        _mid_body,
        grid=(t // tb,),
        in_specs=[row(D_MODEL), row(ATT_WIDTH), row(CONV_CH), full(woa), full(woc), full(g),
                  full(wqt), full(a1), full(a2)],
        out_specs=[row(D_MODEL), row(D_MODEL), col(D_MODEL), col(32)],
        out_shape=[jax.ShapeDtypeStruct((t, D_MODEL), F32),
                   jax.ShapeDtypeStruct((t, D_MODEL), BF16),
                   jax.ShapeDtypeStruct((D_MODEL, t), BF16),
                   jax.ShapeDtypeStruct((32, t), F32)],
        compiler_params=_params(("parallel",)),
        name="mid",
    )(x, att, conv, woa, woc, g, wqt, a1, a2)


def _peer_body(xn_ref, qht_ref, st_ref, a1_ref, a2_ref, u_ref, vt_ref, h_ref, g_ref, y_ref,
               s1_ref, s2_ref, e1_ref, e2_ref, hid_ref, out_ref, *, eb):
    e = pl.program_id(1)
    nsub = eb // PEER_KEYS
    half = PEER_HEADS * PEER_HALF

    @pl.when(e == 0)
    def _():
        qht = qht_ref[...]
        s1 = _dot(a1_ref[...], qht[:half])
        s2 = _dot(a2_ref[...], qht[half:])
        s1_ref[...] = s1
        s2_ref[...] = s2
        for h in range(PEER_HEADS):
            r = slice(h * PEER_KEYS, (h + 1) * PEER_KEYS)
            e1_ref[r, :] = jnp.exp(s1[r] - st_ref[h:h + 1, :])
            e2_ref[r, :] = jnp.exp(s2[r] - st_ref[8 + h:9 + h, :]) * st_ref[24 + h:25 + h, :] * 0.5
        out_ref[...] = jnp.zeros(out_ref.shape, F32)

    xn = xn_ref[...]
    for ii in range(nsub):
        i = e * nsub + ii
        a = _nt(u_ref[ii * PEER_KEYS:(ii + 1) * PEER_KEYS, :], xn)
        gate = None
        for h in range(PEER_HEADS):
            r = slice(h * PEER_KEYS, (h + 1) * PEER_KEYS)
            pair = s1_ref[pl.ds(h * PEER_KEYS + i, 1), :] + s2_ref[r, :]
            g = e1_ref[pl.ds(h * PEER_KEYS + i, 1), :] * e2_ref[r, :]
            g = jnp.where(pair >= st_ref[16 + h:17 + h, :], g, 0.0)
            gate = g if gate is None else gate + g
        hid = a * (1.0 + lax.erf(a * INV_SQRT2)) * gate
        hid_ref[ii * PEER_KEYS:(ii + 1) * PEER_KEYS, :] = hid.astype(BF16)
    out_ref[...] += _dot(vt_ref[...], hid_ref[...])

    @pl.when(e == pl.num_programs(1) - 1)
    def _():
        y_ref[...] = _rms(h_ref[...] + out_ref[...].T, g_ref[...])


def _peer(xn, qht, st, a1, a2, u, vt, h, g, tb, eb):
    t = xn.shape[0]
    ne = u.shape[0] // eb
    full = lambda a: pl.BlockSpec(a.shape, lambda i, e: (0, 0))
    return pl.pallas_call(
        functools.partial(_peer_body, eb=eb),
        grid=(t // tb, ne),
        in_specs=[pl.BlockSpec((tb, D_MODEL), lambda i, e: (i, 0)),
                  pl.BlockSpec((D_MODEL, tb), lambda i, e: (0, i)),
                  pl.BlockSpec((32, tb), lambda i, e: (0, i)),
                  full(a1), full(a2),
                  pl.BlockSpec((eb, D_MODEL), lambda i, e: (e, 0)),
                  pl.BlockSpec((D_MODEL, eb), lambda i, e: (0, e)),
                  pl.BlockSpec((tb, D_MODEL), lambda i, e: (i, 0)),
                  full(g)],
        out_specs=pl.BlockSpec((tb, D_MODEL), lambda i, e: (i, 0)),
        out_shape=jax.ShapeDtypeStruct((t, D_MODEL), F32),
        scratch_shapes=[pltpu.VMEM((PEER_HEADS * PEER_KEYS, tb), F32)] * 4
                       + [pltpu.VMEM((eb, tb), BF16), pltpu.VMEM((D_MODEL, tb), F32)],
        compiler_params=_params(("parallel", "arbitrary")),
        name="peer",
    )(xn, qht, st, a1, a2, u, vt, h, g)


def _pick(n, pref):
    b = min(n, pref)
    while n % b:
        b //= 2
    return b


def _token_stage(x, att, conv, p, tb_mid, tb_peer):
    h, xn, qht, st = _mid(x, att, conv, p["woa"], p["woc"], p["ffn_g"], p["wqt"], p["a1s"], p["a2s"], tb_mid)
    return _peer(xn, qht, st, p["a1d"], p["a2d"], p["u"], p["vt"], h, p["final_g"], tb_peer, 2048)


def kernel(x_prompt, x_sample, cache_k, cache_v, cache_kidx, state_conv, attn_norm_g, w_in, conv_w, conv_b,
           conv_ln_g, conv_ln_b, w_out, ffn_norm_g, peer_wq, peer_subkeys, peer_u, peer_v, final_norm_g):
    assert w_in.shape[0] == 1, "single layer"
    nbp, s, _ = x_prompt.shape
    nb, tn, _ = x_sample.shape
    assert nbp == 1
    past = cache_k.shape[2]

    wi = w_in[0]
    o_qi = 3 * ATT_WIDTH
    o_ki = o_qi + N_IDX_HEADS * IDX_DIM
    o_wi = o_ki + IDX_DIM
    o_glu = o_wi + N_IDX_HEADS
    wqkv = wi[:, :o_qi].astype(BF16)
    widx = jnp.concatenate([wi[:, o_qi:o_glu], jnp.zeros((D_MODEL, 128 - IDX_DIM - N_IDX_HEADS), F32)],
                           axis=1).astype(BF16)
    wglu = wi[:, o_glu:].astype(BF16)
    g_attn = attn_norm_g[0][None, :]
    cw = jnp.concatenate([conv_w[0], jnp.zeros((CONV_TAIL - CONV_WIDTH, CONV_CH), F32)], axis=0)
    cb, cg, cbeta = conv_b[0][None, :], conv_ln_g[0][None, :], conv_ln_b[0][None, :]
    eye = jnp.eye(PEER_HEADS, dtype=F32)
    sk = peer_subkeys[0]
    a_s = [jnp.einsum("id,hg->ihgd", sk[c], eye).reshape(PEER_KEYS * PEER_HEADS, PEER_HEADS * PEER_HALF)
           .astype(BF16) for c in range(2)]
    a_d = [jnp.einsum("id,hg->higd", sk[c], eye).reshape(PEER_KEYS * PEER_HEADS, PEER_HEADS * PEER_HALF)
           .astype(BF16) for c in range(2)]
    wq = peer_wq[0].reshape(D_MODEL, PEER_HEADS, 2, PEER_HALF).transpose(2, 1, 3, 0)
    p = {
        "woa": w_out[0][:ATT_WIDTH].astype(BF16), "woc": w_out[0][ATT_WIDTH:].astype(BF16),
        "ffn_g": ffn_norm_g[0][None, :], "final_g": final_norm_g[None, :],
        "wqt": wq.reshape(D_MODEL, D_MODEL).astype(BF16),
        "a1s": a_s[0], "a2s": a_s[1], "a1d": a_d[0], "a2d": a_d[1],
        "u": peer_u[0].astype(BF16), "vt": peer_v[0].T.astype(BF16),
    }
    w_scale = (N_IDX_HEADS ** -0.5) * (IDX_DIM ** -0.5)

    xp = x_prompt[0]
    tb = _pick(s, 512)
    k, v, kw, q_hm, k_hm, vt_hm, qi_hm, conv_p, tail_p = _in_proj_conv(
        xp, g_attn, wqkv, widx, wglu, jnp.zeros((1, CONV_TAIL, CONV_CH), F32), cw, cb, cg, cbeta, tb)
    ki = kw[:, :IDX_DIM]
    ki_b = ki.astype(BF16)
    w_t = (kw[:, IDX_DIM:IDX_DIM + N_IDX_HEADS] * w_scale).T
    tq = _pick(s, 256)
    sel = _idx_sel(qi_hm, w_t, ki_b, tq)
    att_p = _attn(q_hm, k_hm, vt_hm, sel, tq, _pick(s, 1024))
    y_p = _token_stage(xp, att_p, conv_p, p, tb, _pick(s, 512))

    xs = x_sample.reshape(nb * tn, D_MODEL)
    ts = nb * tn
    tbs = _pick(ts, 256)
    ks, vs, kws, glus, qs_hm, _, _, qis_hm = _in_proj(xs, g_attn, wqkv, widx, wglu, tbs)
    pad = jnp.zeros((nb, CONV_TAIL - (CONV_WIDTH - 1), CONV_CH), F32)
    conv_s, tail_s = _conv(glus, jnp.concatenate([pad, state_conv[0]], axis=1), cw, cb, cg, cbeta, nb, tn)
    hm4 = lambda a, nh, dt=BF16: a.reshape(nb, tn, nh, a.shape[1] // nh).transpose(0, 2, 1, 3).astype(dt)
    per_stream = lambda a: a.reshape(a.shape[0], nb, tn, a.shape[2]).transpose(1, 0, 2, 3)
    kis = kws[:, :IDX_DIM]
    att_s = _sample_attn(
        per_stream(qs_hm), per_stream(qis_hm),
        (kws[:, IDX_DIM:IDX_DIM + N_IDX_HEADS] * w_scale).reshape(nb, tn, N_IDX_HEADS),
        cache_kidx[0], kis.reshape(nb, tn, IDX_DIM),
        cache_k[0].reshape(nb, past, ATT_WIDTH), cache_v[0].reshape(nb, past, ATT_WIDTH),
        hm4(ks, N_HEADS), hm4(vs, N_HEADS))
    att_s = att_s.transpose(0, 2, 1, 3).reshape(ts, ATT_WIDTH)
    y_s = _token_stage(xs, att_s, conv_s, p, tbs, tbs)

    hd = (N_HEADS, HEAD_DIM)
    keep = CONV_TAIL - (CONV_WIDTH - 1)
    return (y_p[None], y_s.reshape(nb, tn, D_MODEL),
            k.reshape(1, 1, s, *hd), v.reshape(1, 1, s, *hd), ki[None, None], tail_p[None, :, keep:],
            ks.reshape(1, nb, tn, *hd), vs.reshape(1, nb, tn, *hd), kis.reshape(1, nb, tn, IDX_DIM),
            tail_s[None, :, keep:])
```

```python
import functools

import jax
import jax.numpy as jnp
from jax import lax
from jax.experimental import pallas as pl
from jax.experimental.pallas import tpu as pltpu

F32 = jnp.float32
BF16 = jnp.bfloat16
I32 = jnp.int32

D_MODEL = 1024
N_HEADS = 8
HEAD_DIM = 64
ATT_WIDTH = N_HEADS * HEAD_DIM
PV_ROWS = HEAD_DIM + 16
N_IDX_HEADS = 4
IDX_DIM = 64
TOPK = 256
CHUNK_SHIFT = 6
CONV_CH = 512
CONV_WIDTH = 31
CONV_TAIL = 32
PEER_HEADS = 8
PEER_KEYS = 128
PEER_HALF = 64
PEER_TOPK = 16
PAIR_SLACK = 2.0 ** -22
EPS = 1e-6
INT_MIN = -2147483648
ORDER_MASK = 0x7FFFFFFF
INV_SQRT2 = 0.7071067811865476
LOG2E = 1.4426950408889634
NEG_INF = float("-inf")
M_INIT = -1e30

VMEM_LIMIT = 56 * 1024 * 1024


def _params(sem, vmem=VMEM_LIMIT):
    return pltpu.CompilerParams(dimension_semantics=sem, vmem_limit_bytes=vmem)


def _nt(a, b):
    return lax.dot_general(a, b, (((1,), (1,)), ((), ())), preferred_element_type=F32)


def _dot(a, b):
    return jnp.dot(a, b, preferred_element_type=F32)


def _rms(x, g):
    return x * lax.rsqrt(jnp.mean(x * x, axis=-1, keepdims=True) + EPS) * g


def _order_key(x):
    b = pltpu.bitcast(x, I32)
    return b ^ ((b >> 31) & ORDER_MASK)


def _in_proj_body(x_ref, g_ref, wqkv_ref, widx_ref, wglu_ref,
                  k_ref, v_ref, kw_ref, glu_ref, qh_ref, kh_ref, vth_ref, qih_ref):
    glu_ref[...] = _project(x_ref, g_ref, wqkv_ref, widx_ref, wglu_ref,
                            k_ref, v_ref, kw_ref, qh_ref, kh_ref, vth_ref, qih_ref)


def _project(x_ref, g_ref, wqkv_ref, widx_ref, wglu_ref, k_ref, v_ref, kw_ref, qh_ref, kh_ref, vth_ref, qih_ref):
    xn = _rms(x_ref[...], g_ref[...]).astype(BF16)
    qkv = _dot(xn, wqkv_ref[...])
    q = qkv[:, :ATT_WIDTH] * (HEAD_DIM ** -0.5 * LOG2E)
    k = qkv[:, ATT_WIDTH:2 * ATT_WIDTH]
    v = qkv[:, 2 * ATT_WIDTH:]
    k_ref[...] = k
    v_ref[...] = v
    for h in range(N_HEADS):
        head = slice(h * HEAD_DIM, (h + 1) * HEAD_DIM)
        qh_ref[h] = q[:, head].astype(BF16)
        kh_ref[h] = k[:, head].astype(BF16)
    vth_ref[:, 0:HEAD_DIM, :] = v.T.reshape(N_HEADS, HEAD_DIM, v.shape[0]).astype(BF16)
    vth_ref[:, HEAD_DIM:, :] = jnp.ones((N_HEADS, PV_ROWS - HEAD_DIM, v.shape[0]), BF16)
    ix = _dot(xn, widx_ref[...])
    for h in range(N_IDX_HEADS):
        qih_ref[h] = ix[:, h * IDX_DIM:(h + 1) * IDX_DIM].astype(BF16)
    kw_ref[...] = ix[:, N_IDX_HEADS * IDX_DIM:]
    return _dot(xn, wglu_ref[...])


def _in_proj(x, g, wqkv, widx, wglu, tb):
    t = x.shape[0]
    row = lambda w: pl.BlockSpec((tb, w), lambda i: (i, 0))
    full = lambda a: pl.BlockSpec(a.shape, lambda i: (0, 0))
    heads = lambda n: pl.BlockSpec((n, tb, HEAD_DIM), lambda i: (0, i, 0))
    return pl.pallas_call(
        _in_proj_body,
        grid=(t // tb,),
        in_specs=[row(D_MODEL), full(g), full(wqkv), full(widx), full(wglu)],
        out_specs=[row(ATT_WIDTH), row(ATT_WIDTH), row(128), row(2 * CONV_CH),
                   heads(N_HEADS), heads(N_HEADS),
                   pl.BlockSpec((N_HEADS, PV_ROWS, tb), lambda i: (0, 0, i)), heads(N_IDX_HEADS)],
        out_shape=[jax.ShapeDtypeStruct((t, ATT_WIDTH), F32),
                   jax.ShapeDtypeStruct((t, ATT_WIDTH), F32),
                   jax.ShapeDtypeStruct((t, 128), F32),
                   jax.ShapeDtypeStruct((t, 2 * CONV_CH), F32),
                   jax.ShapeDtypeStruct((N_HEADS, t, HEAD_DIM), BF16),
                   jax.ShapeDtypeStruct((N_HEADS, t, HEAD_DIM), BF16),
                   jax.ShapeDtypeStruct((N_HEADS, PV_ROWS, t), BF16),
                   jax.ShapeDtypeStruct((N_IDX_HEADS, t, IDX_DIM), BF16)],
        compiler_params=_params(("parallel",)),
        name="in_proj",
    )(x, g, wqkv, widx, wglu)


def _conv_body(glu_ref, tail0_ref, w_ref, b_ref, g_ref, beta_ref, out_ref, tail_ref, ubuf, shifted, *, tb):
    _conv_core(glu_ref[...], pl.program_id(1), tail0_ref, w_ref, b_ref, g_ref, beta_ref,
               out_ref, tail_ref, ubuf, shifted, tb)


def _conv_core(glu, j, tail0_ref, w_ref, b_ref, g_ref, beta_ref, out_ref, tail_ref, ubuf, shifted, tb):
    @pl.when(j == 0)
    def _():
        ubuf[0:CONV_TAIL, :] = tail0_ref[0]

    @pl.when(j > 0)
    def _():
        ubuf[0:CONV_TAIL, :] = ubuf[tb:tb + CONV_TAIL, :]

    ubuf[CONV_TAIL:CONV_TAIL + tb, :] = glu[:, :CONV_CH] * jax.nn.sigmoid(glu[:, CONV_CH:])
    off = CONV_TAIL - (CONV_WIDTH - 1)
    acc = None
    for res in range(8):
        taps = [t for t in range(CONV_WIDTH) if (off + t) % 8 == res]
        lo, hi = (off + taps[0]) // 8, (off + taps[-1]) // 8
        n = 8 * (hi - lo) + tb
        shifted[0:n, :] = ubuf[res + 8 * lo:res + 8 * lo + n, :]
        for t in taps:
            a = 8 * ((off + t) // 8 - lo)
            term = shifted[a:a + tb, :] * w_ref[t:t + 1, :]
            acc = term if acc is None else acc + term
    y = acc + b_ref[...]
    mu = jnp.mean(y, axis=-1, keepdims=True)
    d = y - mu
    var = jnp.mean(d * d, axis=-1, keepdims=True)
    z = d * lax.rsqrt(var + EPS) * g_ref[...] + beta_ref[...]
    out_ref[...] = (z * jax.nn.sigmoid(z)).astype(BF16)
    tail_ref[0] = ubuf[tb:tb + CONV_TAIL, :]


def _conv(glu, tail0, w, b, g, beta, nseq, tb):
    t = glu.shape[0]
    nblk = t // (nseq * tb)
    full = lambda a: pl.BlockSpec(a.shape, lambda s, j: (0, 0))
    return pl.pallas_call(
        functools.partial(_conv_body, tb=tb),
        grid=(nseq, nblk),
        in_specs=[pl.BlockSpec((tb, 2 * CONV_CH), lambda s, j: (s * nblk + j, 0)),
                  pl.BlockSpec((1, CONV_TAIL, CONV_CH), lambda s, j: (s, 0, 0)),
                  full(w), full(b), full(g), full(beta)],
        out_specs=[pl.BlockSpec((tb, CONV_CH), lambda s, j: (s * nblk + j, 0)),
                   pl.BlockSpec((1, CONV_TAIL, CONV_CH), lambda s, j: (s, 0, 0))],
        out_shape=[jax.ShapeDtypeStruct((t, CONV_CH), BF16),
                   jax.ShapeDtypeStruct((nseq, CONV_TAIL, CONV_CH), F32)],
        scratch_shapes=[pltpu.VMEM((tb + CONV_TAIL, CONV_CH), F32)] * 2,
        compiler_params=_params(("parallel", "arbitrary")),
        name="conv",
    )(glu, tail0, w, b, g, beta)


def _in_proj_conv_body(x_ref, g_ref, wqkv_ref, widx_ref, wglu_ref, tail0_ref, cw_ref, cb_ref, cg_ref, cbeta_ref,
                       k_ref, v_ref, kw_ref, qh_ref, kh_ref, vth_ref, qih_ref, conv_ref, tail_ref,
                       ubuf, shifted, *, tb):
    glu = _project(x_ref, g_ref, wqkv_ref, widx_ref, wglu_ref, k_ref, v_ref, kw_ref, qh_ref, kh_ref, vth_ref, qih_ref)
    _conv_core(glu, pl.program_id(0), tail0_ref, cw_ref, cb_ref, cg_ref, cbeta_ref,
               conv_ref, tail_ref, ubuf, shifted, tb)


def _in_proj_conv(x, g, wqkv, widx, wglu, tail0, cw, cb, cg, cbeta, tb):
    t = x.shape[0]
    row = lambda w: pl.BlockSpec((tb, w), lambda i: (i, 0))
    full = lambda a: pl.BlockSpec(a.shape, lambda i: (0,) * a.ndim)
    heads = lambda n: pl.BlockSpec((n, tb, HEAD_DIM), lambda i: (0, i, 0))
    return pl.pallas_call(
        functools.partial(_in_proj_conv_body, tb=tb),
        grid=(t // tb,),
        in_specs=[row(D_MODEL), full(g), full(wqkv), full(widx), full(wglu),
                  full(tail0), full(cw), full(cb), full(cg), full(cbeta)],
        out_specs=[row(ATT_WIDTH), row(ATT_WIDTH), row(128),
                   heads(N_HEADS), heads(N_HEADS),
                   pl.BlockSpec((N_HEADS, PV_ROWS, tb), lambda i: (0, 0, i)), heads(N_IDX_HEADS),
                   row(CONV_CH), pl.BlockSpec((1, CONV_TAIL, CONV_CH), lambda i: (0, 0, 0))],
        out_shape=[jax.ShapeDtypeStruct((t, ATT_WIDTH), F32),
                   jax.ShapeDtypeStruct((t, ATT_WIDTH), F32),
                   jax.ShapeDtypeStruct((t, 128), F32),
                   jax.ShapeDtypeStruct((N_HEADS, t, HEAD_DIM), BF16),
                   jax.ShapeDtypeStruct((N_HEADS, t, HEAD_DIM), BF16),
                   jax.ShapeDtypeStruct((N_HEADS, PV_ROWS, t), BF16),
                   jax.ShapeDtypeStruct((N_IDX_HEADS, t, IDX_DIM), BF16),
                   jax.ShapeDtypeStruct((t, CONV_CH), BF16),
                   jax.ShapeDtypeStruct((1, CONV_TAIL, CONV_CH), F32)],
        scratch_shapes=[pltpu.VMEM((tb + CONV_TAIL, CONV_CH), F32)] * 2,
        compiler_params=_params(("arbitrary",)),
        name="in_proj_conv",
    )(x, g, wqkv, widx, wglu, tail0, cw, cb, cg, cbeta)


def _idx_scores_t(ki_tile, qi_ref, w):
    acc = None
    for h in range(N_IDX_HEADS):
        term = w[h:h + 1, :] * jnp.maximum(_nt(ki_tile, qi_ref[h]), 0.0)
        acc = term if acc is None else acc + term
    return acc


def _chunk_limit(q0, tq):
    qpos = q0 + lax.broadcasted_iota(I32, (1, tq), 1)
    return ((qpos >> CHUNK_SHIFT) + 1) << CHUNK_SHIFT


def _bit_planes(words):
    x = list(words)
    mask, j = 0x0000FFFF, 16
    while j:
        k = 0
        while k < 32:
            t = (x[k] ^ lax.shift_right_logical(x[k + j], jnp.int32(j))) & mask
            x[k] = x[k] ^ t
            x[k + j] = x[k + j] ^ (t << j)
            k = (k + j + 1) & ~j
        j >>= 1
        mask ^= (mask << j) & 0xFFFFFFFF
    return x


def _sublane_prefix(x):
    sub = lax.broadcasted_iota(I32, x.shape, 0)
    for sh in (1, 2, 4):
        x = x + jnp.where(sub >= sh, pltpu.roll(x, sh, axis=0), 0)
    return x


def _sel_body(qi_ref, w_ref, ki_ref, sel_ref, planes_ref, cand_ref, gt_ref, *, tq, grp):
    i = pl.program_id(0)
    nt = i + 1
    ng = (nt + grp - 1) // grp
    w = w_ref[...]
    limit = _chunk_limit(i * tq, tq)
    rows = lax.broadcasted_iota(I32, (tq, tq), 0)
    assert tq == 8 * 32

    def fill(t, diagonal):
        r0 = pl.multiple_of(t * tq, tq)
        b = pltpu.bitcast(_idx_scores_t(ki_ref[pl.ds(r0, tq), :], qi_ref, w), I32)
        u = b ^ ((b >> 31) | INT_MIN)
        if diagonal:
            u = jnp.where(rows + r0 < limit, u, 0)
        planes = _bit_planes([u[8 * k:8 * k + 8, :] for k in range(32)])
        at = pl.ds(pl.multiple_of(t * 8, 8), 8)
        for p in range(32):
            planes_ref[p, at, :] = planes[p]
        cand_ref[at, :] = jnp.full((8, tq), -1, I32)
        gt_ref[at, :] = jnp.zeros((8, tq), I32)

    def fill_full(t, c):
        fill(t, False)
        return c

    lax.fori_loop(0, nt - 1, fill_full, 0)
    fill(nt - 1, True)

    def pad(t, c):
        at = pl.ds(pl.multiple_of(t * 8, 8), 8)
        planes_ref[:, at, :] = jnp.zeros((32, 8, tq), I32)
        cand_ref[at, :] = jnp.zeros((8, tq), I32)
        gt_ref[at, :] = jnp.zeros((8, tq), I32)
        return c

    lax.fori_loop(nt, ng * grp, pad, 0)

    def sweep(p_prev, flip, p_next):
        def group(g, acc):
            at = pl.ds(pl.multiple_of(g * (grp * 8), grp * 8), grp * 8)
            c = cand_ref[at, :]
            if p_prev is not None:
                prev = planes_ref[p_prev, at, :]
                gt_ref[at, :] = gt_ref[at, :] | (c & prev & flip)
                c = c & (prev ^ flip)
                cand_ref[at, :] = c
            if p_next is not None:
                hit = lax.population_count(c & planes_ref[p_next, at, :])
                acc = acc + jnp.sum(hit.reshape(grp, 8, tq), axis=0)
            return acc
        acc = lax.fori_loop(0, ng, group, jnp.zeros((8, tq), I32))
        return jnp.sum(acc, axis=0, keepdims=True)

    def decide(p, cnt, kth_u, need):
        one = cnt >= need
        kth_u = jnp.where(one, kth_u | lax.shift_right_logical(jnp.int32(INT_MIN), p), kth_u)
        return kth_u, jnp.where(one, need, need - cnt), jnp.where(one, 0, -1)

    zero = jnp.zeros((1, tq), I32)
    kth_u, need, flip = decide(jnp.int32(0), sweep(None, None, 0), zero, jnp.full((1, tq), TOPK, I32))

    def bit_step(p, carry):
        kth_u, need, flip = carry
        return decide(p, sweep(p - 1, flip, p), kth_u, need)

    kth_u, need, flip = lax.fori_loop(1, 32, bit_step, (kth_u, need, flip))
    sweep(31, flip, None)
    need = jnp.where(kth_u == 0, 0, need)

    def tie_count(t):
        c = cand_ref[pl.ds(pl.multiple_of(t * 8, 8), 8), :]
        return c, jnp.sum(lax.population_count(c), axis=0, keepdims=True)

    def find(t, carry):
        seen, part, words = carry
        c, n = tie_count(t)
        hit = (seen < need) & (seen + n > need)
        return seen + n, jnp.where(hit, need - seen, part), jnp.where(hit, c, words)

    _, part, words = lax.fori_loop(0, nt, find, (zero, zero, jnp.zeros((8, tq), I32)))
    keep = jnp.zeros((8, tq), I32)
    before = zero
    for k in range(32):
        bit = lax.shift_right_logical(words, jnp.int32(31 - k)) & 1
        rank = before + _sublane_prefix(bit)
        keep = keep | jnp.where((bit == 1) & (rank <= part), INT_MIN if k == 0 else 1 << (31 - k), 0)
        before = rank[7:8, :]

    def emit(t, seen):
        c, n = tie_count(t)
        at = pl.ds(pl.multiple_of(t * 8, 8), 8)
        whole = seen + n <= need
        partial = (seen < need) & jnp.logical_not(whole)
        sel_ref[0, at, :] = gt_ref[at, :] | jnp.where(whole, c, jnp.where(partial, keep, 0))
        return seen + n

    lax.fori_loop(0, nt, emit, zero)

    def clear(t, c):
        sel_ref[0, pl.ds(pl.multiple_of(t * 8, 8), 8), :] = jnp.zeros((8, tq), I32)
        return c

    lax.fori_loop(nt, sel_ref.shape[1] // 8, clear, 0)


def _idx_sel(qi_hm, w_t, ki, tq):
    s = ki.shape[0]
    nrow = s // tq * 8
    return pl.pallas_call(
        functools.partial(_sel_body, tq=tq, grp=_pick(s // tq, 16)),
        grid=(s // tq,),
        in_specs=[pl.BlockSpec((N_IDX_HEADS, tq, IDX_DIM), lambda i: (0, i, 0)),
                  pl.BlockSpec((N_IDX_HEADS, tq), lambda i: (0, i)),
                  pl.BlockSpec((s, IDX_DIM), lambda i: (0, 0))],
        out_specs=pl.BlockSpec((1, nrow, tq), lambda i: (i, 0, 0)),
        out_shape=jax.ShapeDtypeStruct((s // tq, nrow, tq), I32),
        scratch_shapes=[pltpu.VMEM((32, nrow, tq), I32), pltpu.VMEM((nrow, tq), I32),
                        pltpu.VMEM((nrow, tq), I32)],
        compiler_params=_params(("arbitrary",)),
        name="idx_sel",
    )(qi_hm, w_t, ki)


def _attn_body(qb_ref, kb_ref, q_ref, k_ref, vt_ref, sel_ref,
               out_ref, m_ref, acc_ref, s_ref, p_ref, bias_ref, *, tq, tk):
    step = pl.program_id(0)
    i = qb_ref[step]
    j = kb_ref[step]

    @pl.when(j == 0)
    def _():
        m_ref[...] = jnp.full(m_ref.shape, M_INIT, F32)
        acc_ref[...] = jnp.zeros(acc_ref.shape, F32)

    for tile in range(tk // tq):
        words = sel_ref[0, tile * 8:(tile + 1) * 8, :]
        for k in range(32):
            bias_ref[pl.ds(tile * tq + 8 * k, 8), :] = jnp.where((words << k) < 0, 0.0, NEG_INF)
    bias = bias_ref[...]
    tops = []
    for h in range(N_HEADS):
        s = _nt(k_ref[h], q_ref[h]) + bias
        s_ref[h] = s
        tops.append(jnp.max(s, axis=0, keepdims=True))
    m_old = m_ref[...]
    m_new = jnp.maximum(m_old, jnp.concatenate(tops, axis=0))
    alpha = jnp.exp2(m_old - m_new)
    for h in range(N_HEADS):
        p_ref[h] = jnp.exp2(s_ref[h] - m_new[h:h + 1, :]).astype(BF16)
    m_ref[...] = m_new
    for h in range(N_HEADS):
        acc_ref[h] = alpha[h:h + 1, :] * acc_ref[h] + _dot(vt_ref[h], p_ref[h])

    @pl.when(j == ((i + 1) * tq - 1) // tk)
    def _():
        o = jnp.concatenate([acc_ref[h, 0:HEAD_DIM, :] / acc_ref[h, HEAD_DIM:HEAD_DIM + 1, :]
                             for h in range(N_HEADS)], axis=0)
        out_ref[...] = o.T.astype(BF16)


def _attn(q_hm, k_hm, vt_hm, sel, tq, tk):
    s = q_hm.shape[1]
    assert tk % tq == 0
    qb, kb = [], []
    for i in range(s // tq):
        for j in range(((i + 1) * tq - 1) // tk + 1):
            qb.append(i)
            kb.append(j)
    qb = jnp.asarray(qb, I32)
    kb = jnp.asarray(kb, I32)
    grid_spec = pltpu.PrefetchScalarGridSpec(
        num_scalar_prefetch=2,
        grid=(int(qb.shape[0]),),
        in_specs=[pl.BlockSpec((N_HEADS, tq, HEAD_DIM), lambda t, qb, kb: (0, qb[t], 0)),
                  pl.BlockSpec((N_HEADS, tk, HEAD_DIM), lambda t, qb, kb: (0, kb[t], 0)),
                  pl.BlockSpec((N_HEADS, PV_ROWS, tk), lambda t, qb, kb: (0, 0, kb[t])),
                  pl.BlockSpec((1, tk // tq * 8, tq), lambda t, qb, kb: (qb[t], kb[t], 0))],
        out_specs=pl.BlockSpec((tq, ATT_WIDTH), lambda t, qb, kb: (qb[t], 0)),
        scratch_shapes=[pltpu.VMEM((N_HEADS, tq), F32),
                        pltpu.VMEM((N_HEADS, PV_ROWS, tq), F32),
                        pltpu.VMEM((N_HEADS, tk, tq), F32),
                        pltpu.VMEM((N_HEADS, tk, tq), BF16),
                        pltpu.VMEM((tk, tq), F32)],
    )
    return pl.pallas_call(
        functools.partial(_attn_body, tq=tq, tk=tk),
        grid_spec=grid_spec,
        out_shape=jax.ShapeDtypeStruct((s, ATT_WIDTH), BF16),
        compiler_params=_params(("arbitrary",)),
        name="attn",
    )(qb, kb, q_hm, k_hm, vt_hm, sel)


NEW_PAD = 128


def _sample_attn_body(q_ref, qi_ref, w_ref, kic_ref, kin_ref, kc_ref, vc_ref, kn_ref, vn_ref,
                      out_ref, kinp, knp, vnp, *, past, tn):
    kinp[...] = jnp.zeros(kinp.shape, BF16)
    kinp[0:tn, :] = kin_ref[0].astype(BF16)
    knp[...] = jnp.zeros(knp.shape, BF16)
    vnp[...] = jnp.zeros(vnp.shape, BF16)
    for h in range(N_HEADS):
        knp[h, 0:tn, :] = kn_ref[0, h]
        vnp[h, 0:tn, :] = vn_ref[0, h]

    w = w_ref[0]
    kic = kic_ref[0].astype(BF16)
    kin = kinp[...]
    sc = None
    sn = None
    for h in range(N_IDX_HEADS):
        qih = qi_ref[0, h]
        wc = w[:, h:h + 1]
        tc = wc * jnp.maximum(_nt(qih, kic), 0.0)
        tnw = wc * jnp.maximum(_nt(qih, kin), 0.0)
        sc = tc if sc is None else sc + tc
        sn = tnw if sn is None else sn + tnw
    new_ok = lax.broadcasted_iota(I32, (tn, NEW_PAD), 1) < tn
    keyc = _order_key(sc)
    keyn = jnp.where(new_ok, _order_key(sn), INT_MIN)

    def count(pc, pn):
        return (jnp.sum(jnp.where(pc, 1, 0), axis=1, keepdims=True)
                + jnp.sum(jnp.where(pn, 1, 0), axis=1, keepdims=True))

    def bit_step(p, u):
        cand_u = u | (jnp.int32(1) << (31 - p))
        cand = cand_u ^ INT_MIN
        return jnp.where(count(keyc >= cand, keyn >= cand) >= TOPK, cand_u, u)

    kth = lax.fori_loop(0, 32, bit_step, jnp.zeros((tn, 1), I32)) ^ INT_MIN
    need = (TOPK - count(keyc > kth, keyn > kth)).astype(F32)

    cw = 256
    tri = jnp.where(lax.broadcasted_iota(I32, (cw, cw), 0) <= lax.broadcasted_iota(I32, (cw, cw), 1),
                    1.0, 0.0).astype(BF16)
    seen = jnp.zeros((tn, 1), F32)
    bias_c = []
    for c in range(past // cw):
        kc_ = keyc[:, c * cw:(c + 1) * cw]
        eq = kc_ == kth
        pre = _dot(jnp.where(eq, 1.0, 0.0).astype(BF16), tri)
        sel = (kc_ > kth) | (eq & (pre + seen <= need))
        bias_c.append(jnp.where(sel, 0.0, NEG_INF))
        seen = seen + pre[:, cw - 1:cw]
    bias_c = jnp.concatenate(bias_c, axis=1)
    eqn = keyn == kth
    pren = _dot(jnp.where(eqn, 1.0, 0.0).astype(BF16), tri[:NEW_PAD, :NEW_PAD])
    bias_n = jnp.where(new_ok & ((keyn > kth) | (eqn & (pren + seen <= need))), 0.0, NEG_INF)

    kc = kc_ref[0].astype(BF16)
    vc = vc_ref[0].astype(BF16)
    for h in range(N_HEADS):
        qh = q_ref[0, h]
        head = slice(h * HEAD_DIM, (h + 1) * HEAD_DIM)
        s_c = _nt(qh, kc[:, head]) + bias_c
        s_n = _nt(qh, knp[h]) + bias_n
        m = jnp.maximum(jnp.max(s_c, axis=1, keepdims=True), jnp.max(s_n, axis=1, keepdims=True))
        p_c = jnp.exp2(s_c - m)
        p_n = jnp.exp2(s_n - m)
        l = jnp.sum(p_c, axis=1, keepdims=True) + jnp.sum(p_n, axis=1, keepdims=True)
        o = _dot(p_c.astype(BF16), vc[:, head]) + _dot(p_n.astype(BF16), vnp[h])
        out_ref[0, h] = (o / l).astype(BF16)


def _sample_attn(q_hm, qi_hm, w, kic, kin, kc, vc, kn_hm, vn_hm):
    nb, _, tn, _ = q_hm.shape
    past = kic.shape[1]
    b4 = lambda a: pl.BlockSpec((1,) + a.shape[1:], lambda b: (b, 0, 0, 0))
    b3 = lambda a: pl.BlockSpec((1,) + a.shape[1:], lambda b: (b, 0, 0))
    return pl.pallas_call(
        functools.partial(_sample_attn_body, past=past, tn=tn),
        grid=(nb,),
        in_specs=[b4(q_hm), b4(qi_hm), b3(w), b3(kic), b3(kin), b3(kc), b3(vc), b4(kn_hm), b4(vn_hm)],
        out_specs=pl.BlockSpec((1, N_HEADS, tn, HEAD_DIM), lambda b: (b, 0, 0, 0)),
        out_shape=jax.ShapeDtypeStruct((nb, N_HEADS, tn, HEAD_DIM), BF16),
        scratch_shapes=[pltpu.VMEM((NEW_PAD, IDX_DIM), BF16),
                        pltpu.VMEM((N_HEADS, NEW_PAD, HEAD_DIM), BF16),
                        pltpu.VMEM((N_HEADS, NEW_PAD, HEAD_DIM), BF16)],
        compiler_params=_params(("parallel",)),
        name="sample_attn",
    )(q_hm, qi_hm, w, kic, kin, kc, vc, kn_hm, vn_hm)


def _cmpx(a, b):
    return jnp.maximum(a, b), jnp.minimum(a, b)


def _bitonic_merge_desc(v):
    n = len(v)
    v = list(v)
    j = n // 2
    while j >= 1:
        for i in range(n):
            l = i ^ j
            if l > i:
                v[i], v[l] = _cmpx(v[i], v[l])
        j //= 2
    return v


def _bitonic_sort_desc(v):
    n = len(v)
    v = list(v)
    k = 2
    while k <= n:
        j = k // 2
        while j >= 1:
            for i in range(n):
                l = i ^ j
                if l > i:
                    hi, lo = _cmpx(v[i], v[l])
                    v[i], v[l] = (hi, lo) if (i & k) == 0 else (lo, hi)
            j //= 2
        k *= 2
    return v


def _merge_top(a, b):
    n = len(a)
    return _bitonic_merge_desc([jnp.maximum(a[k], b[n - 1 - k]) for k in range(n)])


def _top16_desc(vals):
    groups = [_bitonic_sort_desc(vals[g:g + PEER_TOPK]) for g in range(0, len(vals), PEER_TOPK)]
    while len(groups) > 1:
        groups = [_merge_top(groups[g], groups[g + 1]) for g in range(0, len(groups), 2)]
    return groups[0]


def _mid_body(x_ref, att_ref, conv_ref, woa_ref, woc_ref, g_ref, wqt_ref, a1_ref, a2_ref,
              h_ref, xn_ref, qht_ref, st_ref):
    h = x_ref[...] + _dot(att_ref[...], woa_ref[...]) + _dot(conv_ref[...], woc_ref[...])
    h_ref[...] = h
    xn = _rms(h, g_ref[...]).astype(BF16)
    xn_ref[...] = xn
    qht = _nt(wqt_ref[...], xn).astype(BF16)
    qht_ref[...] = qht
    half = PEER_HEADS * PEER_HALF
    s1 = _dot(a1_ref[...], qht[:half])
    s2 = _dot(a2_ref[...], qht[half:])
    rows = lambda s: [s[r * PEER_HEADS:(r + 1) * PEER_HEADS, :] for r in range(PEER_KEYS)]
    v1 = _top16_desc(rows(s1))
    v2 = _top16_desc(rows(s2))
    ninf = jnp.full(v1[0].shape, NEG_INF, F32)
    top = [v1[0] + v2[b] for b in range(PEER_TOPK)]
    for a in range(1, PEER_TOPK):
        n_a = PEER_TOPK // (a + 1)
        top = _merge_top(top, [v1[a] + v2[b] if b < n_a else ninf for b in range(PEER_TOPK)])
    z = jnp.ones_like(top[0])
    for k in range(1, PEER_TOPK):
        z = z + jnp.exp(top[k] - top[0])
    st_ref[0:8, :] = v1[0]
    st_ref[8:16, :] = v2[0]
    st_ref[16:24, :] = top[PEER_TOPK - 1]
    st_ref[24:32, :] = 1.0 / z


def _mid(x, att, conv, woa, woc, g, wqt, a1, a2, tb):
    t = x.shape[0]
    row = lambda w: pl.BlockSpec((tb, w), lambda i: (i, 0))
    col = lambda r: pl.BlockSpec((r, tb), lambda i: (0, i))
    full = lambda a: pl.BlockSpec(a.shape, lambda i: (0, 0))
    return pl.pallas_call(
        _mid_body,
        grid=(t // tb,),
        in_specs=[row(D_MODEL), row(ATT_WIDTH), row(CONV_CH), full(woa), full(woc), full(g),
                  full(wqt), full(a1), full(a2)],
        out_specs=[row(D_MODEL), row(D_MODEL), col(D_MODEL), col(32)],
        out_shape=[jax.ShapeDtypeStruct((t, D_MODEL), F32),
                   jax.ShapeDtypeStruct((t, D_MODEL), BF16),
                   jax.ShapeDtypeStruct((D_MODEL, t), BF16),
                   jax.ShapeDtypeStruct((32, t), F32)],
        compiler_params=_params(("parallel",)),
        name="mid",
    )(x, att, conv, woa, woc, g, wqt, a1, a2)


def _peer_body(xn_ref, qht_ref, st_ref, a1_ref, a2_ref, u_ref, vt_ref, h_ref, g_ref, y_ref,
               s1_ref, s2_ref, e1_ref, e2_ref, hid_ref, out_ref, *, eb):
    e = pl.program_id(1)
    nsub = eb // PEER_KEYS
    half = PEER_HEADS * PEER_HALF

    @pl.when(e == 0)
    def _():
        qht = qht_ref[...]
        s1 = _dot(a1_ref[...], qht[:half])
        s2 = _dot(a2_ref[...], qht[half:])
        s1_ref[...] = s1
        s2_ref[...] = s2
        for h in range(PEER_HEADS):
            r = slice(h * PEER_KEYS, (h + 1) * PEER_KEYS)
            e1_ref[r, :] = jnp.exp(s1[r] - st_ref[h:h + 1, :])
            e2_ref[r, :] = jnp.exp(s2[r] - st_ref[8 + h:9 + h, :]) * st_ref[24 + h:25 + h, :] * 0.5
        out_ref[...] = jnp.zeros(out_ref.shape, F32)

    xn = xn_ref[...]
    for ii in range(nsub):
        i = e * nsub + ii
        a = _nt(u_ref[ii * PEER_KEYS:(ii + 1) * PEER_KEYS, :], xn)
        gate = None
        for h in range(PEER_HEADS):
            r = slice(h * PEER_KEYS, (h + 1) * PEER_KEYS)
            s1r = s1_ref[pl.ds(h * PEER_KEYS + i, 1), :]
            thr = st_ref[16 + h:17 + h, :]
            bound = (thr - s1r) - (jnp.abs(thr) + jnp.abs(s1r)) * PAIR_SLACK
            g = e1_ref[pl.ds(h * PEER_KEYS + i, 1), :] * e2_ref[r, :]
            g = jnp.where(s2_ref[r, :] >= bound, g, 0.0)
            gate = g if gate is None else gate + g
        hid = a * (1.0 + lax.erf(a * INV_SQRT2)) * gate
        hid_ref[ii * PEER_KEYS:(ii + 1) * PEER_KEYS, :] = hid.astype(BF16)
    out_ref[...] += _dot(vt_ref[...], hid_ref[...])

    @pl.when(e == pl.num_programs(1) - 1)
    def _():
        y_ref[...] = _rms(h_ref[...] + out_ref[...].T, g_ref[...])


def _peer(xn, qht, st, a1, a2, u, vt, h, g, tb, eb):
    t = xn.shape[0]
    ne = u.shape[0] // eb
    full = lambda a: pl.BlockSpec(a.shape, lambda i, e: (0, 0))
    return pl.pallas_call(
        functools.partial(_peer_body, eb=eb),
        grid=(t // tb, ne),
        in_specs=[pl.BlockSpec((tb, D_MODEL), lambda i, e: (i, 0)),
                  pl.BlockSpec((D_MODEL, tb), lambda i, e: (0, i)),
                  pl.BlockSpec((32, tb), lambda i, e: (0, i)),
                  full(a1), full(a2),
                  pl.BlockSpec((eb, D_MODEL), lambda i, e: (e, 0)),
                  pl.BlockSpec((D_MODEL, eb), lambda i, e: (0, e)),
                  pl.BlockSpec((tb, D_MODEL), lambda i, e: (i, 0)),
                  full(g)],
        out_specs=pl.BlockSpec((tb, D_MODEL), lambda i, e: (i, 0)),
        out_shape=jax.ShapeDtypeStruct((t, D_MODEL), F32),
        scratch_shapes=[pltpu.VMEM((PEER_HEADS * PEER_KEYS, tb), F32)] * 4
                       + [pltpu.VMEM((eb, tb), BF16), pltpu.VMEM((D_MODEL, tb), F32)],
        compiler_params=_params(("parallel", "arbitrary")),
        name="peer",
    )(xn, qht, st, a1, a2, u, vt, h, g)


def _pick(n, pref):
    b = min(n, pref)
    while n % b:
        b //= 2
    return b


def _token_stage(x, att, conv, p, tb_mid, tb_peer):
    h, xn, qht, st = _mid(x, att, conv, p["woa"], p["woc"], p["ffn_g"], p["wqt"], p["a1s"], p["a2s"], tb_mid)
    return _peer(xn, qht, st, p["a1d"], p["a2d"], p["u"], p["vt"], h, p["final_g"], tb_peer, 2048)


def kernel(x_prompt, x_sample, cache_k, cache_v, cache_kidx, state_conv, attn_norm_g, w_in, conv_w, conv_b,
           conv_ln_g, conv_ln_b, w_out, ffn_norm_g, peer_wq, peer_subkeys, peer_u, peer_v, final_norm_g):
    assert w_in.shape[0] == 1, "single layer"
    nbp, s, _ = x_prompt.shape
    nb, tn, _ = x_sample.shape
    assert nbp == 1
    past = cache_k.shape[2]

    wi = w_in[0]
    o_qi = 3 * ATT_WIDTH
    o_ki = o_qi + N_IDX_HEADS * IDX_DIM
    o_wi = o_ki + IDX_DIM
    o_glu = o_wi + N_IDX_HEADS
    wqkv = wi[:, :o_qi].astype(BF16)
    widx = jnp.concatenate([wi[:, o_qi:o_glu], jnp.zeros((D_MODEL, 128 - IDX_DIM - N_IDX_HEADS), F32)],
                           axis=1).astype(BF16)
    wglu = wi[:, o_glu:].astype(BF16)
    g_attn = attn_norm_g[0][None, :]
    cw = jnp.concatenate([conv_w[0], jnp.zeros((CONV_TAIL - CONV_WIDTH, CONV_CH), F32)], axis=0)
    cb, cg, cbeta = conv_b[0][None, :], conv_ln_g[0][None, :], conv_ln_b[0][None, :]
    eye = jnp.eye(PEER_HEADS, dtype=F32)
    sk = peer_subkeys[0]
    a_s = [jnp.einsum("id,hg->ihgd", sk[c], eye).reshape(PEER_KEYS * PEER_HEADS, PEER_HEADS * PEER_HALF)
           .astype(BF16) for c in range(2)]
    a_d = [jnp.einsum("id,hg->higd", sk[c], eye).reshape(PEER_KEYS * PEER_HEADS, PEER_HEADS * PEER_HALF)
           .astype(BF16) for c in range(2)]
    wq = peer_wq[0].reshape(D_MODEL, PEER_HEADS, 2, PEER_HALF).transpose(2, 1, 3, 0)
    p = {
        "woa": w_out[0][:ATT_WIDTH].astype(BF16), "woc": w_out[0][ATT_WIDTH:].astype(BF16),
        "ffn_g": ffn_norm_g[0][None, :], "final_g": final_norm_g[None, :],
        "wqt": wq.reshape(D_MODEL, D_MODEL).astype(BF16),
        "a1s": a_s[0], "a2s": a_s[1], "a1d": a_d[0], "a2d": a_d[1],
        "u": peer_u[0].astype(BF16), "vt": peer_v[0].T.astype(BF16),
    }
    w_scale = (N_IDX_HEADS ** -0.5) * (IDX_DIM ** -0.5)

    xp = x_prompt[0]
    tb = _pick(s, 512)
    k, v, kw, q_hm, k_hm, vt_hm, qi_hm, conv_p, tail_p = _in_proj_conv(
        xp, g_attn, wqkv, widx, wglu, jnp.zeros((1, CONV_TAIL, CONV_CH), F32), cw, cb, cg, cbeta, tb)
    ki = kw[:, :IDX_DIM]
    ki_b = ki.astype(BF16)
    w_t = (kw[:, IDX_DIM:IDX_DIM + N_IDX_HEADS] * w_scale).T
    tq = _pick(s, 256)
    sel = _idx_sel(qi_hm, w_t, ki_b, tq)
    att_p = _attn(q_hm, k_hm, vt_hm, sel, tq, _pick(s, 1024))
    y_p = _token_stage(xp, att_p, conv_p, p, tb, _pick(s, 512))

    xs = x_sample.reshape(nb * tn, D_MODEL)
    ts = nb * tn
    tbs = _pick(ts, 256)
    ks, vs, kws, glus, qs_hm, _, _, qis_hm = _in_proj(xs, g_attn, wqkv, widx, wglu, tbs)
    pad = jnp.zeros((nb, CONV_TAIL - (CONV_WIDTH - 1), CONV_CH), F32)
    conv_s, tail_s = _conv(glus, jnp.concatenate([pad, state_conv[0]], axis=1), cw, cb, cg, cbeta, nb, tn)
    hm4 = lambda a, nh, dt=BF16: a.reshape(nb, tn, nh, a.shape[1] // nh).transpose(0, 2, 1, 3).astype(dt)
    per_stream = lambda a: a.reshape(a.shape[0], nb, tn, a.shape[2]).transpose(1, 0, 2, 3)
    kis = kws[:, :IDX_DIM]
    att_s = _sample_attn(
        per_stream(qs_hm), per_stream(qis_hm),
        (kws[:, IDX_DIM:IDX_DIM + N_IDX_HEADS] * w_scale).reshape(nb, tn, N_IDX_HEADS),
        cache_kidx[0], kis.reshape(nb, tn, IDX_DIM),
        cache_k[0].reshape(nb, past, ATT_WIDTH), cache_v[0].reshape(nb, past, ATT_WIDTH),
        hm4(ks, N_HEADS), hm4(vs, N_HEADS))
    att_s = att_s.transpose(0, 2, 1, 3).reshape(ts, ATT_WIDTH)
    y_s = _token_stage(xs, att_s, conv_s, p, tbs, tbs)

    hd = (N_HEADS, HEAD_DIM)
    keep = CONV_TAIL - (CONV_WIDTH - 1)
    return (y_p[None], y_s.reshape(nb, tn, D_MODEL),
            k.reshape(1, 1, s, *hd), v.reshape(1, 1, s, *hd), ki[None, None], tail_p[None, :, keep:],
            ks.reshape(1, nb, tn, *hd), vs.reshape(1, nb, tn, *hd), kis.reshape(1, nb, tn, IDX_DIM),
            tail_s[None, :, keep:])
```

```python
import functools

import jax
import jax.numpy as jnp
from jax import lax
from jax.experimental import pallas as pl
from jax.experimental.pallas import tpu as pltpu

F32 = jnp.float32
BF16 = jnp.bfloat16
I32 = jnp.int32

D_MODEL = 1024
N_HEADS = 8
HEAD_DIM = 64
ATT_WIDTH = N_HEADS * HEAD_DIM
PV_ROWS = HEAD_DIM + 16
N_IDX_HEADS = 4
IDX_DIM = 64
TOPK = 256
CHUNK_SHIFT = 6
CONV_CH = 512
CONV_WIDTH = 31
CONV_TAIL = 32
PEER_HEADS = 8
PEER_KEYS = 128
PEER_HALF = 64
PEER_TOPK = 16
PAIR_SLACK = 2.0 ** -22
PAIR_SLACK_ABS = 2.0 ** -20
EPS = 1e-6
INT_MIN = -2147483648
ORDER_MASK = 0x7FFFFFFF
INV_SQRT2 = 0.7071067811865476
LOG2E = 1.4426950408889634
NEG_INF = float("-inf")
M_INIT = -1e30

VMEM_LIMIT = 56 * 1024 * 1024


def _params(sem, vmem=VMEM_LIMIT):
    return pltpu.CompilerParams(dimension_semantics=sem, vmem_limit_bytes=vmem)


def _nt(a, b):
    return lax.dot_general(a, b, (((1,), (1,)), ((), ())), preferred_element_type=F32)


def _dot(a, b):
    return jnp.dot(a, b, preferred_element_type=F32)


def _rms(x, g):
    return x * lax.rsqrt(jnp.mean(x * x, axis=-1, keepdims=True) + EPS) * g


def _order_key(x):
    b = pltpu.bitcast(x, I32)
    return b ^ ((b >> 31) & ORDER_MASK)


def _in_proj_body(x_ref, g_ref, wqkv_ref, widx_ref, wglu_ref,
                  k_ref, v_ref, kw_ref, glu_ref, qh_ref, kh_ref, vth_ref, qih_ref):
    glu_ref[...] = _project(x_ref, g_ref, wqkv_ref, widx_ref, wglu_ref,
                            k_ref, v_ref, kw_ref, qh_ref, kh_ref, vth_ref, qih_ref)


def _project(x_ref, g_ref, wqkv_ref, widx_ref, wglu_ref, k_ref, v_ref, kw_ref, qh_ref, kh_ref, vth_ref, qih_ref):
    xn = _rms(x_ref[...], g_ref[...]).astype(BF16)
    qkv = _dot(xn, wqkv_ref[...])
    q = qkv[:, :ATT_WIDTH] * (HEAD_DIM ** -0.5 * LOG2E)
    k = qkv[:, ATT_WIDTH:2 * ATT_WIDTH]
    v = qkv[:, 2 * ATT_WIDTH:]
    k_ref[...] = k
    v_ref[...] = v
    for h in range(N_HEADS):
        head = slice(h * HEAD_DIM, (h + 1) * HEAD_DIM)
        qh_ref[h] = q[:, head].astype(BF16)
        kh_ref[h] = k[:, head].astype(BF16)
    vth_ref[:, 0:HEAD_DIM, :] = v.T.reshape(N_HEADS, HEAD_DIM, v.shape[0]).astype(BF16)
    vth_ref[:, HEAD_DIM:, :] = jnp.ones((N_HEADS, PV_ROWS - HEAD_DIM, v.shape[0]), BF16)
    ix = _dot(xn, widx_ref[...])
    for h in range(N_IDX_HEADS):
        qih_ref[h] = ix[:, h * IDX_DIM:(h + 1) * IDX_DIM].astype(BF16)
    kw_ref[...] = ix[:, N_IDX_HEADS * IDX_DIM:]
    return _dot(xn, wglu_ref[...])


def _in_proj(x, g, wqkv, widx, wglu, tb):
    t = x.shape[0]
    row = lambda w: pl.BlockSpec((tb, w), lambda i: (i, 0))
    full = lambda a: pl.BlockSpec(a.shape, lambda i: (0, 0))
    heads = lambda n: pl.BlockSpec((n, tb, HEAD_DIM), lambda i: (0, i, 0))
    return pl.pallas_call(
        _in_proj_body,
        grid=(t // tb,),
        in_specs=[row(D_MODEL), full(g), full(wqkv), full(widx), full(wglu)],
        out_specs=[row(ATT_WIDTH), row(ATT_WIDTH), row(128), row(2 * CONV_CH),
                   heads(N_HEADS), heads(N_HEADS),
                   pl.BlockSpec((N_HEADS, PV_ROWS, tb), lambda i: (0, 0, i)), heads(N_IDX_HEADS)],
        out_shape=[jax.ShapeDtypeStruct((t, ATT_WIDTH), F32),
                   jax.ShapeDtypeStruct((t, ATT_WIDTH), F32),
                   jax.ShapeDtypeStruct((t, 128), F32),
                   jax.ShapeDtypeStruct((t, 2 * CONV_CH), F32),
                   jax.ShapeDtypeStruct((N_HEADS, t, HEAD_DIM), BF16),
                   jax.ShapeDtypeStruct((N_HEADS, t, HEAD_DIM), BF16),
                   jax.ShapeDtypeStruct((N_HEADS, PV_ROWS, t), BF16),
                   jax.ShapeDtypeStruct((N_IDX_HEADS, t, IDX_DIM), BF16)],
        compiler_params=_params(("parallel",)),
        name="in_proj",
    )(x, g, wqkv, widx, wglu)


def _conv_body(glu_ref, tail0_ref, w_ref, b_ref, g_ref, beta_ref, out_ref, tail_ref, ubuf, shifted, *, tb):
    _conv_core(glu_ref[...], pl.program_id(1), tail0_ref, w_ref, b_ref, g_ref, beta_ref,
               out_ref, tail_ref, ubuf, shifted, tb)


def _conv_core(glu, j, tail0_ref, w_ref, b_ref, g_ref, beta_ref, out_ref, tail_ref, ubuf, shifted, tb):
    @pl.when(j == 0)
    def _():
        ubuf[0:CONV_TAIL, :] = tail0_ref[0]

    @pl.when(j > 0)
    def _():
        ubuf[0:CONV_TAIL, :] = ubuf[tb:tb + CONV_TAIL, :]

    ubuf[CONV_TAIL:CONV_TAIL + tb, :] = glu[:, :CONV_CH] * jax.nn.sigmoid(glu[:, CONV_CH:])
    off = CONV_TAIL - (CONV_WIDTH - 1)
    acc = None
    for res in range(8):
        taps = [t for t in range(CONV_WIDTH) if (off + t) % 8 == res]
        lo, hi = (off + taps[0]) // 8, (off + taps[-1]) // 8
        n = 8 * (hi - lo) + tb
        shifted[0:n, :] = ubuf[res + 8 * lo:res + 8 * lo + n, :]
        for t in taps:
            a = 8 * ((off + t) // 8 - lo)
            term = shifted[a:a + tb, :] * w_ref[t:t + 1, :]
            acc = term if acc is None else acc + term
    y = acc + b_ref[...]
    mu = jnp.mean(y, axis=-1, keepdims=True)
    d = y - mu
    var = jnp.mean(d * d, axis=-1, keepdims=True)
    z = d * lax.rsqrt(var + EPS) * g_ref[...] + beta_ref[...]
    out_ref[...] = (z * jax.nn.sigmoid(z)).astype(BF16)
    tail_ref[0] = ubuf[tb:tb + CONV_TAIL, :]


def _conv(glu, tail0, w, b, g, beta, nseq, tb):
    t = glu.shape[0]
    nblk = t // (nseq * tb)
    full = lambda a: pl.BlockSpec(a.shape, lambda s, j: (0, 0))
    return pl.pallas_call(
        functools.partial(_conv_body, tb=tb),
        grid=(nseq, nblk),
        in_specs=[pl.BlockSpec((tb, 2 * CONV_CH), lambda s, j: (s * nblk + j, 0)),
                  pl.BlockSpec((1, CONV_TAIL, CONV_CH), lambda s, j: (s, 0, 0)),
                  full(w), full(b), full(g), full(beta)],
        out_specs=[pl.BlockSpec((tb, CONV_CH), lambda s, j: (s * nblk + j, 0)),
                   pl.BlockSpec((1, CONV_TAIL, CONV_CH), lambda s, j: (s, 0, 0))],
        out_shape=[jax.ShapeDtypeStruct((t, CONV_CH), BF16),
                   jax.ShapeDtypeStruct((nseq, CONV_TAIL, CONV_CH), F32)],
        scratch_shapes=[pltpu.VMEM((tb + CONV_TAIL, CONV_CH), F32)] * 2,
        compiler_params=_params(("parallel", "arbitrary")),
        name="conv",
    )(glu, tail0, w, b, g, beta)


def _in_proj_conv_body(x_ref, g_ref, wqkv_ref, widx_ref, wglu_ref, tail0_ref, cw_ref, cb_ref, cg_ref, cbeta_ref,
                       k_ref, v_ref, kw_ref, qh_ref, kh_ref, vth_ref, qih_ref, conv_ref, tail_ref,
                       ubuf, shifted, *, tb):
    glu = _project(x_ref, g_ref, wqkv_ref, widx_ref, wglu_ref, k_ref, v_ref, kw_ref, qh_ref, kh_ref, vth_ref, qih_ref)
    _conv_core(glu, pl.program_id(0), tail0_ref, cw_ref, cb_ref, cg_ref, cbeta_ref,
               conv_ref, tail_ref, ubuf, shifted, tb)


def _in_proj_conv(x, g, wqkv, widx, wglu, tail0, cw, cb, cg, cbeta, tb):
    t = x.shape[0]
    row = lambda w: pl.BlockSpec((tb, w), lambda i: (i, 0))
    full = lambda a: pl.BlockSpec(a.shape, lambda i: (0,) * a.ndim)
    heads = lambda n: pl.BlockSpec((n, tb, HEAD_DIM), lambda i: (0, i, 0))
    return pl.pallas_call(
        functools.partial(_in_proj_conv_body, tb=tb),
        grid=(t // tb,),
        in_specs=[row(D_MODEL), full(g), full(wqkv), full(widx), full(wglu),
                  full(tail0), full(cw), full(cb), full(cg), full(cbeta)],
        out_specs=[row(ATT_WIDTH), row(ATT_WIDTH), row(128),
                   heads(N_HEADS), heads(N_HEADS),
                   pl.BlockSpec((N_HEADS, PV_ROWS, tb), lambda i: (0, 0, i)), heads(N_IDX_HEADS),
                   row(CONV_CH), pl.BlockSpec((1, CONV_TAIL, CONV_CH), lambda i: (0, 0, 0))],
        out_shape=[jax.ShapeDtypeStruct((t, ATT_WIDTH), F32),
                   jax.ShapeDtypeStruct((t, ATT_WIDTH), F32),
                   jax.ShapeDtypeStruct((t, 128), F32),
                   jax.ShapeDtypeStruct((N_HEADS, t, HEAD_DIM), BF16),
                   jax.ShapeDtypeStruct((N_HEADS, t, HEAD_DIM), BF16),
                   jax.ShapeDtypeStruct((N_HEADS, PV_ROWS, t), BF16),
                   jax.ShapeDtypeStruct((N_IDX_HEADS, t, IDX_DIM), BF16),
                   jax.ShapeDtypeStruct((t, CONV_CH), BF16),
                   jax.ShapeDtypeStruct((1, CONV_TAIL, CONV_CH), F32)],
        scratch_shapes=[pltpu.VMEM((tb + CONV_TAIL, CONV_CH), F32)] * 2,
        compiler_params=_params(("arbitrary",)),
        name="in_proj_conv",
    )(x, g, wqkv, widx, wglu, tail0, cw, cb, cg, cbeta)


def _idx_scores_t(ki_tile, qi_ref, w):
    acc = None
    for h in range(N_IDX_HEADS):
        term = w[h:h + 1, :] * jnp.maximum(_nt(ki_tile, qi_ref[h]), 0.0)
        acc = term if acc is None else acc + term
    return acc


def _chunk_limit(q0, tq):
    qpos = q0 + lax.broadcasted_iota(I32, (1, tq), 1)
    return ((qpos >> CHUNK_SHIFT) + 1) << CHUNK_SHIFT


def _bit_planes(words):
    x = list(words)
    mask, j = 0x0000FFFF, 16
    while j:
        k = 0
        while k < 32:
            t = (x[k] ^ lax.shift_right_logical(x[k + j], jnp.int32(j))) & mask
            x[k] = x[k] ^ t
            x[k + j] = x[k + j] ^ (t << j)
            k = (k + j + 1) & ~j
        j >>= 1
        mask ^= (mask << j) & 0xFFFFFFFF
    return x


def _sublane_prefix(x):
    sub = lax.broadcasted_iota(I32, x.shape, 0)
    for sh in (1, 2, 4):
        x = x + jnp.where(sub >= sh, pltpu.roll(x, sh, axis=0), 0)
    return x


def _sel_body(qi_ref, w_ref, ki_ref, sel_ref, planes_ref, cand_ref, gt_ref, *, tq, grp):
    i = pl.program_id(0)
    nt = i + 1
    ng = (nt + grp - 1) // grp
    w = w_ref[...]
    limit = _chunk_limit(i * tq, tq)
    rows = lax.broadcasted_iota(I32, (tq, tq), 0)
    assert tq == 8 * 32

    def fill(t, diagonal):
        r0 = pl.multiple_of(t * tq, tq)
        b = pltpu.bitcast(_idx_scores_t(ki_ref[pl.ds(r0, tq), :], qi_ref, w), I32)
        u = b ^ ((b >> 31) | INT_MIN)
        if diagonal:
            u = jnp.where(rows + r0 < limit, u, 0)
        planes = _bit_planes([u[8 * k:8 * k + 8, :] for k in range(32)])
        at = pl.ds(pl.multiple_of(t * 8, 8), 8)
        for p in range(32):
            planes_ref[p, at, :] = planes[p]
        cand_ref[at, :] = jnp.full((8, tq), -1, I32)
        gt_ref[at, :] = jnp.zeros((8, tq), I32)

    def fill_full(t, c):
        fill(t, False)
        return c

    lax.fori_loop(0, nt - 1, fill_full, 0)
    fill(nt - 1, True)

    def pad(t, c):
        at = pl.ds(pl.multiple_of(t * 8, 8), 8)
        planes_ref[:, at, :] = jnp.zeros((32, 8, tq), I32)
        cand_ref[at, :] = jnp.zeros((8, tq), I32)
        gt_ref[at, :] = jnp.zeros((8, tq), I32)
        return c

    lax.fori_loop(nt, ng * grp, pad, 0)

    def sweep(p_prev, flip, p_next):
        def group(g, acc):
            at = pl.ds(pl.multiple_of(g * (grp * 8), grp * 8), grp * 8)
            c = cand_ref[at, :]
            if p_prev is not None:
                prev = planes_ref[p_prev, at, :]
                gt_ref[at, :] = gt_ref[at, :] | (c & prev & flip)
                c = c & (prev ^ flip)
                cand_ref[at, :] = c
            if p_next is not None:
                hit = lax.population_count(c & planes_ref[p_next, at, :])
                acc = acc + jnp.sum(hit.reshape(grp, 8, tq), axis=0)
            return acc
        acc = lax.fori_loop(0, ng, group, jnp.zeros((8, tq), I32))
        return jnp.sum(acc, axis=0, keepdims=True)

    def decide(p, cnt, kth_u, need):
        one = cnt >= need
        kth_u = jnp.where(one, kth_u | lax.shift_right_logical(jnp.int32(INT_MIN), p), kth_u)
        return kth_u, jnp.where(one, need, need - cnt), jnp.where(one, 0, -1)

    zero = jnp.zeros((1, tq), I32)
    kth_u, need, flip = decide(jnp.int32(0), sweep(None, None, 0), zero, jnp.full((1, tq), TOPK, I32))

    def bit_step(p, carry):
        kth_u, need, flip = carry
        return decide(p, sweep(p - 1, flip, p), kth_u, need)

    kth_u, need, flip = lax.fori_loop(1, 32, bit_step, (kth_u, need, flip))
    sweep(31, flip, None)
    need = jnp.where(kth_u == 0, 0, need)

    def tie_count(t):
        c = cand_ref[pl.ds(pl.multiple_of(t * 8, 8), 8), :]
        return c, jnp.sum(lax.population_count(c), axis=0, keepdims=True)

    def find(t, carry):
        seen, part, words = carry
        c, n = tie_count(t)
        hit = (seen < need) & (seen + n > need)
        return seen + n, jnp.where(hit, need - seen, part), jnp.where(hit, c, words)

    _, part, words = lax.fori_loop(0, nt, find, (zero, zero, jnp.zeros((8, tq), I32)))
    keep = jnp.zeros((8, tq), I32)
    before = zero
    for k in range(32):
        bit = lax.shift_right_logical(words, jnp.int32(31 - k)) & 1
        rank = before + _sublane_prefix(bit)
        keep = keep | jnp.where((bit == 1) & (rank <= part), INT_MIN if k == 0 else 1 << (31 - k), 0)
        before = rank[7:8, :]

    def emit(t, seen):
        c, n = tie_count(t)
        at = pl.ds(pl.multiple_of(t * 8, 8), 8)
        whole = seen + n <= need
        partial = (seen < need) & jnp.logical_not(whole)
        sel_ref[0, at, :] = gt_ref[at, :] | jnp.where(whole, c, jnp.where(partial, keep, 0))
        return seen + n

    lax.fori_loop(0, nt, emit, zero)

    def clear(t, c):
        sel_ref[0, pl.ds(pl.multiple_of(t * 8, 8), 8), :] = jnp.zeros((8, tq), I32)
        return c

    lax.fori_loop(nt, sel_ref.shape[1] // 8, clear, 0)


def _idx_sel(qi_hm, w_t, ki, tq):
    s = ki.shape[0]
    nrow = s // tq * 8
    return pl.pallas_call(
        functools.partial(_sel_body, tq=tq, grp=_pick(s // tq, 16)),
        grid=(s // tq,),
        in_specs=[pl.BlockSpec((N_IDX_HEADS, tq, IDX_DIM), lambda i: (0, i, 0)),
                  pl.BlockSpec((N_IDX_HEADS, tq), lambda i: (0, i)),
                  pl.BlockSpec((s, IDX_DIM), lambda i: (0, 0))],
        out_specs=pl.BlockSpec((1, nrow, tq), lambda i: (i, 0, 0)),
        out_shape=jax.ShapeDtypeStruct((s // tq, nrow, tq), I32),
        scratch_shapes=[pltpu.VMEM((32, nrow, tq), I32), pltpu.VMEM((nrow, tq), I32),
                        pltpu.VMEM((nrow, tq), I32)],
        compiler_params=_params(("arbitrary",)),
        name="idx_sel",
    )(qi_hm, w_t, ki)


def _attn_body(qb_ref, kb_ref, q_ref, k_ref, vt_ref, sel_ref,
               out_ref, m_ref, acc_ref, s_ref, p_ref, bias_ref, *, tq, tk):
    step = pl.program_id(0)
    i = qb_ref[step]
    j = kb_ref[step]

    @pl.when(j == 0)
    def _():
        m_ref[...] = jnp.full(m_ref.shape, M_INIT, F32)
        acc_ref[...] = jnp.zeros(acc_ref.shape, F32)

    for tile in range(tk // tq):
        words = sel_ref[0, tile * 8:(tile + 1) * 8, :]
        for k in range(32):
            bias_ref[pl.ds(tile * tq + 8 * k, 8), :] = jnp.where((words << k) < 0, 0.0, NEG_INF)
    bias = bias_ref[...]
    tops = []
    for h in range(N_HEADS):
        s = _nt(k_ref[h], q_ref[h]) + bias
        s_ref[h] = s
        tops.append(jnp.max(s, axis=0, keepdims=True))
    m_old = m_ref[...]
    m_new = jnp.maximum(m_old, jnp.concatenate(tops, axis=0))
    alpha = jnp.exp2(m_old - m_new)
    for h in range(N_HEADS):
        p_ref[h] = jnp.exp2(s_ref[h] - m_new[h:h + 1, :]).astype(BF16)
    m_ref[...] = m_new
    for h in range(N_HEADS):
        acc_ref[h] = alpha[h:h + 1, :] * acc_ref[h] + _dot(vt_ref[h], p_ref[h])

    @pl.when(j == ((i + 1) * tq - 1) // tk)
    def _():
        o = jnp.concatenate([acc_ref[h, 0:HEAD_DIM, :] / acc_ref[h, HEAD_DIM:HEAD_DIM + 1, :]
                             for h in range(N_HEADS)], axis=0)
        out_ref[...] = o.T.astype(BF16)


def _attn(q_hm, k_hm, vt_hm, sel, tq, tk):
    s = q_hm.shape[1]
    assert tk % tq == 0
    qb, kb = [], []
    for i in range(s // tq):
        for j in range(((i + 1) * tq - 1) // tk + 1):
            qb.append(i)
            kb.append(j)
    qb = jnp.asarray(qb, I32)
    kb = jnp.asarray(kb, I32)
    grid_spec = pltpu.PrefetchScalarGridSpec(
        num_scalar_prefetch=2,
        grid=(int(qb.shape[0]),),
        in_specs=[pl.BlockSpec((N_HEADS, tq, HEAD_DIM), lambda t, qb, kb: (0, qb[t], 0)),
                  pl.BlockSpec((N_HEADS, tk, HEAD_DIM), lambda t, qb, kb: (0, kb[t], 0)),
                  pl.BlockSpec((N_HEADS, PV_ROWS, tk), lambda t, qb, kb: (0, 0, kb[t])),
                  pl.BlockSpec((1, tk // tq * 8, tq), lambda t, qb, kb: (qb[t], kb[t], 0))],
        out_specs=pl.BlockSpec((tq, ATT_WIDTH), lambda t, qb, kb: (qb[t], 0)),
        scratch_shapes=[pltpu.VMEM((N_HEADS, tq), F32),
                        pltpu.VMEM((N_HEADS, PV_ROWS, tq), F32),
                        pltpu.VMEM((N_HEADS, tk, tq), F32),
                        pltpu.VMEM((N_HEADS, tk, tq), BF16),
                        pltpu.VMEM((tk, tq), F32)],
    )
    return pl.pallas_call(
        functools.partial(_attn_body, tq=tq, tk=tk),
        grid_spec=grid_spec,
        out_shape=jax.ShapeDtypeStruct((s, ATT_WIDTH), BF16),
        compiler_params=_params(("arbitrary",)),
        name="attn",
    )(qb, kb, q_hm, k_hm, vt_hm, sel)


NEW_PAD = 128


def _sample_attn_body(q_ref, qi_ref, w_ref, kic_ref, kin_ref, kc_ref, vc_ref, kn_ref, vn_ref,
                      out_ref, kinp, knp, vnp, *, past, tn):
    kinp[...] = jnp.zeros(kinp.shape, BF16)
    kinp[0:tn, :] = kin_ref[0].astype(BF16)
    knp[...] = jnp.zeros(knp.shape, BF16)
    vnp[...] = jnp.zeros(vnp.shape, BF16)
    for h in range(N_HEADS):
        knp[h, 0:tn, :] = kn_ref[0, h]
        vnp[h, 0:tn, :] = vn_ref[0, h]

    w = w_ref[0]
    kic = kic_ref[0].astype(BF16)
    kin = kinp[...]
    sc = None
    sn = None
    for h in range(N_IDX_HEADS):
        qih = qi_ref[0, h]
        wc = w[:, h:h + 1]
        tc = wc * jnp.maximum(_nt(qih, kic), 0.0)
        tnw = wc * jnp.maximum(_nt(qih, kin), 0.0)
        sc = tc if sc is None else sc + tc
        sn = tnw if sn is None else sn + tnw
    new_ok = lax.broadcasted_iota(I32, (tn, NEW_PAD), 1) < tn
    keyc = _order_key(sc)
    keyn = jnp.where(new_ok, _order_key(sn), INT_MIN)

    def count(pc, pn):
        return (jnp.sum(jnp.where(pc, 1, 0), axis=1, keepdims=True)
                + jnp.sum(jnp.where(pn, 1, 0), axis=1, keepdims=True))

    def bit_step(p, u):
        cand_u = u | (jnp.int32(1) << (31 - p))
        cand = cand_u ^ INT_MIN
        return jnp.where(count(keyc >= cand, keyn >= cand) >= TOPK, cand_u, u)

    kth = lax.fori_loop(0, 32, bit_step, jnp.zeros((tn, 1), I32)) ^ INT_MIN
    need = (TOPK - count(keyc > kth, keyn > kth)).astype(F32)

    cw = 256
    tri = jnp.where(lax.broadcasted_iota(I32, (cw, cw), 0) <= lax.broadcasted_iota(I32, (cw, cw), 1),
                    1.0, 0.0).astype(BF16)
    seen = jnp.zeros((tn, 1), F32)
    bias_c = []
    for c in range(past // cw):
        kc_ = keyc[:, c * cw:(c + 1) * cw]
        eq = kc_ == kth
        pre = _dot(jnp.where(eq, 1.0, 0.0).astype(BF16), tri)
        sel = (kc_ > kth) | (eq & (pre + seen <= need))
        bias_c.append(jnp.where(sel, 0.0, NEG_INF))
        seen = seen + pre[:, cw - 1:cw]
    bias_c = jnp.concatenate(bias_c, axis=1)
    eqn = keyn == kth
    pren = _dot(jnp.where(eqn, 1.0, 0.0).astype(BF16), tri[:NEW_PAD, :NEW_PAD])
    bias_n = jnp.where(new_ok & ((keyn > kth) | (eqn & (pren + seen <= need))), 0.0, NEG_INF)

    kc = kc_ref[0].astype(BF16)
    vc = vc_ref[0].astype(BF16)
    for h in range(N_HEADS):
        qh = q_ref[0, h]
        head = slice(h * HEAD_DIM, (h + 1) * HEAD_DIM)
        s_c = _nt(qh, kc[:, head]) + bias_c
        s_n = _nt(qh, knp[h]) + bias_n
        m = jnp.maximum(jnp.max(s_c, axis=1, keepdims=True), jnp.max(s_n, axis=1, keepdims=True))
        p_c = jnp.exp2(s_c - m)
        p_n = jnp.exp2(s_n - m)
        l = jnp.sum(p_c, axis=1, keepdims=True) + jnp.sum(p_n, axis=1, keepdims=True)
        o = _dot(p_c.astype(BF16), vc[:, head]) + _dot(p_n.astype(BF16), vnp[h])
        out_ref[0, h] = (o / l).astype(BF16)


def _sample_attn(q_hm, qi_hm, w, kic, kin, kc, vc, kn_hm, vn_hm):
    nb, _, tn, _ = q_hm.shape
    past = kic.shape[1]
    b4 = lambda a: pl.BlockSpec((1,) + a.shape[1:], lambda b: (b, 0, 0, 0))
    b3 = lambda a: pl.BlockSpec((1,) + a.shape[1:], lambda b: (b, 0, 0))
    return pl.pallas_call(
        functools.partial(_sample_attn_body, past=past, tn=tn),
        grid=(nb,),
        in_specs=[b4(q_hm), b4(qi_hm), b3(w), b3(kic), b3(kin), b3(kc), b3(vc), b4(kn_hm), b4(vn_hm)],
        out_specs=pl.BlockSpec((1, N_HEADS, tn, HEAD_DIM), lambda b: (b, 0, 0, 0)),
        out_shape=jax.ShapeDtypeStruct((nb, N_HEADS, tn, HEAD_DIM), BF16),
        scratch_shapes=[pltpu.VMEM((NEW_PAD, IDX_DIM), BF16),
                        pltpu.VMEM((N_HEADS, NEW_PAD, HEAD_DIM), BF16),
                        pltpu.VMEM((N_HEADS, NEW_PAD, HEAD_DIM), BF16)],
        compiler_params=_params(("parallel",)),
        name="sample_attn",
    )(q_hm, qi_hm, w, kic, kin, kc, vc, kn_hm, vn_hm)


def _cmpx(a, b):
    return jnp.maximum(a, b), jnp.minimum(a, b)


def _bitonic_merge_desc(v):
    n = len(v)
    v = list(v)
    j = n // 2
    while j >= 1:
        for i in range(n):
            l = i ^ j
            if l > i:
                v[i], v[l] = _cmpx(v[i], v[l])
        j //= 2
    return v


def _bitonic_sort_desc(v):
    n = len(v)
    v = list(v)
    k = 2
    while k <= n:
        j = k // 2
        while j >= 1:
            for i in range(n):
                l = i ^ j
                if l > i:
                    hi, lo = _cmpx(v[i], v[l])
                    v[i], v[l] = (hi, lo) if (i & k) == 0 else (lo, hi)
            j //= 2
        k *= 2
    return v


def _merge_top(a, b):
    n = len(a)
    return _bitonic_merge_desc([jnp.maximum(a[k], b[n - 1 - k]) for k in range(n)])


def _top16_desc(vals):
    groups = [_bitonic_sort_desc(vals[g:g + PEER_TOPK]) for g in range(0, len(vals), PEER_TOPK)]
    while len(groups) > 1:
        groups = [_merge_top(groups[g], groups[g + 1]) for g in range(0, len(groups), 2)]
    return groups[0]


def _mid_body(x_ref, att_ref, conv_ref, woa_ref, woc_ref, g_ref, wqt_ref, a1_ref, a2_ref,
              h_ref, xn_ref, qht_ref, st_ref):
    h = x_ref[...] + _dot(att_ref[...], woa_ref[...]) + _dot(conv_ref[...], woc_ref[...])
    h_ref[...] = h
    xn = _rms(h, g_ref[...]).astype(BF16)
    xn_ref[...] = xn
    qht = _nt(wqt_ref[...], xn).astype(BF16)
    qht_ref[...] = qht
    half = PEER_HEADS * PEER_HALF
    s1 = _dot(a1_ref[...], qht[:half])
    s2 = _dot(a2_ref[...], qht[half:])
    rows = lambda s: [s[r * PEER_HEADS:(r + 1) * PEER_HEADS, :] for r in range(PEER_KEYS)]
    v1 = _top16_desc(rows(s1))
    v2 = _top16_desc(rows(s2))
    ninf = jnp.full(v1[0].shape, NEG_INF, F32)
    top = [v1[0] + v2[b] for b in range(PEER_TOPK)]
    for a in range(1, PEER_TOPK):
        n_a = PEER_TOPK // (a + 1)
        top = _merge_top(top, [v1[a] + v2[b] if b < n_a else ninf for b in range(PEER_TOPK)])
    z = jnp.ones_like(top[0])
    for k in range(1, PEER_TOPK):
        z = z + jnp.exp(top[k] - top[0])
    st_ref[0:8, :] = v1[0]
    st_ref[8:16, :] = v2[0]
    st_ref[16:24, :] = top[PEER_TOPK - 1]
    st_ref[24:32, :] = 1.0 / z


def _mid(x, att, conv, woa, woc, g, wqt, a1, a2, tb):
    t = x.shape[0]
    row = lambda w: pl.BlockSpec((tb, w), lambda i: (i, 0))
    col = lambda r: pl.BlockSpec((r, tb), lambda i: (0, i))
    full = lambda a: pl.BlockSpec(a.shape, lambda i: (0, 0))
    return pl.pallas_call(
        _mid_body,
        grid=(t // tb,),
        in_specs=[row(D_MODEL), row(ATT_WIDTH), row(CONV_CH), full(woa), full(woc), full(g),
                  full(wqt), full(a1), full(a2)],
        out_specs=[row(D_MODEL), row(D_MODEL), col(D_MODEL), col(32)],
        out_shape=[jax.ShapeDtypeStruct((t, D_MODEL), F32),
                   jax.ShapeDtypeStruct((t, D_MODEL), BF16),
                   jax.ShapeDtypeStruct((D_MODEL, t), BF16),
                   jax.ShapeDtypeStruct((32, t), F32)],
        compiler_params=_params(("parallel",)),
        name="mid",
    )(x, att, conv, woa, woc, g, wqt, a1, a2)


def _peer_body(xn_ref, qht_ref, st_ref, a1_ref, a2_ref, u_ref, vt_ref, h_ref, g_ref, y_ref,
               s1_ref, s2_ref, e1_ref, e2_ref, hid_ref, out_ref, *, eb):
    e = pl.program_id(1)
    nsub = eb // PEER_KEYS
    half = PEER_HEADS * PEER_HALF

    @pl.when(e == 0)
    def _():
        qht = qht_ref[...]
        s1 = _dot(a1_ref[...], qht[:half])
        s2 = _dot(a2_ref[...], qht[half:])
        s1_ref[...] = s1
        s2_ref[...] = s2
        for h in range(PEER_HEADS):
            r = slice(h * PEER_KEYS, (h + 1) * PEER_KEYS)
            e1_ref[r, :] = jnp.exp(s1[r] - st_ref[h:h + 1, :])
            e2_ref[r, :] = jnp.exp(s2[r] - st_ref[8 + h:9 + h, :]) * st_ref[24 + h:25 + h, :] * 0.5
        out_ref[...] = jnp.zeros(out_ref.shape, F32)

    xn = xn_ref[...]
    for ii in range(nsub):
        i = e * nsub + ii
        a = _nt(u_ref[ii * PEER_KEYS:(ii + 1) * PEER_KEYS, :], xn)
        gate = None
        for h in range(PEER_HEADS):
            r = slice(h * PEER_KEYS, (h + 1) * PEER_KEYS)
            s1r = s1_ref[pl.ds(h * PEER_KEYS + i, 1), :]
            thr = st_ref[16 + h:17 + h, :]
            bound = (thr - s1r) - (jnp.abs(thr) + jnp.abs(s1r)) * PAIR_SLACK - PAIR_SLACK_ABS
            e2_min = jnp.exp(bound - st_ref[8 + h:9 + h, :]) * st_ref[24 + h:25 + h, :] * 0.5
            e2 = e2_ref[r, :]
            g = jnp.where(e2 >= e2_min, e1_ref[pl.ds(h * PEER_KEYS + i, 1), :] * e2, 0.0)
            gate = g if gate is None else gate + g
        hid = a * (1.0 + lax.erf(a * INV_SQRT2)) * gate
        hid_ref[ii * PEER_KEYS:(ii + 1) * PEER_KEYS, :] = hid.astype(BF16)
    out_ref[...] += _dot(vt_ref[...], hid_ref[...])

    @pl.when(e == pl.num_programs(1) - 1)
    def _():
        y_ref[...] = _rms(h_ref[...] + out_ref[...].T, g_ref[...])


def _peer(xn, qht, st, a1, a2, u, vt, h, g, tb, eb):
    t = xn.shape[0]
    ne = u.shape[0] // eb
    full = lambda a: pl.BlockSpec(a.shape, lambda i, e: (0, 0))
    return pl.pallas_call(
        functools.partial(_peer_body, eb=eb),
        grid=(t // tb, ne),
        in_specs=[pl.BlockSpec((tb, D_MODEL), lambda i, e: (i, 0)),
                  pl.BlockSpec((D_MODEL, tb), lambda i, e: (0, i)),
                  pl.BlockSpec((32, tb), lambda i, e: (0, i)),
                  full(a1), full(a2),
                  pl.BlockSpec((eb, D_MODEL), lambda i, e: (e, 0)),
                  pl.BlockSpec((D_MODEL, eb), lambda i, e: (0, e)),
                  pl.BlockSpec((tb, D_MODEL), lambda i, e: (i, 0)),
                  full(g)],
        out_specs=pl.BlockSpec((tb, D_MODEL), lambda i, e: (i, 0)),
        out_shape=jax.ShapeDtypeStruct((t, D_MODEL), F32),
        scratch_shapes=[pltpu.VMEM((PEER_HEADS * PEER_KEYS, tb), F32)] * 4
                       + [pltpu.VMEM((eb, tb), BF16), pltpu.VMEM((D_MODEL, tb), F32)],
        compiler_params=_params(("parallel", "arbitrary")),
        name="peer",
    )(xn, qht, st, a1, a2, u, vt, h, g)


def _pick(n, pref):
    b = min(n, pref)
    while n % b:
        b //= 2
    return b


def _token_stage(x, att, conv, p, tb_mid, tb_peer):
    h, xn, qht, st = _mid(x, att, conv, p["woa"], p["woc"], p["ffn_g"], p["wqt"], p["a1s"], p["a2s"], tb_mid)
    return _peer(xn, qht, st, p["a1d"], p["a2d"], p["u"], p["vt"], h, p["final_g"], tb_peer, 2048)


def kernel(x_prompt, x_sample, cache_k, cache_v, cache_kidx, state_conv, attn_norm_g, w_in, conv_w, conv_b,
           conv_ln_g, conv_ln_b, w_out, ffn_norm_g, peer_wq, peer_subkeys, peer_u, peer_v, final_norm_g):
    assert w_in.shape[0] == 1, "single layer"
    nbp, s, _ = x_prompt.shape
    nb, tn, _ = x_sample.shape
    assert nbp == 1
    past = cache_k.shape[2]

    wi = w_in[0]
    o_qi = 3 * ATT_WIDTH
    o_ki = o_qi + N_IDX_HEADS * IDX_DIM
    o_wi = o_ki + IDX_DIM
    o_glu = o_wi + N_IDX_HEADS
    wqkv = wi[:, :o_qi].astype(BF16)
    widx = jnp.concatenate([wi[:, o_qi:o_glu], jnp.zeros((D_MODEL, 128 - IDX_DIM - N_IDX_HEADS), F32)],
                           axis=1).astype(BF16)
    wglu = wi[:, o_glu:].astype(BF16)
    g_attn = attn_norm_g[0][None, :]
    cw = jnp.concatenate([conv_w[0], jnp.zeros((CONV_TAIL - CONV_WIDTH, CONV_CH), F32)], axis=0)
    cb, cg, cbeta = conv_b[0][None, :], conv_ln_g[0][None, :], conv_ln_b[0][None, :]
    eye = jnp.eye(PEER_HEADS, dtype=F32)
    sk = peer_subkeys[0]
    a_s = [jnp.einsum("id,hg->ihgd", sk[c], eye).reshape(PEER_KEYS * PEER_HEADS, PEER_HEADS * PEER_HALF)
           .astype(BF16) for c in range(2)]
    a_d = [jnp.einsum("id,hg->higd", sk[c], eye).reshape(PEER_KEYS * PEER_HEADS, PEER_HEADS * PEER_HALF)
           .astype(BF16) for c in range(2)]
    wq = peer_wq[0].reshape(D_MODEL, PEER_HEADS, 2, PEER_HALF).transpose(2, 1, 3, 0)
    p = {
        "woa": w_out[0][:ATT_WIDTH].astype(BF16), "woc": w_out[0][ATT_WIDTH:].astype(BF16),
        "ffn_g": ffn_norm_g[0][None, :], "final_g": final_norm_g[None, :],
        "wqt": wq.reshape(D_MODEL, D_MODEL).astype(BF16),
        "a1s": a_s[0], "a2s": a_s[1], "a1d": a_d[0], "a2d": a_d[1],
        "u": peer_u[0].astype(BF16), "vt": peer_v[0].T.astype(BF16),
    }
    w_scale = (N_IDX_HEADS ** -0.5) * (IDX_DIM ** -0.5)

    xp = x_prompt[0]
    tb = _pick(s, 512)
    k, v, kw, q_hm, k_hm, vt_hm, qi_hm, conv_p, tail_p = _in_proj_conv(
        xp, g_attn, wqkv, widx, wglu, jnp.zeros((1, CONV_TAIL, CONV_CH), F32), cw, cb, cg, cbeta, tb)
    ki = kw[:, :IDX_DIM]
    ki_b = ki.astype(BF16)
    w_t = (kw[:, IDX_DIM:IDX_DIM + N_IDX_HEADS] * w_scale).T
    tq = _pick(s, 256)
    sel = _idx_sel(qi_hm, w_t, ki_b, tq)
    att_p = _attn(q_hm, k_hm, vt_hm, sel, tq, _pick(s, 1024))
    y_p = _token_stage(xp, att_p, conv_p, p, tb, _pick(s, 512))

    xs = x_sample.reshape(nb * tn, D_MODEL)
    ts = nb * tn
    tbs = _pick(ts, 256)
    ks, vs, kws, glus, qs_hm, _, _, qis_hm = _in_proj(xs, g_attn, wqkv, widx, wglu, tbs)
    pad = jnp.zeros((nb, CONV_TAIL - (CONV_WIDTH - 1), CONV_CH), F32)
    conv_s, tail_s = _conv(glus, jnp.concatenate([pad, state_conv[0]], axis=1), cw, cb, cg, cbeta, nb, tn)
    hm4 = lambda a, nh, dt=BF16: a.reshape(nb, tn, nh, a.shape[1] // nh).transpose(0, 2, 1, 3).astype(dt)
    per_stream = lambda a: a.reshape(a.shape[0], nb, tn, a.shape[2]).transpose(1, 0, 2, 3)
    kis = kws[:, :IDX_DIM]
    att_s = _sample_attn(
        per_stream(qs_hm), per_stream(qis_hm),
        (kws[:, IDX_DIM:IDX_DIM + N_IDX_HEADS] * w_scale).reshape(nb, tn, N_IDX_HEADS),
        cache_kidx[0], kis.reshape(nb, tn, IDX_DIM),
        cache_k[0].reshape(nb, past, ATT_WIDTH), cache_v[0].reshape(nb, past, ATT_WIDTH),
        hm4(ks, N_HEADS), hm4(vs, N_HEADS))
    att_s = att_s.transpose(0, 2, 1, 3).reshape(ts, ATT_WIDTH)
    y_s = _token_stage(xs, att_s, conv_s, p, tbs, tbs)

    hd = (N_HEADS, HEAD_DIM)
    keep = CONV_TAIL - (CONV_WIDTH - 1)
    return (y_p[None], y_s.reshape(nb, tn, D_MODEL),
            k.reshape(1, 1, s, *hd), v.reshape(1, 1, s, *hd), ki[None, None], tail_p[None, :, keep:],
            ks.reshape(1, nb, tn, *hd), vs.reshape(1, nb, tn, *hd), kis.reshape(1, nb, tn, IDX_DIM),
            tail_s[None, :, keep:])
```
